```python
import jax
import jax.numpy as jnp
from jax import lax
import numpy as np

D_MODEL = 1024
BATCH = 8
SEQ = 4096
DEPTH = 1

MIX_WIDTH = D_MODEL
ATT_WIDTH = MIX_WIDTH // 2
ATT_HEAD_DIM = 64
ATT_HEADS = ATT_WIDTH // ATT_HEAD_DIM
ROT_DIM = ATT_HEAD_DIM // 4
ROPE_THETA = 500000.0
MOBA_BLOCK = 256
MOBA_TOPK = 3
Q_CHUNK = 128
MLSTM_WIDTH = MIX_WIDTH - ATT_WIDTH
MLSTM_HEADS = 4
MLSTM_HEAD_DIM = MLSTM_WIDTH // MLSTM_HEADS
MLSTM_CHUNK = 64
CONV_WIDTH = 4
NORM_EPS = 1e-6
IN_SPLITS = (ATT_WIDTH,) * 4 + (MLSTM_WIDTH,) * 5 + (MLSTM_HEADS,) * 2

kernel_name = 'hybrid_moba_mlstm_block'


def rms_norm(x, g):
    xf = x.astype(jnp.float32)
    xf = xf * lax.rsqrt(jnp.mean(xf * xf, axis=-1, keepdims=True) + NORM_EPS)
    return xf.astype(x.dtype) * g


def split_columns(proj):
    idx = []
    acc = 0
    for s in IN_SPLITS[:-1]:
        acc += s
        idx.append(acc)
    return jnp.split(proj, idx, axis=-1)


def partial_rope(x, pos):
    half = ROT_DIM // 2
    inv_freq = jnp.power(ROPE_THETA, -jnp.arange(half, dtype=jnp.float32) * 2.0 / ROT_DIM)
    ang = pos.astype(jnp.float32)[:, None] * inv_freq[None, :]
    cos = jnp.cos(ang)[None, :, None, :].astype(x.dtype)
    sin = jnp.sin(ang)[None, :, None, :].astype(x.dtype)
    x1 = x[..., :half]
    x2 = x[..., half:ROT_DIM]
    return jnp.concatenate([x1 * cos - x2 * sin, x2 * cos + x1 * sin, x[..., ROT_DIM:]], axis=-1)


def causal_depthwise_conv(u, w, b):
    K = w.shape[0]
    S = u.shape[1]
    up = jnp.pad(u, ((0, 0), (K - 1, 0), (0, 0)))
    out = b
    for j in range(K):
        out = out + w[j] * up[:, j:j + S]
    return out


def moba_attention(q, k, v):
    B, S, H, dh = q.shape
    s_pad = -(-S // MOBA_BLOCK) * MOBA_BLOCK
    padw = ((0, 0), (0, s_pad - S), (0, 0), (0, 0))
    q = jnp.pad(q, padw) * (dh ** -0.5)
    k = jnp.pad(k, padw)
    v = jnp.pad(v, padw)
    nb = s_pad // MOBA_BLOCK
    nqc = s_pad // Q_CHUNK
    topk = min(MOBA_TOPK, nb)
    qc_per_block = MOBA_BLOCK // Q_CHUNK
    kb = k.reshape(B, nb, MOBA_BLOCK, H, dh).transpose(0, 3, 1, 2, 4)
    vb = v.reshape(B, nb, MOBA_BLOCK, H, dh).transpose(0, 3, 1, 2, 4)
    kmean = jnp.mean(kb.astype(jnp.float32), axis=3).astype(q.dtype)
    q_chunks = q.reshape(B, nqc, Q_CHUNK, H, dh).transpose(0, 1, 3, 2, 4).reshape(B * nqc, H, Q_CHUNK, dh)
    b_idx = jnp.repeat(jnp.arange(B, dtype=jnp.int32), nqc)
    c_idx = jnp.tile(jnp.arange(nqc, dtype=jnp.int32), B)
    h_ix = jnp.arange(H)[:, None, None]
    blk_ids = jnp.arange(nb)

    def one_chunk(args):
        q_c, b, c = args
        cur = c // qc_per_block
        k_b = kb[b]
        v_b = vb[b]
        gate = jnp.einsum('hqd,hnd->hqn', q_c, kmean[b]).astype(jnp.float32)
        gate = jnp.where(blk_ids < cur, gate, -jnp.inf)
        _, idx = lax.top_k(gate, topk)
        valid = idx < cur
        k_sel = k_b[h_ix, idx]
        v_sel = v_b[h_ix, idx]
        s_sel = jnp.einsum('hqd,hqjld->hqjl', q_c, k_sel).astype(jnp.float32)
        s_sel = jnp.where(valid[..., None], s_sel, -jnp.inf).reshape(H, Q_CHUNK, topk * MOBA_BLOCK)
        k_own = lax.dynamic_index_in_dim(k_b, cur, axis=1, keepdims=False)
        v_own = lax.dynamic_index_in_dim(v_b, cur, axis=1, keepdims=False)
        s_own = jnp.einsum('hqd,hld->hql', q_c, k_own).astype(jnp.float32)
        q_pos = c * Q_CHUNK + jnp.arange(Q_CHUNK)
        k_pos = cur * MOBA_BLOCK + jnp.arange(MOBA_BLOCK)
        s_own = jnp.where(k_pos[None, :] <= q_pos[:, None], s_own, -jnp.inf)
        p = jax.nn.softmax(jnp.concatenate([s_sel, s_own], axis=-1), axis=-1).astype(q_c.dtype)
        p_sel = p[..., :topk * MOBA_BLOCK].reshape(H, Q_CHUNK, topk, MOBA_BLOCK)
        p_own = p[..., topk * MOBA_BLOCK:]
        return (jnp.einsum('hqjl,hqjld->hqd', p_sel, v_sel)
                + jnp.einsum('hql,hld->hqd', p_own, v_own))

    out = lax.map(one_chunk, (q_chunks, b_idx, c_idx))
    out = out.reshape(B, nqc, H, Q_CHUNK, dh).transpose(0, 1, 3, 2, 4).reshape(B, s_pad, H * dh)
    return out[:, :S]


def mlstm_chunkwise(q, k, v, ig, fg):
    B, H, S, d = q.shape
    L = MLSTM_CHUNK
    nc = S // L
    q, k, v = (t.astype(jnp.float32) for t in (q, k, v))
    ig = ig.astype(jnp.float32)
    lf = jax.nn.log_sigmoid(fg.astype(jnp.float32))
    to_chunks = lambda t: jnp.moveaxis(t.reshape(B, H, nc, L, *t.shape[3:]), 2, 0)
    xs = (to_chunks(q), to_chunks(k), to_chunks(v), to_chunks(ig), to_chunks(lf))
    causal = jnp.tril(jnp.ones((L, L), dtype=bool))

    def step(carry, inp):
        C, n, m = carry
        qc, kc, vc, ic, fc = inp
        bcum = jnp.cumsum(fc, axis=-1)
        D = jnp.where(causal, bcum[..., :, None] - bcum[..., None, :] + ic[..., None, :], -jnp.inf)
        inter = bcum + m[..., None]
        m_t = jnp.maximum(inter, jnp.max(D, axis=-1))
        w_inter = jnp.exp(inter - m_t)
        s = jnp.einsum('bhtd,bhsd->bhts', qc, kc) * jnp.exp(D - m_t[..., None])
        num = (w_inter[..., None] * jnp.einsum('bhvd,bhtd->bhtv', C, qc)
               + jnp.einsum('bhts,bhsv->bhtv', s, vc))
        nq = w_inter * jnp.einsum('bhd,bhtd->bht', n, qc) + jnp.sum(s, axis=-1)
        h = num / jnp.maximum(jnp.abs(nq), jnp.exp(-m_t))[..., None]
        b_last = bcum[..., -1]
        g = b_last[..., None] - bcum + ic
        m_new = jnp.maximum(b_last + m, jnp.max(g, axis=-1))
        decay = jnp.exp(b_last + m - m_new)
        w = jnp.exp(g - m_new[..., None])
        C = decay[..., None, None] * C + jnp.einsum('bhs,bhsv,bhsd->bhvd', w, vc, kc)
        n = decay[..., None] * n + jnp.einsum('bhs,bhsd->bhd', w, kc)
        return (C, n, m_new), h

    init = (jnp.zeros((B, H, d, d), jnp.float32), jnp.zeros((B, H, d), jnp.float32),
            jnp.zeros((B, H), jnp.float32))
    _, hs = lax.scan(step, init, xs)
    return jnp.moveaxis(hs, 0, 2).reshape(B, H, S, d)


def setup_inputs(seed: int = 0) -> dict:
    key = jax.random.key(seed)
    ks = jax.random.split(key, 12)
    f32 = jnp.float32
    nrm = lambda k, shape: jax.random.normal(k, shape, f32)
    in_cols = sum(IN_SPLITS)
    return {
        'x': nrm(ks[0], (BATCH, SEQ, D_MODEL)),
        'pre_norm_g': 1.0 + 0.05 * nrm(ks[1], (DEPTH, D_MODEL)),
        'w_in': nrm(ks[2], (DEPTH, D_MODEL, in_cols)) * D_MODEL ** -0.5,
        'mlstm_i_bias': 0.1 * nrm(ks[3], (DEPTH, MLSTM_HEADS)),
        'mlstm_f_bias': jnp.linspace(3.0, 6.0, MLSTM_HEADS, dtype=f32)[None, :] + 0.01 * nrm(ks[4], (DEPTH, MLSTM_HEADS)),
        'conv_w': nrm(ks[5], (DEPTH, CONV_WIDTH, 2 * MLSTM_WIDTH)) * CONV_WIDTH ** -0.5,
        'conv_b': 0.01 * nrm(ks[6], (DEPTH, 2 * MLSTM_WIDTH)),
        'attn_out_g': 1.0 + 0.05 * nrm(ks[7], (DEPTH, ATT_WIDTH)),
        'mlstm_out_g': 1.0 + 0.05 * nrm(ks[8], (DEPTH, MLSTM_WIDTH)),
        'w_out': nrm(ks[9], (DEPTH, MIX_WIDTH, D_MODEL)) * MIX_WIDTH ** -0.5,
        'post_norm_g': 1.0 + 0.05 * nrm(ks[10], (DEPTH, D_MODEL)),
    }


def reference(x, pre_norm_g, w_in, mlstm_i_bias, mlstm_f_bias, conv_w, conv_b,
              attn_out_g, mlstm_out_g, w_out, post_norm_g):
    B, S, _ = x.shape
    pos = jnp.arange(S, dtype=jnp.int32)
    att_heads = lambda t: t.reshape(B, S, ATT_HEADS, ATT_HEAD_DIM)
    ml_heads = lambda t: t.reshape(B, S, MLSTM_HEADS, MLSTM_HEAD_DIM).transpose(0, 2, 1, 3)
    for l in range(DEPTH):
        h = rms_norm(x, pre_norm_g[l])
        proj = jnp.einsum('bsd,dc->bsc', h, w_in[l])
        aq, ak, av, az, mq, mk, mv, mo, mz, mi, mf = split_columns(proj)

        ya = moba_attention(partial_rope(att_heads(aq), pos), partial_rope(att_heads(ak), pos), att_heads(av))
        ya = rms_norm(ya, attn_out_g[l]) * jax.nn.silu(az)

        qk = jax.nn.silu(causal_depthwise_conv(jnp.concatenate([mq, mk], axis=-1), conv_w[l], conv_b[l]))
        mq_c, mk_c = jnp.split(qk, 2, axis=-1)
        hm = mlstm_chunkwise(ml_heads(mq_c), ml_heads(mk_c) * MLSTM_HEAD_DIM ** -0.5, ml_heads(mv),
                             (mi + mlstm_i_bias[l]).transpose(0, 2, 1),
                             (mf + mlstm_f_bias[l]).transpose(0, 2, 1))
        mu = jnp.mean(hm, axis=-1, keepdims=True)
        var = jnp.mean(jnp.square(hm - mu), axis=-1, keepdims=True)
        hm = (hm - mu) * lax.rsqrt(var + NORM_EPS)
        hm = hm.transpose(0, 2, 1, 3).reshape(B, S, MLSTM_WIDTH).astype(x.dtype) * mlstm_out_g[l]
        ym = jax.nn.sigmoid(mo) * hm * jax.nn.silu(mz)

        y = jnp.einsum('bsc,cd->bsd', jnp.concatenate([ya, ym], axis=-1), w_out[l])
        x = x + rms_norm(y, post_norm_g[l])
    return x
```

```python
import functools

import jax
import jax.numpy as jnp
from jax import lax
from jax.experimental import pallas as pl
from jax.experimental.pallas import tpu as pltpu

F32 = jnp.float32
BF16 = jnp.bfloat16

ATT_HEADS = 8
ATT_HEAD_DIM = 64
ATT_WIDTH = ATT_HEADS * ATT_HEAD_DIM
ROT_DIM = ATT_HEAD_DIM // 4
ROPE_THETA = 500000.0
MOBA_BLOCK = 256
MOBA_TOPK = 3
MLSTM_HEADS = 4
MLSTM_HEAD_DIM = 128
MLSTM_WIDTH = MLSTM_HEADS * MLSTM_HEAD_DIM
CONV_WIDTH = 4
NORM_EPS = 1e-6

LANES = 128
SUBLANES = 8
PROJ_ROWS = 512
MLSTM_CHUNK = 256
NEG_BIG = -1e30
VMEM_LIMIT = 56 * 1024 * 1024


def _sigmoid(x):
    return 1.0 / (1.0 + jnp.exp(-x))


def _silu(x):
    return x * _sigmoid(x)


def _log_sigmoid(x):
    return jnp.minimum(x, 0.0) - jnp.log(1.0 + jnp.exp(-jnp.abs(x)))


def _split3(x):
    hi = x.astype(BF16)
    r1 = x - hi.astype(F32)
    mid = r1.astype(BF16)
    lo = (r1 - mid.astype(F32)).astype(BF16)
    return hi, mid, lo


def _dot(a, b):
    return jnp.dot(a, b, preferred_element_type=F32)


def _proj_kernel(x_ref, g_ref, w_ref, wg_ref, cos_ref, sa_ref, sb_ref,
                 q_ref, k_ref, v_ref, az_ref, mq_ref, mk_ref, mv_ref, mo_ref, mz_ref, gate_ref):
    x = x_ref[...]
    ms = jnp.mean(x * x, axis=-1, keepdims=True)
    hb = ((x * lax.rsqrt(ms + NORM_EPS)) * g_ref[...]).astype(BF16)

    cos = cos_ref[...]
    sa = sa_ref[...]
    sb = sb_ref[...]

    def rope(acc):
        parts = []
        for c in range(ATT_WIDTH // LANES):
            xs = acc[:, c * LANES:(c + 1) * LANES]
            parts.append(xs * cos
                         + pltpu.roll(xs, LANES - ROT_DIM // 2, 1) * sa
                         + pltpu.roll(xs, ROT_DIM // 2, 1) * sb)
        return jnp.concatenate(parts, axis=1)

    outs = (q_ref, k_ref, v_ref, az_ref, mq_ref, mk_ref, mv_ref, mo_ref, mz_ref)
    for gi, o_ref in enumerate(outs):
        acc = _dot(hb, w_ref[:, gi * ATT_WIDTH:(gi + 1) * ATT_WIDTH])
        if gi == 0:
            acc = rope(acc) * (ATT_HEAD_DIM ** -0.5)
        elif gi == 1:
            acc = rope(acc)
        o_ref[...] = acc.astype(o_ref.dtype)
    gate_ref[...] = _dot(hb, wg_ref[...])


def _rope_tables(seq):
    half = ROT_DIM // 2
    inv_freq = jnp.power(ROPE_THETA, -jnp.arange(half, dtype=F32) * 2.0 / ROT_DIM)
    ang = jnp.arange(seq, dtype=jnp.int32).astype(F32)[:, None] * inv_freq[None, :]
    cos = jnp.cos(ang)
    sin = jnp.sin(ang)
    ones = jnp.ones((seq, ATT_HEAD_DIM - ROT_DIM), F32)
    zeros = jnp.zeros((seq, ATT_HEAD_DIM - ROT_DIM), F32)
    zh = jnp.zeros((seq, half), F32)
    cos_h = jnp.concatenate([cos, cos, ones], axis=1)
    sa_h = jnp.concatenate([-sin, zh, zeros], axis=1)
    sb_h = jnp.concatenate([zh, sin, zeros], axis=1)
    rep = LANES // ATT_HEAD_DIM
    return (jnp.tile(cos_h, (1, rep)), jnp.tile(sa_h, (1, rep)), jnp.tile(sb_h, (1, rep)))


def _input_projection(x2, pre_g, w_main, w_gate, tables, seq):
    n_tok, d_model = x2.shape
    tm = PROJ_ROWS
    seq_tiles = seq // tm
    row = lambda i: (i, 0)
    fixed = lambda i: (0, 0)
    tab = lambda i: (i % seq_tiles, 0)
    wide = jax.ShapeDtypeStruct((n_tok, ATT_WIDTH), BF16)
    return pl.pallas_call(
        _proj_kernel,
        out_shape=(wide,) * 9 + (jax.ShapeDtypeStruct((n_tok, LANES), F32),),
        grid=(n_tok // tm,),
        in_specs=[
            pl.BlockSpec((tm, d_model), row),
            pl.BlockSpec((1, d_model), fixed),
            pl.BlockSpec(w_main.shape, fixed),
            pl.BlockSpec(w_gate.shape, fixed),
            pl.BlockSpec((tm, LANES), tab),
            pl.BlockSpec((tm, LANES), tab),
            pl.BlockSpec((tm, LANES), tab),
        ],
        out_specs=tuple(pl.BlockSpec((tm, ATT_WIDTH), row) for _ in range(9))
        + (pl.BlockSpec((tm, LANES), row),),
        compiler_params=pltpu.CompilerParams(
            dimension_semantics=("arbitrary",), vmem_limit_bytes=VMEM_LIMIT),
        name="input_projection",
    )(x2, pre_g, w_main, w_gate, *tables)


def _moba_kernel(qt_ref, k_ref, vt_ref, o_ref,
                 kst_hi, kst_mid, kst_lo, qta_ref, bias_ref, m_ref, l_ref, acc_ref, *, nb):
    i = pl.program_id(1)
    blk = MOBA_BLOCK
    hd = ATT_HEAD_DIM

    @pl.when(i == 0)
    def _():
        lane_head = lax.broadcasted_iota(jnp.int32, (nb, ATT_WIDTH), 1) // hd
        means = []
        for n in range(nb):
            means.append(jnp.mean(k_ref[0, n].astype(F32), axis=0, keepdims=True))
        km = jnp.concatenate(means, axis=0)
        for h in range(ATT_HEADS):
            hi, mid, lo = _split3(jnp.where(lane_head == h, km, 0.0))
            kst_hi[h * nb:(h + 1) * nb, :] = hi
            kst_mid[h * nb:(h + 1) * nb, :] = mid
            kst_lo[h * nb:(h + 1) * nb, :] = lo

    qt = qt_ref[0, 0]
    gates = _dot(kst_hi[...], qt) + _dot(kst_mid[...], qt) + _dot(kst_lo[...], qt)

    n_iota = lax.broadcasted_iota(jnp.int32, (nb, blk), 0)
    past = n_iota < i
    row_head = lax.broadcasted_iota(jnp.int32, (LANES, blk), 0) // hd
    for h in range(ATT_HEADS):
        g = jnp.where(past, gates[h * nb:(h + 1) * nb, :], -jnp.inf)
        rank = jnp.zeros((nb, blk), F32)
        for mm in range(nb):
            gm = g[mm:mm + 1, :]
            beats = (gm > g) | ((gm == g) & (n_iota > mm))
            rank = rank + jnp.where(beats, 1.0, 0.0)
        sel = past & (rank < float(MOBA_TOPK))
        bias_ref[h] = jnp.where(sel, 0.0, NEG_BIG)
        p = h // 2
        qpair = qt[p * LANES:(p + 1) * LANES, :]
        qta_ref[h] = jnp.where(row_head == (h % 2), qpair, jnp.zeros_like(qpair))

    kb = k_ref[0, i]
    vtb = vt_ref[0, i]
    l_iota = lax.broadcasted_iota(jnp.int32, (blk, blk), 0)
    q_iota = lax.broadcasted_iota(jnp.int32, (blk, blk), 1)
    causal = l_iota <= q_iota
    for h in range(ATT_HEADS):
        p = h // 2
        st = _dot(kb[:, p * LANES:(p + 1) * LANES], qta_ref[h])
        st = jnp.where(causal, st, NEG_BIG)
        m0 = jnp.max(st, axis=0, keepdims=True)
        pt = jnp.exp(st - m0)
        m_ref[h] = m0
        l_ref[h] = jnp.sum(pt, axis=0, keepdims=True)
        acc_ref[h] = _dot(vtb[h * hd:(h + 1) * hd, :], pt.astype(BF16))

    def body(n, carry):
        kb = k_ref[0, n]
        vtb = vt_ref[0, n]
        for h in range(ATT_HEADS):
            p = h // 2
            st = _dot(kb[:, p * LANES:(p + 1) * LANES], qta_ref[h])
            b = bias_ref[h, pl.ds(n, 1), :]
            m_old = m_ref[h]
            m_new = jnp.maximum(m_old, jnp.max(st, axis=0, keepdims=True) + b)
            pt = jnp.exp(st - (m_new - b))
            alpha = jnp.exp(m_old - m_new)
            l_ref[h] = alpha * l_ref[h] + jnp.sum(pt, axis=0, keepdims=True)
            acc_ref[h] = alpha * acc_ref[h] + _dot(vtb[h * hd:(h + 1) * hd, :], pt.astype(BF16))
            m_ref[h] = m_new
        return carry

    lax.fori_loop(0, i, body, 0)

    for h in range(ATT_HEADS):
        o_ref[0, 0, h * hd:(h + 1) * hd, :] = (acc_ref[h] / l_ref[h]).astype(o_ref.dtype)


def _moba_attention(qt4, k4, vt4):
    bsz, nb, width, blk = qt4.shape
    hn = ATT_HEADS * nb
    return pl.pallas_call(
        functools.partial(_moba_kernel, nb=nb),
        out_shape=jax.ShapeDtypeStruct((bsz, nb, width, blk), BF16),
        grid=(bsz, nb),
        in_specs=[
            pl.BlockSpec((1, 1, width, blk), lambda b, i: (b, i, 0, 0)),
            pl.BlockSpec((1, nb, blk, width), lambda b, i: (b, 0, 0, 0)),
            pl.BlockSpec((1, nb, width, blk), lambda b, i: (b, 0, 0, 0)),
        ],
        out_specs=pl.BlockSpec((1, 1, width, blk), lambda b, i: (b, i, 0, 0)),
        scratch_shapes=[
            pltpu.VMEM((hn, width), BF16),
            pltpu.VMEM((hn, width), BF16),
            pltpu.VMEM((hn, width), BF16),
            pltpu.VMEM((ATT_HEADS, LANES, blk), BF16),
            pltpu.VMEM((ATT_HEADS, nb, blk), F32),
            pltpu.VMEM((ATT_HEADS, 1, blk), F32),
            pltpu.VMEM((ATT_HEADS, 1, blk), F32),
            pltpu.VMEM((ATT_HEADS, ATT_HEAD_DIM, blk), F32),
        ],
        compiler_params=pltpu.CompilerParams(
            dimension_semantics=("arbitrary", "arbitrary"), vmem_limit_bytes=VMEM_LIMIT),
        name="moba_attention",
    )(qt4, k4, vt4)


def _lane_col(x, c):
    lane = lax.broadcasted_iota(jnp.int32, x.shape, 1)
    return jnp.sum(jnp.where(lane == c, x, 0.0), axis=-1, keepdims=True)


def _mlstm_kernel(mq_ref, mk_ref, mv_ref, mo_ref, mz_ref, gcol_ref, grow_ref,
                  bcol_ref, brow_ref, cw_ref, cb_ref, og_ref, tril_ref, triu_ref,
                  o_ref, ct_ref, n_ref, m_ref, pq_ref, pk_ref):
    c = pl.program_id(1)
    L = MLSTM_CHUNK
    hd = MLSTM_HEAD_DIM
    halo = SUBLANES

    @pl.when(c == 0)
    def _():
        ct_ref[...] = jnp.zeros_like(ct_ref)
        n_ref[...] = jnp.zeros_like(n_ref)
        m_ref[...] = jnp.zeros_like(m_ref)
        pq_ref[...] = jnp.zeros_like(pq_ref)
        pk_ref[...] = jnp.zeros_like(pk_ref)

    def conv_silu(u_ref, prev_ref, w, b):
        u = u_ref[0].astype(F32)
        ext = jnp.concatenate([prev_ref[...], u], axis=0)
        prev_ref[...] = u[L - halo:, :]
        out = b
        for j in range(CONV_WIDTH):
            off = halo - (CONV_WIDTH - 1) + j
            out = out + w[j:j + 1, :] * ext[off:off + L, :]
        return _silu(out)

    cw = cw_ref[...]
    cb = cb_ref[...]
    qc = conv_silu(mq_ref, pq_ref, cw[:, :MLSTM_WIDTH], cb[:, :MLSTM_WIDTH])
    kc = conv_silu(mk_ref, pk_ref, cw[:, MLSTM_WIDTH:], cb[:, MLSTM_WIDTH:]) * (hd ** -0.5)

    gc = gcol_ref[0] + bcol_ref[...]
    gr = grow_ref[0] + brow_ref[...]
    lf_c = _log_sigmoid(gc)
    lf_r = _log_sigmoid(gr)
    tril = tril_ref[...]
    triu = triu_ref[...]
    c_hi, c_mid, c_lo = _split3(lf_c)
    bcum_c = _dot(tril, c_hi) + _dot(tril, c_mid) + _dot(tril, c_lo)
    r_hi, r_mid, r_lo = _split3(lf_r)
    bcum_r = _dot(r_hi, triu) + _dot(r_mid, triu) + _dot(r_lo, triu)

    t_iota = lax.broadcasted_iota(jnp.int32, (L, L), 0)
    s_iota = lax.broadcasted_iota(jnp.int32, (L, L), 1)
    causal = s_iota <= t_iota

    og = og_ref[...]
    for h in range(MLSTM_HEADS):
        sl = slice(h * hd, (h + 1) * hd)
        q = qc[:, sl]
        k = kc[:, sl]
        qb = q.astype(BF16)
        kb = k.astype(BF16)
        v = mv_ref[0, :, sl]
        b_col = _lane_col(bcum_c, MLSTM_HEADS + h)
        i_col = _lane_col(gc, h)
        b_row = bcum_r[MLSTM_HEADS + h:MLSTM_HEADS + h + 1, :]
        i_row = gr[h:h + 1, :]
        m_prev = m_ref[h][:, 0:1]
        n_prev = n_ref[h]
        ct = ct_ref[h]

        dmat = jnp.where(causal, b_col - b_row + i_row, -jnp.inf)
        inter = b_col + m_prev
        m_t = jnp.maximum(inter, jnp.max(dmat, axis=-1, keepdims=True))
        w_inter = jnp.exp(inter - m_t)
        qk = lax.dot_general(qb, kb, (((1,), (1,)), ((), ())), preferred_element_type=F32)
        s = qk * jnp.exp(dmat - m_t)
        num = w_inter * _dot(qb, ct.astype(BF16)) + _dot(s.astype(BF16), v)
        nq = (w_inter * jnp.sum(q * n_prev, axis=-1, keepdims=True)
              + jnp.sum(s, axis=-1, keepdims=True))
        hh = num / jnp.maximum(jnp.abs(nq), jnp.exp(-m_t))

        b_last = b_col[L - 1:L, :]
        g = b_last - b_col + i_col
        m_new = jnp.maximum(b_last + m_prev, jnp.max(g, axis=0, keepdims=True))
        decay = jnp.exp(b_last + m_prev - m_new)
        w = jnp.exp(g - m_new)
        wv = (w * v.astype(F32)).astype(BF16)
        ct_ref[h] = decay * ct + lax.dot_general(
            kb, wv, (((0,), (0,)), ((), ())), preferred_element_type=F32)
        n_ref[h] = decay * n_prev + jnp.sum(w * k, axis=0, keepdims=True)
        m_ref[h] = jnp.broadcast_to(m_new, (1, LANES))

        mu = jnp.mean(hh, axis=-1, keepdims=True)
        var = jnp.mean(jnp.square(hh - mu), axis=-1, keepdims=True)
        hn = (hh - mu) * lax.rsqrt(var + NORM_EPS) * og[:, sl]
        ym = _sigmoid(mo_ref[0, :, sl].astype(F32)) * hn * _silu(mz_ref[0, :, sl].astype(F32))
        o_ref[0, :, sl] = ym.astype(o_ref.dtype)


def _mlstm(mq, mk, mv, mo, mz, gcol, grow, bias_col, bias_row, conv_w, conv_b, out_g):
    bsz, seq, width = mq.shape
    L = MLSTM_CHUNK
    tril = jnp.tril(jnp.ones((L, L), F32)).astype(BF16)
    triu = jnp.triu(jnp.ones((L, L), F32)).astype(BF16)
    tok = lambda b, c: (b, c, 0)
    fixed = lambda b, c: (0, 0)
    wide = pl.BlockSpec((1, L, width), tok)
    return pl.pallas_call(
        _mlstm_kernel,
        out_shape=jax.ShapeDtypeStruct((bsz, seq, width), BF16),
        grid=(bsz, seq // L),
        in_specs=[
            wide, wide, wide, wide, wide,
            pl.BlockSpec((1, L, LANES), tok),
            pl.BlockSpec((1, SUBLANES, L), lambda b, c: (b, 0, c)),
            pl.BlockSpec((1, LANES), fixed),
            pl.BlockSpec((SUBLANES, 1), fixed),
            pl.BlockSpec(conv_w.shape, fixed),
            pl.BlockSpec(conv_b.shape, fixed),
            pl.BlockSpec(out_g.shape, fixed),
            pl.BlockSpec((L, L), fixed),
            pl.BlockSpec((L, L), fixed),
        ],
        out_specs=wide,
        scratch_shapes=[
            pltpu.VMEM((MLSTM_HEADS, MLSTM_HEAD_DIM, MLSTM_HEAD_DIM), F32),
            pltpu.VMEM((MLSTM_HEADS, 1, MLSTM_HEAD_DIM), F32),
            pltpu.VMEM((MLSTM_HEADS, 1, LANES), F32),
            pltpu.VMEM((SUBLANES, width), F32),
            pltpu.VMEM((SUBLANES, width), F32),
        ],
        compiler_params=pltpu.CompilerParams(
            dimension_semantics=("arbitrary", "arbitrary"), vmem_limit_bytes=VMEM_LIMIT),
        name="mlstm",
    )(mq, mk, mv, mo, mz, gcol, grow, bias_col, bias_row, conv_w, conv_b, out_g, tril, triu)


def _out_kernel(x_ref, o_ref, az_ref, ym_ref, ag_ref, wa_ref, wm_ref, pg_ref, out_ref):
    o = o_ref[...].astype(F32)
    ms = jnp.mean(o * o, axis=-1, keepdims=True)
    ya = (o * lax.rsqrt(ms + NORM_EPS)) * ag_ref[...] * _silu(az_ref[...].astype(F32))
    y = _dot(ya.astype(BF16), wa_ref[...]) + _dot(ym_ref[...], wm_ref[...])
    ms2 = jnp.mean(y * y, axis=-1, keepdims=True)
    out_ref[...] = x_ref[...] + (y * lax.rsqrt(ms2 + NORM_EPS)) * pg_ref[...]


def _output_projection(x2, o2, az, ym, attn_g, w_a, w_m, post_g):
    n_tok, d_model = x2.shape
    tm = PROJ_ROWS
    row = lambda i: (i, 0)
    fixed = lambda i: (0, 0)
    return pl.pallas_call(
        _out_kernel,
        out_shape=jax.ShapeDtypeStruct((n_tok, d_model), F32),
        grid=(n_tok // tm,),
        in_specs=[
            pl.BlockSpec((tm, d_model), row),
            pl.BlockSpec((tm, ATT_WIDTH), row),
            pl.BlockSpec((tm, ATT_WIDTH), row),
            pl.BlockSpec((tm, MLSTM_WIDTH), row),
            pl.BlockSpec((1, ATT_WIDTH), fixed),
            pl.BlockSpec(w_a.shape, fixed),
            pl.BlockSpec(w_m.shape, fixed),
            pl.BlockSpec((1, d_model), fixed),
        ],
        out_specs=pl.BlockSpec((tm, d_model), row),
        compiler_params=pltpu.CompilerParams(
            dimension_semantics=("arbitrary",), vmem_limit_bytes=VMEM_LIMIT),
        name="output_projection",
    )(x2, o2, az, ym, attn_g, w_a, w_m, post_g)


def _layer(x, pre_g, w_in, i_bias, f_bias, conv_w, conv_b, attn_g, mlstm_g, w_out, post_g):
    bsz, seq, d_model = x.shape
    assert seq % MOBA_BLOCK == 0 and seq % PROJ_ROWS == 0 and seq % MLSTM_CHUNK == 0
    nb = seq // MOBA_BLOCK
    n_tok = bsz * seq
    n_main = 4 * ATT_WIDTH + 5 * MLSTM_WIDTH
    n_gate = 2 * MLSTM_HEADS
    assert w_in.shape == (d_model, n_main + n_gate)

    w_main = w_in[:, :n_main].astype(BF16)
    w_gate = jnp.pad(w_in[:, n_main:], ((0, 0), (0, LANES - n_gate))).astype(BF16)
    x2 = x.reshape(n_tok, d_model)
    (q, k, v, az, mq, mk, mv, mo, mz, gates) = _input_projection(
        x2, pre_g[None, :], w_main, w_gate, _rope_tables(seq), seq)

    to_blocks_t = lambda t: t.reshape(bsz, nb, MOBA_BLOCK, ATT_WIDTH).transpose(0, 1, 3, 2)
    ot4 = _moba_attention(to_blocks_t(q), k.reshape(bsz, nb, MOBA_BLOCK, ATT_WIDTH), to_blocks_t(v))
    o2 = ot4.transpose(0, 1, 3, 2).reshape(n_tok, ATT_WIDTH)

    gcol = gates.reshape(bsz, seq, LANES)
    grow = gcol[:, :, :SUBLANES].transpose(0, 2, 1)
    gate_bias = jnp.concatenate([i_bias, f_bias])
    bias_col = jnp.pad(gate_bias, (0, LANES - n_gate))[None, :]
    bias_row = gate_bias[:, None]
    shp = (bsz, seq, MLSTM_WIDTH)
    ym = _mlstm(mq.reshape(shp), mk.reshape(shp), mv.reshape(shp), mo.reshape(shp), mz.reshape(shp),
                gcol, grow, bias_col, bias_row, conv_w, conv_b[None, :], mlstm_g[None, :])

    w_out_b = w_out.astype(BF16)
    out = _output_projection(x2, o2, az, ym.reshape(n_tok, MLSTM_WIDTH), attn_g[None, :],
                             w_out_b[:ATT_WIDTH], w_out_b[ATT_WIDTH:], post_g[None, :])
    return out.reshape(bsz, seq, d_model)


def kernel(x, pre_norm_g, w_in, mlstm_i_bias, mlstm_f_bias, conv_w, conv_b, attn_out_g, mlstm_out_g,
           w_out, post_norm_g):
    for l in range(pre_norm_g.shape[0]):
        x = _layer(x, pre_norm_g[l], w_in[l], mlstm_i_bias[l], mlstm_f_bias[l], conv_w[l], conv_b[l],
                   attn_out_g[l], mlstm_out_g[l], w_out[l], post_norm_g[l])
    return x
```

```python
import functools

import jax
import jax.numpy as jnp
from jax import lax
from jax.experimental import pallas as pl
from jax.experimental.pallas import tpu as pltpu

F32 = jnp.float32
BF16 = jnp.bfloat16

ATT_HEADS = 8
ATT_HEAD_DIM = 64
ATT_WIDTH = ATT_HEADS * ATT_HEAD_DIM
ROT_DIM = ATT_HEAD_DIM // 4
ROPE_THETA = 500000.0
MOBA_BLOCK = 256
MOBA_TOPK = 3
MLSTM_HEADS = 4
MLSTM_HEAD_DIM = 128
MLSTM_WIDTH = MLSTM_HEADS * MLSTM_HEAD_DIM
CONV_WIDTH = 4
NORM_EPS = 1e-6

LANES = 128
SUBLANES = 8
PROJ_ROWS = 512
MLSTM_CHUNK = 256
NEG_BIG = -1e30
VMEM_LIMIT = 56 * 1024 * 1024


def _sigmoid(x):
    return 1.0 / (1.0 + jnp.exp(-x))


def _silu(x):
    return x * _sigmoid(x)


def _log_sigmoid(x):
    return jnp.minimum(x, 0.0) - jnp.log(1.0 + jnp.exp(-jnp.abs(x)))


def _split3(x):
    hi = x.astype(BF16)
    r1 = x - hi.astype(F32)
    mid = r1.astype(BF16)
    lo = (r1 - mid.astype(F32)).astype(BF16)
    return hi, mid, lo


def _dot(a, b):
    return jnp.dot(a, b, preferred_element_type=F32)


def _proj_kernel(x_ref, g_ref, w_ref, wg_ref, cos_ref, sa_ref, sb_ref,
                 q_ref, k_ref, v_ref, az_ref, mq_ref, mk_ref, mv_ref, mo_ref, mz_ref, gate_ref):
    x = x_ref[...]
    ms = jnp.mean(x * x, axis=-1, keepdims=True)
    hb = ((x * lax.rsqrt(ms + NORM_EPS)) * g_ref[...]).astype(BF16)

    cos = cos_ref[...]
    sa = sa_ref[...]
    sb = sb_ref[...]

    def rope(acc):
        parts = []
        for c in range(ATT_WIDTH // LANES):
            xs = acc[:, c * LANES:(c + 1) * LANES]
            parts.append(xs * cos
                         + pltpu.roll(xs, LANES - ROT_DIM // 2, 1) * sa
                         + pltpu.roll(xs, ROT_DIM // 2, 1) * sb)
        return jnp.concatenate(parts, axis=1)

    outs = (q_ref, k_ref, v_ref, az_ref, mq_ref, mk_ref, mv_ref, mo_ref, mz_ref)
    for gi, o_ref in enumerate(outs):
        acc = _dot(hb, w_ref[:, gi * ATT_WIDTH:(gi + 1) * ATT_WIDTH])
        if gi == 0:
            acc = rope(acc) * (ATT_HEAD_DIM ** -0.5)
        elif gi == 1:
            acc = rope(acc)
        o_ref[...] = acc.astype(o_ref.dtype)
    gate_ref[...] = _dot(hb, wg_ref[...])


def _rope_tables(seq):
    half = ROT_DIM // 2
    inv_freq = jnp.power(ROPE_THETA, -jnp.arange(half, dtype=F32) * 2.0 / ROT_DIM)
    ang = jnp.arange(seq, dtype=jnp.int32).astype(F32)[:, None] * inv_freq[None, :]
    cos = jnp.cos(ang)
    sin = jnp.sin(ang)
    ones = jnp.ones((seq, ATT_HEAD_DIM - ROT_DIM), F32)
    zeros = jnp.zeros((seq, ATT_HEAD_DIM - ROT_DIM), F32)
    zh = jnp.zeros((seq, half), F32)
    cos_h = jnp.concatenate([cos, cos, ones], axis=1)
    sa_h = jnp.concatenate([-sin, zh, zeros], axis=1)
    sb_h = jnp.concatenate([zh, sin, zeros], axis=1)
    rep = LANES // ATT_HEAD_DIM
    return (jnp.tile(cos_h, (1, rep)), jnp.tile(sa_h, (1, rep)), jnp.tile(sb_h, (1, rep)))


def _input_projection(x2, pre_g, w_main, w_gate, tables, seq):
    n_tok, d_model = x2.shape
    tm = PROJ_ROWS
    seq_tiles = seq // tm
    row = lambda i: (i, 0)
    fixed = lambda i: (0, 0)
    tab = lambda i: (i % seq_tiles, 0)
    wide = jax.ShapeDtypeStruct((n_tok, ATT_WIDTH), BF16)
    return pl.pallas_call(
        _proj_kernel,
        out_shape=(wide,) * 9 + (jax.ShapeDtypeStruct((n_tok, LANES), F32),),
        grid=(n_tok // tm,),
        in_specs=[
            pl.BlockSpec((tm, d_model), row),
            pl.BlockSpec((1, d_model), fixed),
            pl.BlockSpec(w_main.shape, fixed),
            pl.BlockSpec(w_gate.shape, fixed),
            pl.BlockSpec((tm, LANES), tab),
            pl.BlockSpec((tm, LANES), tab),
            pl.BlockSpec((tm, LANES), tab),
        ],
        out_specs=tuple(pl.BlockSpec((tm, ATT_WIDTH), row) for _ in range(9))
        + (pl.BlockSpec((tm, LANES), row),),
        compiler_params=pltpu.CompilerParams(
            dimension_semantics=("arbitrary",), vmem_limit_bytes=VMEM_LIMIT),
        name="input_projection",
    )(x2, pre_g, w_main, w_gate, *tables)


def _moba_kernel(qt_ref, k_ref, vt_ref, o_ref,
                 kst_hi, kst_mid, kst_lo, qta_ref, bias_ref, m_ref, l_ref, acc_ref, *, nb):
    i = pl.program_id(1)
    blk = MOBA_BLOCK
    hd = ATT_HEAD_DIM

    @pl.when(i == 0)
    def _():
        lane_head = lax.broadcasted_iota(jnp.int32, (nb, ATT_WIDTH), 1) // hd
        means = []
        for n in range(nb):
            means.append(jnp.mean(k_ref[0, n].astype(F32), axis=0, keepdims=True))
        km = jnp.concatenate(means, axis=0)
        for h in range(ATT_HEADS):
            hi, mid, lo = _split3(jnp.where(lane_head == h, km, 0.0))
            kst_hi[h * nb:(h + 1) * nb, :] = hi
            kst_mid[h * nb:(h + 1) * nb, :] = mid
            kst_lo[h * nb:(h + 1) * nb, :] = lo

    qt = qt_ref[0, 0]
    gates = _dot(kst_hi[...], qt) + _dot(kst_mid[...], qt) + _dot(kst_lo[...], qt)

    n_iota = lax.broadcasted_iota(jnp.int32, (nb, blk), 0)
    n_iota_f = n_iota.astype(F32)
    past = n_iota < i
    row_head = lax.broadcasted_iota(jnp.int32, (LANES, blk), 0) // hd
    for h in range(ATT_HEADS):
        g = jnp.where(past, gates[h * nb:(h + 1) * nb, :], -jnp.inf)
        sel = jnp.zeros((nb, blk), jnp.bool_)
        for _ in range(min(MOBA_TOPK, nb)):
            gmax = jnp.max(g, axis=0, keepdims=True)
            first = jnp.min(jnp.where(g == gmax, n_iota_f, float(nb)), axis=0, keepdims=True)
            hit = n_iota_f == first
            sel = sel | hit
            g = jnp.where(hit, -jnp.inf, g)
        bias_ref[h] = jnp.where(sel & past, 0.0, NEG_BIG)
        p = h // 2
        qpair = qt[p * LANES:(p + 1) * LANES, :]
        qta_ref[h] = jnp.where(row_head == (h % 2), qpair, jnp.zeros_like(qpair))

    kb = k_ref[0, i]
    vtb = vt_ref[0, i]
    l_iota = lax.broadcasted_iota(jnp.int32, (blk, blk), 0)
    q_iota = lax.broadcasted_iota(jnp.int32, (blk, blk), 1)
    causal = l_iota <= q_iota
    sts = [_dot(kb[:, (h // 2) * LANES:(h // 2 + 1) * LANES], qta_ref[h]) for h in range(ATT_HEADS)]
    pts = []
    for h in range(ATT_HEADS):
        st = jnp.where(causal, sts[h], NEG_BIG)
        m0 = jnp.max(st, axis=0, keepdims=True)
        pt = jnp.exp(st - m0)
        m_ref[h] = m0
        l_ref[h] = jnp.sum(pt, axis=0, keepdims=True)
        pts.append(pt.astype(BF16))
    for h in range(ATT_HEADS):
        acc_ref[h] = _dot(vtb[h * hd:(h + 1) * hd, :], pts[h])

    def body(n, carry):
        kb = k_ref[0, n]
        vtb = vt_ref[0, n]
        sts = [_dot(kb[:, (h // 2) * LANES:(h // 2 + 1) * LANES], qta_ref[h]) for h in range(ATT_HEADS)]
        pts = []
        alphas = []
        for h in range(ATT_HEADS):
            st = sts[h]
            b = bias_ref[h, pl.ds(n, 1), :]
            m_old = m_ref[h]
            m_new = jnp.maximum(m_old, jnp.max(st, axis=0, keepdims=True) + b)
            pt = jnp.exp(st - (m_new - b))
            alpha = jnp.exp(m_old - m_new)
            l_ref[h] = alpha * l_ref[h] + jnp.sum(pt, axis=0, keepdims=True)
            m_ref[h] = m_new
            pts.append(pt.astype(BF16))
            alphas.append(alpha)
        for h in range(ATT_HEADS):
            acc_ref[h] = alphas[h] * acc_ref[h] + _dot(vtb[h * hd:(h + 1) * hd, :], pts[h])
        return carry

    lax.fori_loop(0, i, body, 0)

    for h in range(ATT_HEADS):
        o_ref[0, 0, h * hd:(h + 1) * hd, :] = (acc_ref[h] / l_ref[h]).astype(o_ref.dtype)


def _moba_attention(qt4, k4, vt4):
    bsz, nb, width, blk = qt4.shape
    hn = ATT_HEADS * nb
    return pl.pallas_call(
        functools.partial(_moba_kernel, nb=nb),
        out_shape=jax.ShapeDtypeStruct((bsz, nb, width, blk), BF16),
        grid=(bsz, nb),
        in_specs=[
            pl.BlockSpec((1, 1, width, blk), lambda b, i: (b, i, 0, 0)),
            pl.BlockSpec((1, nb, blk, width), lambda b, i: (b, 0, 0, 0)),
            pl.BlockSpec((1, nb, width, blk), lambda b, i: (b, 0, 0, 0)),
        ],
        out_specs=pl.BlockSpec((1, 1, width, blk), lambda b, i: (b, i, 0, 0)),
        scratch_shapes=[
            pltpu.VMEM((hn, width), BF16),
            pltpu.VMEM((hn, width), BF16),
            pltpu.VMEM((hn, width), BF16),
            pltpu.VMEM((ATT_HEADS, LANES, blk), BF16),
            pltpu.VMEM((ATT_HEADS, nb, blk), F32),
            pltpu.VMEM((ATT_HEADS, 1, blk), F32),
            pltpu.VMEM((ATT_HEADS, 1, blk), F32),
            pltpu.VMEM((ATT_HEADS, ATT_HEAD_DIM, blk), F32),
        ],
        compiler_params=pltpu.CompilerParams(
            dimension_semantics=("arbitrary", "arbitrary"), vmem_limit_bytes=VMEM_LIMIT),
        name="moba_attention",
    )(qt4, k4, vt4)


def _lane_col(x, c):
    lane = lax.broadcasted_iota(jnp.int32, x.shape, 1)
    return jnp.sum(jnp.where(lane == c, x, 0.0), axis=-1, keepdims=True)


def _mlstm_kernel(mq_ref, mk_ref, mv_ref, mo_ref, mz_ref, gcol_ref, grow_ref,
                  bcol_ref, brow_ref, cw_ref, cb_ref, og_ref, tril_ref, triu_ref,
                  o_ref, ct_ref, n_ref, m_ref, pq_ref, pk_ref):
    c = pl.program_id(1)
    L = MLSTM_CHUNK
    hd = MLSTM_HEAD_DIM
    halo = SUBLANES

    @pl.when(c == 0)
    def _():
        ct_ref[...] = jnp.zeros_like(ct_ref)
        n_ref[...] = jnp.zeros_like(n_ref)
        m_ref[...] = jnp.zeros_like(m_ref)
        pq_ref[...] = jnp.zeros_like(pq_ref)
        pk_ref[...] = jnp.zeros_like(pk_ref)

    def conv_silu(u_ref, prev_ref, w, b):
        u = u_ref[0].astype(F32)
        ext = jnp.concatenate([prev_ref[...], u], axis=0)
        prev_ref[...] = u[L - halo:, :]
        out = b
        for j in range(CONV_WIDTH):
            off = halo - (CONV_WIDTH - 1) + j
            out = out + w[j:j + 1, :] * ext[off:off + L, :]
        return _silu(out)

    cw = cw_ref[...]
    cb = cb_ref[...]
    qc = conv_silu(mq_ref, pq_ref, cw[:, :MLSTM_WIDTH], cb[:, :MLSTM_WIDTH])
    kc = conv_silu(mk_ref, pk_ref, cw[:, MLSTM_WIDTH:], cb[:, MLSTM_WIDTH:]) * (hd ** -0.5)

    gc = gcol_ref[0] + bcol_ref[...]
    gr = grow_ref[0] + brow_ref[...]
    lf_c = _log_sigmoid(gc)
    lf_r = _log_sigmoid(gr)
    tril = tril_ref[...]
    triu = triu_ref[...]
    c_hi, c_mid, c_lo = _split3(lf_c)
    bcum_c = _dot(tril, c_hi) + _dot(tril, c_mid) + _dot(tril, c_lo)
    r_hi, r_mid, r_lo = _split3(lf_r)
    bcum_r = _dot(r_hi, triu) + _dot(r_mid, triu) + _dot(r_lo, triu)

    t_iota = lax.broadcasted_iota(jnp.int32, (L, L), 0)
    s_iota = lax.broadcasted_iota(jnp.int32, (L, L), 1)
    causal = s_iota <= t_iota

    og = og_ref[...]
    for h in range(MLSTM_HEADS):
        sl = slice(h * hd, (h + 1) * hd)
        q = qc[:, sl]
        k = kc[:, sl]
        qb = q.astype(BF16)
        kb = k.astype(BF16)
        v = mv_ref[0, :, sl]
        b_col = _lane_col(bcum_c, MLSTM_HEADS + h)
        i_col = _lane_col(gc, h)
        b_row = bcum_r[MLSTM_HEADS + h:MLSTM_HEADS + h + 1, :]
        i_row = gr[h:h + 1, :]
        m_prev = m_ref[h][:, 0:1]
        n_prev = n_ref[h]
        ct = ct_ref[h]

        dmat = jnp.where(causal, b_col - b_row + i_row, -jnp.inf)
        inter = b_col + m_prev
        m_t = jnp.maximum(inter, jnp.max(dmat, axis=-1, keepdims=True))
        w_inter = jnp.exp(inter - m_t)
        qk = lax.dot_general(qb, kb, (((1,), (1,)), ((), ())), preferred_element_type=F32)
        s = qk * jnp.exp(dmat - m_t)
        num = w_inter * _dot(qb, ct.astype(BF16)) + _dot(s.astype(BF16), v)
        nq = (w_inter * jnp.sum(q * n_prev, axis=-1, keepdims=True)
              + jnp.sum(s, axis=-1, keepdims=True))
        hh = num / jnp.maximum(jnp.abs(nq), jnp.exp(-m_t))

        b_last = b_col[L - 1:L, :]
        g = b_last - b_col + i_col
        m_new = jnp.maximum(b_last + m_prev, jnp.max(g, axis=0, keepdims=True))
        decay = jnp.exp(b_last + m_prev - m_new)
        w = jnp.exp(g - m_new)
        wv = (w * v.astype(F32)).astype(BF16)
        ct_ref[h] = decay * ct + lax.dot_general(
            kb, wv, (((0,), (0,)), ((), ())), preferred_element_type=F32)
        n_ref[h] = decay * n_prev + jnp.sum(w * k, axis=0, keepdims=True)
        m_ref[h] = jnp.broadcast_to(m_new, (1, LANES))

        mu = jnp.mean(hh, axis=-1, keepdims=True)
        var = jnp.mean(jnp.square(hh - mu), axis=-1, keepdims=True)
        hn = (hh - mu) * lax.rsqrt(var + NORM_EPS) * og[:, sl]
        ym = _sigmoid(mo_ref[0, :, sl].astype(F32)) * hn * _silu(mz_ref[0, :, sl].astype(F32))
        o_ref[0, :, sl] = ym.astype(o_ref.dtype)


def _mlstm(mq, mk, mv, mo, mz, gcol, grow, bias_col, bias_row, conv_w, conv_b, out_g):
    bsz, seq, width = mq.shape
    L = MLSTM_CHUNK
    tril = jnp.tril(jnp.ones((L, L), F32)).astype(BF16)
    triu = jnp.triu(jnp.ones((L, L), F32)).astype(BF16)
    tok = lambda b, c: (b, c, 0)
    fixed = lambda b, c: (0, 0)
    wide = pl.BlockSpec((1, L, width), tok)
    return pl.pallas_call(
        _mlstm_kernel,
        out_shape=jax.ShapeDtypeStruct((bsz, seq, width), BF16),
        grid=(bsz, seq // L),
        in_specs=[
            wide, wide, wide, wide, wide,
            pl.BlockSpec((1, L, LANES), tok),
            pl.BlockSpec((1, SUBLANES, L), lambda b, c: (b, 0, c)),
            pl.BlockSpec((1, LANES), fixed),
            pl.BlockSpec((SUBLANES, 1), fixed),
            pl.BlockSpec(conv_w.shape, fixed),
            pl.BlockSpec(conv_b.shape, fixed),
            pl.BlockSpec(out_g.shape, fixed),
            pl.BlockSpec((L, L), fixed),
            pl.BlockSpec((L, L), fixed),
        ],
        out_specs=wide,
        scratch_shapes=[
            pltpu.VMEM((MLSTM_HEADS, MLSTM_HEAD_DIM, MLSTM_HEAD_DIM), F32),
            pltpu.VMEM((MLSTM_HEADS, 1, MLSTM_HEAD_DIM), F32),
            pltpu.VMEM((MLSTM_HEADS, 1, LANES), F32),
            pltpu.VMEM((SUBLANES, width), F32),
            pltpu.VMEM((SUBLANES, width), F32),
        ],
        compiler_params=pltpu.CompilerParams(
            dimension_semantics=("arbitrary", "arbitrary"), vmem_limit_bytes=VMEM_LIMIT),
        name="mlstm",
    )(mq, mk, mv, mo, mz, gcol, grow, bias_col, bias_row, conv_w, conv_b, out_g, tril, triu)


def _out_kernel(x_ref, o_ref, az_ref, ym_ref, ag_ref, wa_ref, wm_ref, pg_ref, out_ref):
    o = o_ref[...].astype(F32)
    ms = jnp.mean(o * o, axis=-1, keepdims=True)
    ya = (o * lax.rsqrt(ms + NORM_EPS)) * ag_ref[...] * _silu(az_ref[...].astype(F32))
    y = _dot(ya.astype(BF16), wa_ref[...]) + _dot(ym_ref[...], wm_ref[...])
    ms2 = jnp.mean(y * y, axis=-1, keepdims=True)
    out_ref[...] = x_ref[...] + (y * lax.rsqrt(ms2 + NORM_EPS)) * pg_ref[...]


def _output_projection(x2, o2, az, ym, attn_g, w_a, w_m, post_g):
    n_tok, d_model = x2.shape
    tm = PROJ_ROWS
    row = lambda i: (i, 0)
    fixed = lambda i: (0, 0)
    return pl.pallas_call(
        _out_kernel,
        out_shape=jax.ShapeDtypeStruct((n_tok, d_model), F32),
        grid=(n_tok // tm,),
        in_specs=[
            pl.BlockSpec((tm, d_model), row),
            pl.BlockSpec((tm, ATT_WIDTH), row),
            pl.BlockSpec((tm, ATT_WIDTH), row),
            pl.BlockSpec((tm, MLSTM_WIDTH), row),
            pl.BlockSpec((1, ATT_WIDTH), fixed),
            pl.BlockSpec(w_a.shape, fixed),
            pl.BlockSpec(w_m.shape, fixed),
            pl.BlockSpec((1, d_model), fixed),
        ],
        out_specs=pl.BlockSpec((tm, d_model), row),
        compiler_params=pltpu.CompilerParams(
            dimension_semantics=("arbitrary",), vmem_limit_bytes=VMEM_LIMIT),
        name="output_projection",
    )(x2, o2, az, ym, attn_g, w_a, w_m, post_g)


def _layer(x, pre_g, w_in, i_bias, f_bias, conv_w, conv_b, attn_g, mlstm_g, w_out, post_g):
    bsz, seq, d_model = x.shape
    assert seq % MOBA_BLOCK == 0 and seq % PROJ_ROWS == 0 and seq % MLSTM_CHUNK == 0
    nb = seq // MOBA_BLOCK
    n_tok = bsz * seq
    n_main = 4 * ATT_WIDTH + 5 * MLSTM_WIDTH
    n_gate = 2 * MLSTM_HEADS
    assert w_in.shape == (d_model, n_main + n_gate)

    w_main = w_in[:, :n_main].astype(BF16)
    w_gate = jnp.pad(w_in[:, n_main:], ((0, 0), (0, LANES - n_gate))).astype(BF16)
    x2 = x.reshape(n_tok, d_model)
    (q, k, v, az, mq, mk, mv, mo, mz, gates) = _input_projection(
        x2, pre_g[None, :], w_main, w_gate, _rope_tables(seq), seq)

    to_blocks_t = lambda t: t.reshape(bsz, nb, MOBA_BLOCK, ATT_WIDTH).transpose(0, 1, 3, 2)
    ot4 = _moba_attention(to_blocks_t(q), k.reshape(bsz, nb, MOBA_BLOCK, ATT_WIDTH), to_blocks_t(v))
    o2 = ot4.transpose(0, 1, 3, 2).reshape(n_tok, ATT_WIDTH)

    gcol = gates.reshape(bsz, seq, LANES)
    grow = gcol[:, :, :SUBLANES].transpose(0, 2, 1)
    gate_bias = jnp.concatenate([i_bias, f_bias])
    bias_col = jnp.pad(gate_bias, (0, LANES - n_gate))[None, :]
    bias_row = gate_bias[:, None]
    shp = (bsz, seq, MLSTM_WIDTH)
    ym = _mlstm(mq.reshape(shp), mk.reshape(shp), mv.reshape(shp), mo.reshape(shp), mz.reshape(shp),
                gcol, grow, bias_col, bias_row, conv_w, conv_b[None, :], mlstm_g[None, :])

    w_out_b = w_out.astype(BF16)
    out = _output_projection(x2, o2, az, ym.reshape(n_tok, MLSTM_WIDTH), attn_g[None, :],
                             w_out_b[:ATT_WIDTH], w_out_b[ATT_WIDTH:], post_g[None, :])
    return out.reshape(bsz, seq, d_model)


def kernel(x, pre_norm_g, w_in, mlstm_i_bias, mlstm_f_bias, conv_w, conv_b, attn_out_g, mlstm_out_g,
           w_out, post_norm_g):
    for l in range(pre_norm_g.shape[0]):
        x = _layer(x, pre_norm_g[l], w_in[l], mlstm_i_bias[l], mlstm_f_bias[l], conv_w[l], conv_b[l],
                   attn_out_g[l], mlstm_out_g[l], w_out[l], post_norm_g[l])
    return x
```

```python
import functools

import jax
import jax.numpy as jnp
from jax import lax
from jax.experimental import pallas as pl
from jax.experimental.pallas import tpu as pltpu

F32 = jnp.float32
BF16 = jnp.bfloat16

ATT_HEADS = 8
ATT_HEAD_DIM = 64
ATT_WIDTH = ATT_HEADS * ATT_HEAD_DIM
ROT_DIM = ATT_HEAD_DIM // 4
ROPE_THETA = 500000.0
MOBA_BLOCK = 256
MOBA_TOPK = 3
MLSTM_HEADS = 4
MLSTM_HEAD_DIM = 128
MLSTM_WIDTH = MLSTM_HEADS * MLSTM_HEAD_DIM
CONV_WIDTH = 4
NORM_EPS = 1e-6

LANES = 128
SUBLANES = 8
PROJ_ROWS = 512
MLSTM_CHUNK = 256
NEG_BIG = -1e30
LOG2_E = 1.4426950408889634
VMEM_LIMIT = 56 * 1024 * 1024


def _sigmoid(x):
    return 1.0 / (1.0 + jnp.exp(-x))


def _silu(x):
    return x * _sigmoid(x)


def _log_sigmoid(x):
    return jnp.minimum(x, 0.0) - jnp.log(1.0 + jnp.exp(-jnp.abs(x)))


def _split3(x):
    hi = x.astype(BF16)
    r1 = x - hi.astype(F32)
    mid = r1.astype(BF16)
    lo = (r1 - mid.astype(F32)).astype(BF16)
    return hi, mid, lo


def _dot(a, b):
    return jnp.dot(a, b, preferred_element_type=F32)


def _proj_kernel(x_ref, g_ref, w_ref, wg_ref, cos_ref, sa_ref, sb_ref,
                 q_ref, k_ref, v_ref, az_ref, mq_ref, mk_ref, mv_ref, mo_ref, mz_ref, gate_ref):
    x = x_ref[...]
    ms = jnp.mean(x * x, axis=-1, keepdims=True)
    hb = ((x * lax.rsqrt(ms + NORM_EPS)) * g_ref[...]).astype(BF16)

    cos = cos_ref[...]
    sa = sa_ref[...]
    sb = sb_ref[...]

    def rope(acc):
        parts = []
        for c in range(ATT_WIDTH // LANES):
            xs = acc[:, c * LANES:(c + 1) * LANES]
            parts.append(xs * cos
                         + pltpu.roll(xs, LANES - ROT_DIM // 2, 1) * sa
                         + pltpu.roll(xs, ROT_DIM // 2, 1) * sb)
        return jnp.concatenate(parts, axis=1)

    outs = (q_ref, k_ref, v_ref, az_ref, mq_ref, mk_ref, mv_ref, mo_ref, mz_ref)
    for gi, o_ref in enumerate(outs):
        acc = _dot(hb, w_ref[:, gi * ATT_WIDTH:(gi + 1) * ATT_WIDTH])
        if gi == 0:
            acc = rope(acc) * (LOG2_E * ATT_HEAD_DIM ** -0.5)
        elif gi == 1:
            acc = rope(acc)
        o_ref[...] = acc.astype(o_ref.dtype)
    gate_ref[...] = _dot(hb, wg_ref[...])


def _rope_tables(seq):
    half = ROT_DIM // 2
    inv_freq = jnp.power(ROPE_THETA, -jnp.arange(half, dtype=F32) * 2.0 / ROT_DIM)
    ang = jnp.arange(seq, dtype=jnp.int32).astype(F32)[:, None] * inv_freq[None, :]
    cos = jnp.cos(ang)
    sin = jnp.sin(ang)
    ones = jnp.ones((seq, ATT_HEAD_DIM - ROT_DIM), F32)
    zeros = jnp.zeros((seq, ATT_HEAD_DIM - ROT_DIM), F32)
    zh = jnp.zeros((seq, half), F32)
    cos_h = jnp.concatenate([cos, cos, ones], axis=1)
    sa_h = jnp.concatenate([-sin, zh, zeros], axis=1)
    sb_h = jnp.concatenate([zh, sin, zeros], axis=1)
    rep = LANES // ATT_HEAD_DIM
    return (jnp.tile(cos_h, (1, rep)), jnp.tile(sa_h, (1, rep)), jnp.tile(sb_h, (1, rep)))


def _input_projection(x2, pre_g, w_main, w_gate, tables, seq):
    n_tok, d_model = x2.shape
    tm = PROJ_ROWS
    seq_tiles = seq // tm
    row = lambda i: (i, 0)
    fixed = lambda i: (0, 0)
    tab = lambda i: (i % seq_tiles, 0)
    wide = jax.ShapeDtypeStruct((n_tok, ATT_WIDTH), BF16)
    return pl.pallas_call(
        _proj_kernel,
        out_shape=(wide,) * 9 + (jax.ShapeDtypeStruct((n_tok, LANES), F32),),
        grid=(n_tok // tm,),
        in_specs=[
            pl.BlockSpec((tm, d_model), row),
            pl.BlockSpec((1, d_model), fixed),
            pl.BlockSpec(w_main.shape, fixed),
            pl.BlockSpec(w_gate.shape, fixed),
            pl.BlockSpec((tm, LANES), tab),
            pl.BlockSpec((tm, LANES), tab),
            pl.BlockSpec((tm, LANES), tab),
        ],
        out_specs=tuple(pl.BlockSpec((tm, ATT_WIDTH), row) for _ in range(9))
        + (pl.BlockSpec((tm, LANES), row),),
        compiler_params=pltpu.CompilerParams(
            dimension_semantics=("arbitrary",), vmem_limit_bytes=VMEM_LIMIT),
        name="input_projection",
    )(x2, pre_g, w_main, w_gate, *tables)


def _moba_kernel(qt_ref, k_ref, vt_ref, o_ref,
                 kst_hi, kst_mid, kst_lo, qta_ref, bias_ref, m_ref, acc_ref,
                 s0_ref, s1_ref, p0_ref, p1_ref, a0_ref, a1_ref, *, nb):
    i = pl.program_id(1)
    blk = MOBA_BLOCK
    hd = ATT_HEAD_DIM

    @pl.when(i == 0)
    def _():
        lane_head = lax.broadcasted_iota(jnp.int32, (nb, ATT_WIDTH), 1) // hd
        means = []
        for n in range(nb):
            means.append(jnp.mean(k_ref[0, n].astype(F32), axis=0, keepdims=True))
        km = jnp.concatenate(means, axis=0)
        for h in range(ATT_HEADS):
            hi, mid, lo = _split3(jnp.where(lane_head == h, km, 0.0))
            kst_hi[h * nb:(h + 1) * nb, :] = hi
            kst_mid[h * nb:(h + 1) * nb, :] = mid
            kst_lo[h * nb:(h + 1) * nb, :] = lo

    qt = qt_ref[0, 0]
    gates = _dot(kst_hi[...], qt) + _dot(kst_mid[...], qt) + _dot(kst_lo[...], qt)

    n_iota = lax.broadcasted_iota(jnp.int32, (nb, blk), 0)
    n_iota_f = n_iota.astype(F32)
    past = n_iota < i
    row_head = lax.broadcasted_iota(jnp.int32, (LANES, blk), 0) // hd
    for h in range(ATT_HEADS):
        g = jnp.where(past, gates[h * nb:(h + 1) * nb, :], -jnp.inf)
        sel = jnp.zeros((nb, blk), jnp.bool_)
        for _ in range(min(MOBA_TOPK, nb)):
            gmax = jnp.max(g, axis=0, keepdims=True)
            first = jnp.min(jnp.where(g == gmax, n_iota_f, float(nb)), axis=0, keepdims=True)
            hit = n_iota_f == first
            sel = sel | hit
            g = jnp.where(hit, -jnp.inf, g)
        bias_ref[h, 0:nb, :] = jnp.where(sel & past, 0.0, NEG_BIG)
        bias_ref[h, nb:nb + SUBLANES, :] = jnp.full((SUBLANES, blk), NEG_BIG, F32)
        p = h // 2
        qpair = qt[p * LANES:(p + 1) * LANES, :]
        qta_ref[h] = jnp.where(row_head == (h % 2), qpair, jnp.zeros_like(qpair))
        m_ref[h] = jnp.full((1, blk), NEG_BIG, F32)
        acc_ref[h] = jnp.zeros(acc_ref.shape[1:], F32)

    l_iota = lax.broadcasted_iota(jnp.int32, (blk, blk), 0)
    q_iota = lax.broadcasted_iota(jnp.int32, (blk, blk), 1)
    causal = l_iota <= q_iota
    ones_rows = jnp.ones((2 * SUBLANES, blk), BF16)

    def scores(block, s_ref, own=False):
        kb = k_ref[0, block]
        for h in range(ATT_HEADS):
            st = _dot(kb[:, (h // 2) * LANES:(h // 2 + 1) * LANES], qta_ref[h])
            s_ref[h] = jnp.where(causal, st, NEG_BIG) if own else st

    def softmax_update(bias_row, s_ref, p_ref, a_ref):
        for h in range(ATT_HEADS):
            st = s_ref[h]
            m_old = m_ref[h]
            if bias_row is None:
                m_new = jnp.maximum(m_old, jnp.max(st, axis=0, keepdims=True))
                shift = m_new
            else:
                b = bias_ref[h, pl.ds(bias_row, 1), :]
                m_new = jnp.maximum(m_old, jnp.max(st, axis=0, keepdims=True) + b)
                shift = m_new - b
            p_ref[h] = jnp.exp2(st - shift).astype(BF16)
            a_ref[h] = jnp.exp2(m_old - m_new)
            m_ref[h] = m_new

    def values(block, p_ref, a_ref):
        vtb = vt_ref[0, block]
        for h in range(ATT_HEADS):
            lhs = jnp.concatenate([vtb[h * hd:(h + 1) * hd, :], ones_rows], axis=0)
            acc_ref[h] = a_ref[h] * acc_ref[h] + _dot(lhs, p_ref[h])

    scores(i, s0_ref, own=True)
    scores(0, s1_ref)
    softmax_update(None, s0_ref, p0_ref, a0_ref)

    def body(u, carry):
        t = 2 * u + 2
        scores(jnp.minimum(t - 1, nb - 1), s0_ref)
        values(jnp.where(u == 0, i, t - 3), p0_ref, a0_ref)
        softmax_update(t - 2, s1_ref, p1_ref, a1_ref)
        scores(jnp.minimum(t, nb - 1), s1_ref)
        values(t - 2, p1_ref, a1_ref)
        softmax_update(t - 1, s0_ref, p0_ref, a0_ref)
        return carry

    lax.fori_loop(0, (i + 2) // 2, body, 0)

    for h in range(ATT_HEADS):
        acc = acc_ref[h]
        o_ref[0, 0, h * hd:(h + 1) * hd, :] = (acc[:hd, :] / acc[hd:hd + 1, :]).astype(o_ref.dtype)


def _moba_attention(qt4, k4, vt4):
    bsz, nb, width, blk = qt4.shape
    hn = ATT_HEADS * nb
    tile_f32 = pltpu.VMEM((ATT_HEADS, blk, blk), F32)
    tile_bf16 = pltpu.VMEM((ATT_HEADS, blk, blk), BF16)
    row_f32 = pltpu.VMEM((ATT_HEADS, 1, blk), F32)
    return pl.pallas_call(
        functools.partial(_moba_kernel, nb=nb),
        out_shape=jax.ShapeDtypeStruct((bsz, nb, width, blk), BF16),
        grid=(bsz, nb),
        in_specs=[
            pl.BlockSpec((1, 1, width, blk), lambda b, i: (b, i, 0, 0)),
            pl.BlockSpec((1, nb, blk, width), lambda b, i: (b, 0, 0, 0)),
            pl.BlockSpec((1, nb, width, blk), lambda b, i: (b, 0, 0, 0)),
        ],
        out_specs=pl.BlockSpec((1, 1, width, blk), lambda b, i: (b, i, 0, 0)),
        scratch_shapes=[
            pltpu.VMEM((hn, width), BF16),
            pltpu.VMEM((hn, width), BF16),
            pltpu.VMEM((hn, width), BF16),
            pltpu.VMEM((ATT_HEADS, LANES, blk), BF16),
            pltpu.VMEM((ATT_HEADS, nb + SUBLANES, blk), F32),
            row_f32,
            pltpu.VMEM((ATT_HEADS, ATT_HEAD_DIM + 2 * SUBLANES, blk), F32),
            tile_f32, tile_f32, tile_bf16, tile_bf16, row_f32, row_f32,
        ],
        compiler_params=pltpu.CompilerParams(
            dimension_semantics=("arbitrary", "arbitrary"), vmem_limit_bytes=VMEM_LIMIT),
        name="moba_attention",
    )(qt4, k4, vt4)


def _lane_col(x, c):
    lane = lax.broadcasted_iota(jnp.int32, x.shape, 1)
    return jnp.sum(jnp.where(lane == c, x, 0.0), axis=-1, keepdims=True)


def _mlstm_kernel(mq_ref, mk_ref, mv_ref, mo_ref, mz_ref, gcol_ref, grow_ref,
                  bcol_ref, brow_ref, cw_ref, cb_ref, og_ref, tril_ref, triu_ref,
                  o_ref, ct_ref, n_ref, m_ref, pq_ref, pk_ref):
    c = pl.program_id(1)
    L = MLSTM_CHUNK
    hd = MLSTM_HEAD_DIM
    halo = SUBLANES

    @pl.when(c == 0)
    def _():
        ct_ref[...] = jnp.zeros_like(ct_ref)
        n_ref[...] = jnp.zeros_like(n_ref)
        m_ref[...] = jnp.zeros_like(m_ref)
        pq_ref[...] = jnp.zeros_like(pq_ref)
        pk_ref[...] = jnp.zeros_like(pk_ref)

    def conv_silu(u_ref, prev_ref, w, b):
        u = u_ref[0].astype(F32)
        ext = jnp.concatenate([prev_ref[...], u], axis=0)
        prev_ref[...] = u[L - halo:, :]
        out = b
        for j in range(CONV_WIDTH):
            off = halo - (CONV_WIDTH - 1) + j
            out = out + w[j:j + 1, :] * ext[off:off + L, :]
        return _silu(out)

    cw = cw_ref[...]
    cb = cb_ref[...]
    qc = conv_silu(mq_ref, pq_ref, cw[:, :MLSTM_WIDTH], cb[:, :MLSTM_WIDTH])
    kc = conv_silu(mk_ref, pk_ref, cw[:, MLSTM_WIDTH:], cb[:, MLSTM_WIDTH:]) * (hd ** -0.5)

    gc = gcol_ref[0] + bcol_ref[...]
    gr = grow_ref[0] + brow_ref[...]
    lf_c = _log_sigmoid(gc)
    lf_r = _log_sigmoid(gr)
    tril = tril_ref[...]
    triu = triu_ref[...]
    c_hi, c_mid, c_lo = _split3(lf_c)
    bcum_c = _dot(tril, c_hi) + _dot(tril, c_mid) + _dot(tril, c_lo)
    r_hi, r_mid, r_lo = _split3(lf_r)
    bcum_r = _dot(r_hi, triu) + _dot(r_mid, triu) + _dot(r_lo, triu)

    t_iota = lax.broadcasted_iota(jnp.int32, (L, L), 0)
    s_iota = lax.broadcasted_iota(jnp.int32, (L, L), 1)
    causal = s_iota <= t_iota

    og = og_ref[...]
    for h in range(MLSTM_HEADS):
        sl = slice(h * hd, (h + 1) * hd)
        q = qc[:, sl]
        k = kc[:, sl]
        qb = q.astype(BF16)
        kb = k.astype(BF16)
        v = mv_ref[0, :, sl]
        b_col = _lane_col(bcum_c, MLSTM_HEADS + h)
        i_col = _lane_col(gc, h)
        b_row = bcum_r[MLSTM_HEADS + h:MLSTM_HEADS + h + 1, :]
        i_row = gr[h:h + 1, :]
        m_prev = m_ref[h][:, 0:1]
        n_prev = n_ref[h]
        ct = ct_ref[h]

        dmat = jnp.where(causal, b_col - b_row + i_row, -jnp.inf)
        inter = b_col + m_prev
        m_t = jnp.maximum(inter, jnp.max(dmat, axis=-1, keepdims=True))
        w_inter = jnp.exp(inter - m_t)
        qk = lax.dot_general(qb, kb, (((1,), (1,)), ((), ())), preferred_element_type=F32)
        s = qk * jnp.exp(dmat - m_t)
        num = w_inter * _dot(qb, ct.astype(BF16)) + _dot(s.astype(BF16), v)
        nq = (w_inter * jnp.sum(q * n_prev, axis=-1, keepdims=True)
              + jnp.sum(s, axis=-1, keepdims=True))
        hh = num / jnp.maximum(jnp.abs(nq), jnp.exp(-m_t))

        b_last = b_col[L - 1:L, :]
        g = b_last - b_col + i_col
        m_new = jnp.maximum(b_last + m_prev, jnp.max(g, axis=0, keepdims=True))
        decay = jnp.exp(b_last + m_prev - m_new)
        w = jnp.exp(g - m_new)
        wv = (w * v.astype(F32)).astype(BF16)
        ct_ref[h] = decay * ct + lax.dot_general(
            kb, wv, (((0,), (0,)), ((), ())), preferred_element_type=F32)
        n_ref[h] = decay * n_prev + jnp.sum(w * k, axis=0, keepdims=True)
        m_ref[h] = jnp.broadcast_to(m_new, (1, LANES))

        mu = jnp.mean(hh, axis=-1, keepdims=True)
        var = jnp.mean(jnp.square(hh - mu), axis=-1, keepdims=True)
        hn = (hh - mu) * lax.rsqrt(var + NORM_EPS) * og[:, sl]
        ym = _sigmoid(mo_ref[0, :, sl].astype(F32)) * hn * _silu(mz_ref[0, :, sl].astype(F32))
        o_ref[0, :, sl] = ym.astype(o_ref.dtype)


def _mlstm(mq, mk, mv, mo, mz, gcol, grow, bias_col, bias_row, conv_w, conv_b, out_g):
    bsz, seq, width = mq.shape
    L = MLSTM_CHUNK
    tril = jnp.tril(jnp.ones((L, L), F32)).astype(BF16)
    triu = jnp.triu(jnp.ones((L, L), F32)).astype(BF16)
    tok = lambda b, c: (b, c, 0)
    fixed = lambda b, c: (0, 0)
    wide = pl.BlockSpec((1, L, width), tok)
    return pl.pallas_call(
        _mlstm_kernel,
        out_shape=jax.ShapeDtypeStruct((bsz, seq, width), BF16),
        grid=(bsz, seq // L),
        in_specs=[
            wide, wide, wide, wide, wide,
            pl.BlockSpec((1, L, LANES), tok),
            pl.BlockSpec((1, SUBLANES, L), lambda b, c: (b, 0, c)),
            pl.BlockSpec((1, LANES), fixed),
            pl.BlockSpec((SUBLANES, 1), fixed),
            pl.BlockSpec(conv_w.shape, fixed),
            pl.BlockSpec(conv_b.shape, fixed),
            pl.BlockSpec(out_g.shape, fixed),
            pl.BlockSpec((L, L), fixed),
            pl.BlockSpec((L, L), fixed),
        ],
        out_specs=wide,
        scratch_shapes=[
            pltpu.VMEM((MLSTM_HEADS, MLSTM_HEAD_DIM, MLSTM_HEAD_DIM), F32),
            pltpu.VMEM((MLSTM_HEADS, 1, MLSTM_HEAD_DIM), F32),
            pltpu.VMEM((MLSTM_HEADS, 1, LANES), F32),
            pltpu.VMEM((SUBLANES, width), F32),
            pltpu.VMEM((SUBLANES, width), F32),
        ],
        compiler_params=pltpu.CompilerParams(
            dimension_semantics=("arbitrary", "arbitrary"), vmem_limit_bytes=VMEM_LIMIT),
        name="mlstm",
    )(mq, mk, mv, mo, mz, gcol, grow, bias_col, bias_row, conv_w, conv_b, out_g, tril, triu)


def _out_kernel(x_ref, o_ref, az_ref, ym_ref, ag_ref, wa_ref, wm_ref, pg_ref, out_ref):
    o = o_ref[...].astype(F32)
    ms = jnp.mean(o * o, axis=-1, keepdims=True)
    ya = (o * lax.rsqrt(ms + NORM_EPS)) * ag_ref[...] * _silu(az_ref[...].astype(F32))
    y = _dot(ya.astype(BF16), wa_ref[...]) + _dot(ym_ref[...], wm_ref[...])
    ms2 = jnp.mean(y * y, axis=-1, keepdims=True)
    out_ref[...] = x_ref[...] + (y * lax.rsqrt(ms2 + NORM_EPS)) * pg_ref[...]


def _output_projection(x2, o2, az, ym, attn_g, w_a, w_m, post_g):
    n_tok, d_model = x2.shape
    tm = PROJ_ROWS
    row = lambda i: (i, 0)
    fixed = lambda i: (0, 0)
    return pl.pallas_call(
        _out_kernel,
        out_shape=jax.ShapeDtypeStruct((n_tok, d_model), F32),
        grid=(n_tok // tm,),
        in_specs=[
            pl.BlockSpec((tm, d_model), row),
            pl.BlockSpec((tm, ATT_WIDTH), row),
            pl.BlockSpec((tm, ATT_WIDTH), row),
            pl.BlockSpec((tm, MLSTM_WIDTH), row),
            pl.BlockSpec((1, ATT_WIDTH), fixed),
            pl.BlockSpec(w_a.shape, fixed),
            pl.BlockSpec(w_m.shape, fixed),
            pl.BlockSpec((1, d_model), fixed),
        ],
        out_specs=pl.BlockSpec((tm, d_model), row),
        compiler_params=pltpu.CompilerParams(
            dimension_semantics=("arbitrary",), vmem_limit_bytes=VMEM_LIMIT),
        name="output_projection",
    )(x2, o2, az, ym, attn_g, w_a, w_m, post_g)


def _layer(x, pre_g, w_in, i_bias, f_bias, conv_w, conv_b, attn_g, mlstm_g, w_out, post_g):
    bsz, seq, d_model = x.shape
    assert seq % MOBA_BLOCK == 0 and seq % PROJ_ROWS == 0 and seq % MLSTM_CHUNK == 0
    nb = seq // MOBA_BLOCK
    n_tok = bsz * seq
    n_main = 4 * ATT_WIDTH + 5 * MLSTM_WIDTH
    n_gate = 2 * MLSTM_HEADS
    assert w_in.shape == (d_model, n_main + n_gate)

    w_main = w_in[:, :n_main].astype(BF16)
    w_gate = jnp.pad(w_in[:, n_main:], ((0, 0), (0, LANES - n_gate))).astype(BF16)
    x2 = x.reshape(n_tok, d_model)
    (q, k, v, az, mq, mk, mv, mo, mz, gates) = _input_projection(
        x2, pre_g[None, :], w_main, w_gate, _rope_tables(seq), seq)

    to_blocks_t = lambda t: t.reshape(bsz, nb, MOBA_BLOCK, ATT_WIDTH).transpose(0, 1, 3, 2)
    ot4 = _moba_attention(to_blocks_t(q), k.reshape(bsz, nb, MOBA_BLOCK, ATT_WIDTH), to_blocks_t(v))
    o2 = ot4.transpose(0, 1, 3, 2).reshape(n_tok, ATT_WIDTH)

    gcol = gates.reshape(bsz, seq, LANES)
    grow = gcol[:, :, :SUBLANES].transpose(0, 2, 1)
    gate_bias = jnp.concatenate([i_bias, f_bias])
    bias_col = jnp.pad(gate_bias, (0, LANES - n_gate))[None, :]
    bias_row = gate_bias[:, None]
    shp = (bsz, seq, MLSTM_WIDTH)
    ym = _mlstm(mq.reshape(shp), mk.reshape(shp), mv.reshape(shp), mo.reshape(shp), mz.reshape(shp),
                gcol, grow, bias_col, bias_row, conv_w, conv_b[None, :], mlstm_g[None, :])

    w_out_b = w_out.astype(BF16)
    out = _output_projection(x2, o2, az, ym.reshape(n_tok, MLSTM_WIDTH), attn_g[None, :],
                             w_out_b[:ATT_WIDTH], w_out_b[ATT_WIDTH:], post_g[None, :])
    return out.reshape(bsz, seq, d_model)


def kernel(x, pre_norm_g, w_in, mlstm_i_bias, mlstm_f_bias, conv_w, conv_b, attn_out_g, mlstm_out_g,
           w_out, post_norm_g):
    for l in range(pre_norm_g.shape[0]):
        x = _layer(x, pre_norm_g[l], w_in[l], mlstm_i_bias[l], mlstm_f_bias[l], conv_w[l], conv_b[l],
                   attn_out_g[l], mlstm_out_g[l], w_out[l], post_norm_g[l])
    return x
```

```python
import functools

import jax
import jax.numpy as jnp
from jax import lax
from jax.experimental import pallas as pl
from jax.experimental.pallas import tpu as pltpu

F32 = jnp.float32
BF16 = jnp.bfloat16

ATT_HEADS = 8
ATT_HEAD_DIM = 64
ATT_WIDTH = ATT_HEADS * ATT_HEAD_DIM
ROT_DIM = ATT_HEAD_DIM // 4
ROPE_THETA = 500000.0
MOBA_BLOCK = 256
MOBA_TOPK = 3
MLSTM_HEADS = 4
MLSTM_HEAD_DIM = 128
MLSTM_WIDTH = MLSTM_HEADS * MLSTM_HEAD_DIM
CONV_WIDTH = 4
NORM_EPS = 1e-6

LANES = 128
SUBLANES = 8
GATE_ROWS = 16
PROJ_ROWS = 512
MLSTM_CHUNK = 256
NEG_BIG = -1e30
LOG2_E = 1.4426950408889634
VMEM_LIMIT = 56 * 1024 * 1024


def _sigmoid(x):
    return 1.0 / (1.0 + jnp.exp(-x))


def _silu(x):
    return x * _sigmoid(x)


def _log_sigmoid(x):
    return jnp.minimum(x, 0.0) - jnp.log(1.0 + jnp.exp(-jnp.abs(x)))


def _split3(x):
    hi = x.astype(BF16)
    r1 = x - hi.astype(F32)
    mid = r1.astype(BF16)
    lo = (r1 - mid.astype(F32)).astype(BF16)
    return hi, mid, lo


def _dot(a, b):
    return jnp.dot(a, b, preferred_element_type=F32)


def _proj_kernel(x_ref, g_ref, w_ref, wgt_ref, cos_ref, sa_ref, sb_ref, cw_ref, cb_ref,
                 q_ref, k_ref, v_ref, sz_ref, qc_ref, kc_ref, mv_ref, og_ref, gate_ref,
                 pq_ref, pk_ref, *, seq_tiles):
    first_of_seq = (pl.program_id(0) % seq_tiles) == 0
    tm = x_ref.shape[0]
    halo = SUBLANES
    x = x_ref[...]
    ms = jnp.mean(x * x, axis=-1, keepdims=True)
    hb = ((x * lax.rsqrt(ms + NORM_EPS)) * g_ref[...]).astype(BF16)

    cos = cos_ref[...]
    sa = sa_ref[...]
    sb = sb_ref[...]

    def rope(acc):
        parts = []
        for c in range(ATT_WIDTH // LANES):
            xs = acc[:, c * LANES:(c + 1) * LANES]
            parts.append(xs * cos
                         + pltpu.roll(xs, LANES - ROT_DIM // 2, 1) * sa
                         + pltpu.roll(xs, ROT_DIM // 2, 1) * sb)
        return jnp.concatenate(parts, axis=1)

    def conv_silu(acc, prev_ref, w, b):
        prev = jnp.where(first_of_seq, 0.0, prev_ref[...])
        ext = jnp.concatenate([prev, acc], axis=0)
        prev_ref[...] = acc[tm - halo:, :]
        out = b
        for j in range(CONV_WIDTH):
            off = halo - (CONV_WIDTH - 1) + j
            out = out + w[j:j + 1, :] * ext[off:off + tm, :]
        return _silu(out)

    def proj(gi):
        return _dot(hb, w_ref[:, gi * ATT_WIDTH:(gi + 1) * ATT_WIDTH])

    cw = cw_ref[...]
    cb = cb_ref[...]
    q_ref[...] = (rope(proj(0)) * (LOG2_E * ATT_HEAD_DIM ** -0.5)).astype(q_ref.dtype)
    k_ref[...] = rope(proj(1)).astype(k_ref.dtype)
    v_ref[...] = proj(2).astype(v_ref.dtype)
    sz_ref[...] = _silu(proj(3)).astype(sz_ref.dtype)
    qc_ref[...] = conv_silu(proj(4), pq_ref, cw[:, :MLSTM_WIDTH], cb[:, :MLSTM_WIDTH]).astype(qc_ref.dtype)
    kc = conv_silu(proj(5), pk_ref, cw[:, MLSTM_WIDTH:], cb[:, MLSTM_WIDTH:])
    kc_ref[...] = (kc * (MLSTM_HEAD_DIM ** -0.5)).astype(kc_ref.dtype)
    mv_ref[...] = proj(6).astype(mv_ref.dtype)
    og_ref[...] = (_sigmoid(proj(7)) * _silu(proj(8))).astype(og_ref.dtype)
    gate_ref[...] = lax.dot_general(wgt_ref[...], hb, (((1,), (1,)), ((), ())),
                                    preferred_element_type=F32)


def _rope_tables(seq):
    half = ROT_DIM // 2
    inv_freq = jnp.power(ROPE_THETA, -jnp.arange(half, dtype=F32) * 2.0 / ROT_DIM)
    ang = jnp.arange(seq, dtype=jnp.int32).astype(F32)[:, None] * inv_freq[None, :]
    cos = jnp.cos(ang)
    sin = jnp.sin(ang)
    ones = jnp.ones((seq, ATT_HEAD_DIM - ROT_DIM), F32)
    zeros = jnp.zeros((seq, ATT_HEAD_DIM - ROT_DIM), F32)
    zh = jnp.zeros((seq, half), F32)
    cos_h = jnp.concatenate([cos, cos, ones], axis=1)
    sa_h = jnp.concatenate([-sin, zh, zeros], axis=1)
    sb_h = jnp.concatenate([zh, sin, zeros], axis=1)
    rep = LANES // ATT_HEAD_DIM
    return (jnp.tile(cos_h, (1, rep)), jnp.tile(sa_h, (1, rep)), jnp.tile(sb_h, (1, rep)))


def _input_projection(x2, pre_g, w_main, w_gate_t, tables, conv_w, conv_b, seq):
    n_tok, d_model = x2.shape
    tm = PROJ_ROWS
    seq_tiles = seq // tm
    row = lambda i: (i, 0)
    fixed = lambda i: (0, 0)
    tab = lambda i: (i % seq_tiles, 0)
    wide = jax.ShapeDtypeStruct((n_tok, ATT_WIDTH), BF16)
    n_wide = 8
    return pl.pallas_call(
        functools.partial(_proj_kernel, seq_tiles=seq_tiles),
        out_shape=(wide,) * n_wide + (jax.ShapeDtypeStruct((GATE_ROWS, n_tok), F32),),
        grid=(n_tok // tm,),
        in_specs=[
            pl.BlockSpec((tm, d_model), row),
            pl.BlockSpec((1, d_model), fixed),
            pl.BlockSpec(w_main.shape, fixed),
            pl.BlockSpec(w_gate_t.shape, fixed),
            pl.BlockSpec((tm, LANES), tab),
            pl.BlockSpec((tm, LANES), tab),
            pl.BlockSpec((tm, LANES), tab),
            pl.BlockSpec(conv_w.shape, fixed),
            pl.BlockSpec(conv_b.shape, fixed),
        ],
        out_specs=tuple(pl.BlockSpec((tm, ATT_WIDTH), row) for _ in range(n_wide))
        + (pl.BlockSpec((GATE_ROWS, tm), lambda i: (0, i)),),
        scratch_shapes=[
            pltpu.VMEM((SUBLANES, MLSTM_WIDTH), F32),
            pltpu.VMEM((SUBLANES, MLSTM_WIDTH), F32),
        ],
        compiler_params=pltpu.CompilerParams(
            dimension_semantics=("arbitrary",), vmem_limit_bytes=VMEM_LIMIT),
        name="input_projection",
    )(x2, pre_g, w_main, w_gate_t, *tables, conv_w, conv_b)


def _moba_kernel(qt_ref, k_ref, vt_ref, o_ref,
                 kst_hi, kst_mid, kst_lo, qta_ref, bias_ref, m_ref, acc_ref,
                 s0_ref, s1_ref, p0_ref, p1_ref, a0_ref, a1_ref, *, nb):
    i = pl.program_id(1)
    blk = MOBA_BLOCK
    hd = ATT_HEAD_DIM

    @pl.when(i == 0)
    def _():
        lane_head = lax.broadcasted_iota(jnp.int32, (nb, ATT_WIDTH), 1) // hd
        means = []
        for n in range(nb):
            means.append(jnp.mean(k_ref[0, n].astype(F32), axis=0, keepdims=True))
        km = jnp.concatenate(means, axis=0)
        for h in range(ATT_HEADS):
            hi, mid, lo = _split3(jnp.where(lane_head == h, km, 0.0))
            kst_hi[h * nb:(h + 1) * nb, :] = hi
            kst_mid[h * nb:(h + 1) * nb, :] = mid
            kst_lo[h * nb:(h + 1) * nb, :] = lo

    qt = qt_ref[0, 0]
    gates = _dot(kst_hi[...], qt) + _dot(kst_mid[...], qt) + _dot(kst_lo[...], qt)

    n_iota = lax.broadcasted_iota(jnp.int32, (nb, blk), 0)
    n_iota_f = n_iota.astype(F32)
    past = n_iota < i
    row_head = lax.broadcasted_iota(jnp.int32, (LANES, blk), 0) // hd
    for h in range(ATT_HEADS):
        g = jnp.where(past, gates[h * nb:(h + 1) * nb, :], -jnp.inf)
        sel = jnp.zeros((nb, blk), jnp.bool_)
        for _ in range(min(MOBA_TOPK, nb)):
            gmax = jnp.max(g, axis=0, keepdims=True)
            first = jnp.min(jnp.where(g == gmax, n_iota_f, float(nb)), axis=0, keepdims=True)
            hit = n_iota_f == first
            sel = sel | hit
            g = jnp.where(hit, -jnp.inf, g)
        bias_ref[h, 0:nb, :] = jnp.where(sel & past, 0.0, NEG_BIG)
        bias_ref[h, nb:nb + SUBLANES, :] = jnp.full((SUBLANES, blk), NEG_BIG, F32)
        p = h // 2
        qpair = qt[p * LANES:(p + 1) * LANES, :]
        qta_ref[h] = jnp.where(row_head == (h % 2), qpair, jnp.zeros_like(qpair))
        m_ref[h] = jnp.full((1, blk), NEG_BIG, F32)
        acc_ref[h] = jnp.zeros(acc_ref.shape[1:], F32)

    l_iota = lax.broadcasted_iota(jnp.int32, (blk, blk), 0)
    q_iota = lax.broadcasted_iota(jnp.int32, (blk, blk), 1)
    causal = l_iota <= q_iota
    ones_rows = jnp.ones((2 * SUBLANES, blk), BF16)

    def scores(block, s_ref, own=False):
        kb = k_ref[0, block]
        for h in range(ATT_HEADS):
            st = _dot(kb[:, (h // 2) * LANES:(h // 2 + 1) * LANES], qta_ref[h])
            s_ref[h] = jnp.where(causal, st, NEG_BIG) if own else st

    def softmax_update(bias_row, s_ref, p_ref, a_ref):
        for h in range(ATT_HEADS):
            st = s_ref[h]
            m_old = m_ref[h]
            if bias_row is None:
                m_new = jnp.maximum(m_old, jnp.max(st, axis=0, keepdims=True))
                shift = m_new
            else:
                b = bias_ref[h, pl.ds(bias_row, 1), :]
                m_new = jnp.maximum(m_old, jnp.max(st, axis=0, keepdims=True) + b)
                shift = m_new - b
            p_ref[h] = jnp.exp2(st - shift).astype(BF16)
            a_ref[h] = jnp.exp2(m_old - m_new)
            m_ref[h] = m_new

    def values(block, p_ref, a_ref):
        vtb = vt_ref[0, block]
        for h in range(ATT_HEADS):
            lhs = jnp.concatenate([vtb[h * hd:(h + 1) * hd, :], ones_rows], axis=0)
            acc_ref[h] = a_ref[h] * acc_ref[h] + _dot(lhs, p_ref[h])

    scores(i, s0_ref, own=True)
    scores(0, s1_ref)
    softmax_update(None, s0_ref, p0_ref, a0_ref)

    def body(u, carry):
        t = 2 * u + 2
        scores(jnp.minimum(t - 1, nb - 1), s0_ref)
        values(jnp.where(u == 0, i, t - 3), p0_ref, a0_ref)
        softmax_update(t - 2, s1_ref, p1_ref, a1_ref)
        scores(jnp.minimum(t, nb - 1), s1_ref)
        values(t - 2, p1_ref, a1_ref)
        softmax_update(t - 1, s0_ref, p0_ref, a0_ref)
        return carry

    lax.fori_loop(0, (i + 2) // 2, body, 0)

    for h in range(ATT_HEADS):
        acc = acc_ref[h]
        o_ref[0, 0, h * hd:(h + 1) * hd, :] = (acc[:hd, :] / acc[hd:hd + 1, :]).astype(o_ref.dtype)


def _moba_attention(qt4, k4, vt4):
    bsz, nb, width, blk = qt4.shape
    hn = ATT_HEADS * nb
    tile_f32 = pltpu.VMEM((ATT_HEADS, blk, blk), F32)
    tile_bf16 = pltpu.VMEM((ATT_HEADS, blk, blk), BF16)
    row_f32 = pltpu.VMEM((ATT_HEADS, 1, blk), F32)
    return pl.pallas_call(
        functools.partial(_moba_kernel, nb=nb),
        out_shape=jax.ShapeDtypeStruct((bsz, nb, width, blk), BF16),
        grid=(bsz, nb),
        in_specs=[
            pl.BlockSpec((1, 1, width, blk), lambda b, i: (b, i, 0, 0)),
            pl.BlockSpec((1, nb, blk, width), lambda b, i: (b, 0, 0, 0)),
            pl.BlockSpec((1, nb, width, blk), lambda b, i: (b, 0, 0, 0)),
        ],
        out_specs=pl.BlockSpec((1, 1, width, blk), lambda b, i: (b, i, 0, 0)),
        scratch_shapes=[
            pltpu.VMEM((hn, width), BF16),
            pltpu.VMEM((hn, width), BF16),
            pltpu.VMEM((hn, width), BF16),
            pltpu.VMEM((ATT_HEADS, LANES, blk), BF16),
            pltpu.VMEM((ATT_HEADS, nb + SUBLANES, blk), F32),
            row_f32,
            pltpu.VMEM((ATT_HEADS, ATT_HEAD_DIM + 2 * SUBLANES, blk), F32),
            tile_f32, tile_f32, tile_bf16, tile_bf16, row_f32, row_f32,
        ],
        compiler_params=pltpu.CompilerParams(
            dimension_semantics=("arbitrary", "arbitrary"), vmem_limit_bytes=VMEM_LIMIT),
        name="moba_attention",
    )(qt4, k4, vt4)


def _mlstm_kernel(qc_ref, kc_ref, mv_ref, og_ref, gate_ref, bias_ref, gain_ref, triu_ref, select_ref,
                  o_ref, ct_ref, n_ref, m_ref):
    c = pl.program_id(1)
    L = MLSTM_CHUNK
    hd = MLSTM_HEAD_DIM
    nh = MLSTM_HEADS

    @pl.when(c == 0)
    def _():
        ct_ref[...] = jnp.zeros_like(ct_ref)
        n_ref[...] = jnp.zeros_like(n_ref)
        m_ref[...] = jnp.zeros_like(m_ref)

    gr = gate_ref[...] + bias_ref[...]
    r_hi, r_mid, r_lo = _split3(_log_sigmoid(gr) * LOG2_E)
    triu = triu_ref[...]
    bcum = _dot(r_hi, triu) + _dot(r_mid, triu) + _dot(r_lo, triu)
    row = lax.broadcasted_iota(jnp.int32, gr.shape, 0)
    rows_ib = jnp.where(row < nh, gr * LOG2_E, bcum)
    rows_c = rows_ib[0:nh, :] - rows_ib[nh:2 * nh, :]
    lhs_t = jnp.concatenate(list(_split3(rows_ib)) + [jnp.ones((GATE_ROWS, L), BF16)], axis=0)

    t_iota = lax.broadcasted_iota(jnp.int32, (L, LANES), 0)
    s_iota = lax.broadcasted_iota(jnp.int32, (L, LANES), 1)
    zeros_tail = jnp.zeros((GATE_ROWS, 2 * LANES), BF16)

    gain = gain_ref[...]
    heads = range(MLSTM_HEADS)
    slices = [slice(h * hd, (h + 1) * hd) for h in heads]
    srow = lax.broadcasted_iota(jnp.int32, (GATE_ROWS, L), 0)

    res_all, qk_all, inter_all = [], [], []
    for h in heads:
        c = rows_c[h:h + 1, :]
        c_hi = c.astype(BF16).astype(F32)
        c_mid = (c - c_hi).astype(BF16).astype(F32)
        c_lo = c - c_hi - c_mid
        tail = jnp.where(srow == 0, c_hi, jnp.where(srow == 1, c_mid, jnp.where(srow == 2, c_lo, 0.0)))
        sel_h = jnp.concatenate(
            [select_ref[h], jnp.concatenate([tail.astype(BF16), zeros_tail], axis=1)], axis=0)
        res_all.append(lax.dot_general(lhs_t, sel_h, (((0,), (0,)), ((), ())), preferred_element_type=F32))
        qb = qc_ref[0, :, slices[h]]
        kb = kc_ref[0, :, slices[h]]
        qk_all.append(lax.dot_general(qb, kb, (((1,), (1,)), ((), ())), preferred_element_type=F32))
        inter_all.append(_dot(qb, ct_ref[h].astype(BF16)))

    s_all, wv_all, carry_all = [], [], []
    for h in heads:
        sl = slices[h]
        res = res_all[h]
        qk = qk_all[h]
        qb = qc_ref[0, :, sl]
        kb = kc_ref[0, :, sl]
        v = mv_ref[0, :, sl]
        m_prev = m_ref[h]
        n_prev = n_ref[h]
        b_col = res[:, L:L + LANES]
        i_col = res[:, L + LANES:L + 2 * LANES]

        dslabs = []
        for j in range(L // LANES):
            keep = (s_iota + j * LANES) <= t_iota
            dslabs.append(jnp.where(keep, res[:, j * LANES:(j + 1) * LANES], -jnp.inf))
        dmax = jnp.max(dslabs[0], axis=-1, keepdims=True)
        for d in dslabs[1:]:
            dmax = jnp.maximum(dmax, jnp.max(d, axis=-1, keepdims=True))
        inter = b_col + m_prev
        m_t = jnp.maximum(inter, dmax)
        w_inter = jnp.exp2(inter - m_t)
        s = jnp.concatenate([qk[:, j * LANES:(j + 1) * LANES] * jnp.exp2(d - m_t)
                             for j, d in enumerate(dslabs)], axis=1)
        nq = (w_inter * jnp.sum(qb.astype(F32) * n_prev, axis=-1, keepdims=True)
              + jnp.sum(s, axis=-1, keepdims=True))
        denom = jnp.maximum(jnp.abs(nq), jnp.exp2(-m_t))

        b_last = b_col[L - 1:L, :]
        g = b_last - b_col + i_col
        m_new = jnp.maximum(b_last + m_prev, jnp.max(g, axis=0, keepdims=True))
        decay = jnp.exp2(b_last + m_prev - m_new)
        w = jnp.exp2(g - m_new)
        n_ref[h] = decay * n_prev + jnp.sum(w * kb.astype(F32), axis=0, keepdims=True)
        m_ref[h] = m_new
        s_all.append(s.astype(BF16))
        wv_all.append((w * v.astype(F32)).astype(BF16))
        carry_all.append((w_inter, denom, decay))

    intra_all = [_dot(s_all[h], mv_ref[0, :, slices[h]]) for h in heads]
    for h in heads:
        ct_ref[h] = carry_all[h][2] * ct_ref[h] + lax.dot_general(
            kc_ref[0, :, slices[h]], wv_all[h], (((0,), (0,)), ((), ())), preferred_element_type=F32)

    for h in heads:
        sl = slices[h]
        w_inter, denom, _ = carry_all[h]
        hh = (w_inter * inter_all[h] + intra_all[h]) / denom
        mu = jnp.mean(hh, axis=-1, keepdims=True)
        var = jnp.mean(jnp.square(hh - mu), axis=-1, keepdims=True)
        hn = (hh - mu) * lax.rsqrt(var + NORM_EPS) * gain[:, sl]
        o_ref[0, :, sl] = (og_ref[0, :, sl].astype(F32) * hn).astype(o_ref.dtype)


def _mlstm(qc, kc, mv, og, gates_t, gate_bias, out_g):
    bsz, seq, width = qc.shape
    L = MLSTM_CHUNK
    nc = seq // L
    nh = MLSTM_HEADS
    triu = jnp.triu(jnp.ones((L, L), F32)).astype(BF16)
    r_in = jnp.arange(3 * GATE_ROWS)[None, :, None] % GATE_ROWS
    col = jnp.arange(L + 2 * LANES)[None, None, :]
    head = jnp.arange(nh)[:, None, None]
    select = jnp.where(col < L + LANES, r_in == nh + head, r_in == head).astype(BF16)
    tok = lambda b, c: (b, c, 0)
    fixed = lambda b, c: (0, 0)
    wide = pl.BlockSpec((1, L, width), tok)
    return pl.pallas_call(
        _mlstm_kernel,
        out_shape=jax.ShapeDtypeStruct((bsz, seq, width), BF16),
        grid=(bsz, nc),
        in_specs=[
            wide, wide, wide, wide,
            pl.BlockSpec((GATE_ROWS, L), lambda b, c: (0, b * nc + c)),
            pl.BlockSpec((GATE_ROWS, 1), fixed),
            pl.BlockSpec(out_g.shape, fixed),
            pl.BlockSpec((L, L), fixed),
            pl.BlockSpec(select.shape, lambda b, c: (0, 0, 0)),
        ],
        out_specs=wide,
        scratch_shapes=[
            pltpu.VMEM((nh, MLSTM_HEAD_DIM, MLSTM_HEAD_DIM), F32),
            pltpu.VMEM((nh, 1, MLSTM_HEAD_DIM), F32),
            pltpu.VMEM((nh, 1, LANES), F32),
        ],
        compiler_params=pltpu.CompilerParams(
            dimension_semantics=("arbitrary", "arbitrary"), vmem_limit_bytes=VMEM_LIMIT),
        name="mlstm",
    )(qc, kc, mv, og, gates_t, gate_bias, out_g, triu, select)


def _out_kernel(x_ref, o_ref, sz_ref, ym_ref, ag_ref, wa_ref, wm_ref, pg_ref, out_ref):
    o = o_ref[...].astype(F32)
    ms = jnp.mean(o * o, axis=-1, keepdims=True)
    ya = (o * lax.rsqrt(ms + NORM_EPS)) * ag_ref[...] * sz_ref[...].astype(F32)
    y = _dot(ya.astype(BF16), wa_ref[...]) + _dot(ym_ref[...], wm_ref[...])
    ms2 = jnp.mean(y * y, axis=-1, keepdims=True)
    out_ref[...] = x_ref[...] + (y * lax.rsqrt(ms2 + NORM_EPS)) * pg_ref[...]


def _output_projection(x2, o2, az, ym, attn_g, w_a, w_m, post_g):
    n_tok, d_model = x2.shape
    tm = PROJ_ROWS
    row = lambda i: (i, 0)
    fixed = lambda i: (0, 0)
    return pl.pallas_call(
        _out_kernel,
        out_shape=jax.ShapeDtypeStruct((n_tok, d_model), F32),
        grid=(n_tok // tm,),
        in_specs=[
            pl.BlockSpec((tm, d_model), row),
            pl.BlockSpec((tm, ATT_WIDTH), row),
            pl.BlockSpec((tm, ATT_WIDTH), row),
            pl.BlockSpec((tm, MLSTM_WIDTH), row),
            pl.BlockSpec((1, ATT_WIDTH), fixed),
            pl.BlockSpec(w_a.shape, fixed),
            pl.BlockSpec(w_m.shape, fixed),
            pl.BlockSpec((1, d_model), fixed),
        ],
        out_specs=pl.BlockSpec((tm, d_model), row),
        compiler_params=pltpu.CompilerParams(
            dimension_semantics=("arbitrary",), vmem_limit_bytes=VMEM_LIMIT),
        name="output_projection",
    )(x2, o2, az, ym, attn_g, w_a, w_m, post_g)


def _layer(x, pre_g, w_in, i_bias, f_bias, conv_w, conv_b, attn_g, mlstm_g, w_out, post_g):
    bsz, seq, d_model = x.shape
    assert seq % MOBA_BLOCK == 0 and seq % PROJ_ROWS == 0 and seq % MLSTM_CHUNK == 0
    nb = seq // MOBA_BLOCK
    n_tok = bsz * seq
    n_main = 4 * ATT_WIDTH + 5 * MLSTM_WIDTH
    n_gate = 2 * MLSTM_HEADS
    assert w_in.shape == (d_model, n_main + n_gate)

    w_main = w_in[:, :n_main].astype(BF16)
    w_gate_t = jnp.pad(w_in[:, n_main:].T, ((0, GATE_ROWS - n_gate), (0, 0))).astype(BF16)
    x2 = x.reshape(n_tok, d_model)
    (q, k, v, sz, qc, kc, mv, og, gates_t) = _input_projection(
        x2, pre_g[None, :], w_main, w_gate_t, _rope_tables(seq), conv_w, conv_b[None, :], seq)

    to_blocks_t = lambda t: t.reshape(bsz, nb, MOBA_BLOCK, ATT_WIDTH).transpose(0, 1, 3, 2)
    ot4 = _moba_attention(to_blocks_t(q), k.reshape(bsz, nb, MOBA_BLOCK, ATT_WIDTH), to_blocks_t(v))
    o2 = ot4.transpose(0, 1, 3, 2).reshape(n_tok, ATT_WIDTH)

    gate_bias = jnp.pad(jnp.concatenate([i_bias, f_bias]), (0, GATE_ROWS - n_gate))[:, None]
    shp = (bsz, seq, MLSTM_WIDTH)
    ym = _mlstm(qc.reshape(shp), kc.reshape(shp), mv.reshape(shp), og.reshape(shp),
                gates_t, gate_bias, mlstm_g[None, :])

    w_out_b = w_out.astype(BF16)
    out = _output_projection(x2, o2, sz, ym.reshape(n_tok, MLSTM_WIDTH), attn_g[None, :],
                             w_out_b[:ATT_WIDTH], w_out_b[ATT_WIDTH:], post_g[None, :])
    return out.reshape(bsz, seq, d_model)


def kernel(x, pre_norm_g, w_in, mlstm_i_bias, mlstm_f_bias, conv_w, conv_b, attn_out_g, mlstm_out_g,
           w_out, post_norm_g):
    for l in range(pre_norm_g.shape[0]):
        x = _layer(x, pre_norm_g[l], w_in[l], mlstm_i_bias[l], mlstm_f_bias[l], conv_w[l], conv_b[l],
                   attn_out_g[l], mlstm_out_g[l], w_out[l], post_norm_g[l])
    return x
```

```python
import functools

import jax
import jax.numpy as jnp
from jax import lax
from jax.experimental import pallas as pl
from jax.experimental.pallas import tpu as pltpu

F32 = jnp.float32
BF16 = jnp.bfloat16

ATT_HEADS = 8
ATT_HEAD_DIM = 64
ATT_WIDTH = ATT_HEADS * ATT_HEAD_DIM
ROT_DIM = ATT_HEAD_DIM // 4
ROPE_THETA = 500000.0
MOBA_BLOCK = 256
MOBA_TOPK = 3
MLSTM_HEADS = 4
MLSTM_HEAD_DIM = 128
MLSTM_WIDTH = MLSTM_HEADS * MLSTM_HEAD_DIM
CONV_WIDTH = 4
NORM_EPS = 1e-6

LANES = 128
SUBLANES = 8
GATE_ROWS = 16
PROJ_ROWS = 512
MLSTM_CHUNK = 256
NEG_BIG = -1e30
LOG2_E = 1.4426950408889634
_SKEW = 2
VMEM_LIMIT = 56 * 1024 * 1024


def _sigmoid(x):
    return 1.0 / (1.0 + jnp.exp(-x))


def _silu(x):
    return x * _sigmoid(x)


def _log_sigmoid(x):
    return jnp.minimum(x, 0.0) - jnp.log(1.0 + jnp.exp(-jnp.abs(x)))


def _split3(x):
    hi = x.astype(BF16)
    r1 = x - hi.astype(F32)
    mid = r1.astype(BF16)
    lo = (r1 - mid.astype(F32)).astype(BF16)
    return hi, mid, lo


def _dot(a, b):
    return jnp.dot(a, b, preferred_element_type=F32)


def _proj_kernel(x_ref, g_ref, w_ref, wgt_ref, cos_ref, sa_ref, sb_ref, cw_ref, cb_ref,
                 q_ref, k_ref, v_ref, sz_ref, qc_ref, kc_ref, mv_ref, og_ref, gate_ref,
                 pq_ref, pk_ref, *, seq_tiles):
    first_of_seq = (pl.program_id(0) % seq_tiles) == 0
    tm = x_ref.shape[0]
    halo = SUBLANES
    x = x_ref[...]
    ms = jnp.mean(x * x, axis=-1, keepdims=True)
    hb = ((x * lax.rsqrt(ms + NORM_EPS)) * g_ref[...]).astype(BF16)

    cos = cos_ref[...]
    sa = sa_ref[...]
    sb = sb_ref[...]

    def rope(acc):
        parts = []
        for c in range(ATT_WIDTH // LANES):
            xs = acc[:, c * LANES:(c + 1) * LANES]
            parts.append(xs * cos
                         + pltpu.roll(xs, LANES - ROT_DIM // 2, 1) * sa
                         + pltpu.roll(xs, ROT_DIM // 2, 1) * sb)
        return jnp.concatenate(parts, axis=1)

    def conv_silu(acc, prev_ref, w, b):
        prev = jnp.where(first_of_seq, 0.0, prev_ref[...])
        ext = jnp.concatenate([prev, acc], axis=0)
        prev_ref[...] = acc[tm - halo:, :]
        out = b
        for j in range(CONV_WIDTH):
            off = halo - (CONV_WIDTH - 1) + j
            out = out + w[j:j + 1, :] * ext[off:off + tm, :]
        return _silu(out)

    def proj(gi):
        return _dot(hb, w_ref[:, gi * ATT_WIDTH:(gi + 1) * ATT_WIDTH])

    cw = cw_ref[...]
    cb = cb_ref[...]
    q_ref[...] = (rope(proj(0)) * (LOG2_E * ATT_HEAD_DIM ** -0.5)).astype(q_ref.dtype)
    k_ref[...] = rope(proj(1)).astype(k_ref.dtype)
    v_ref[...] = proj(2).astype(v_ref.dtype)
    sz_ref[...] = _silu(proj(3)).astype(sz_ref.dtype)
    qc_ref[...] = conv_silu(proj(4), pq_ref, cw[:, :MLSTM_WIDTH], cb[:, :MLSTM_WIDTH]).astype(qc_ref.dtype)
    kc = conv_silu(proj(5), pk_ref, cw[:, MLSTM_WIDTH:], cb[:, MLSTM_WIDTH:])
    kc_ref[...] = (kc * (MLSTM_HEAD_DIM ** -0.5)).astype(kc_ref.dtype)
    mv_ref[...] = proj(6).astype(mv_ref.dtype)
    og_ref[...] = (_sigmoid(proj(7)) * _silu(proj(8))).astype(og_ref.dtype)
    gate_ref[...] = lax.dot_general(wgt_ref[...], hb, (((1,), (1,)), ((), ())),
                                    preferred_element_type=F32)


def _rope_tables(seq):
    half = ROT_DIM // 2
    inv_freq = jnp.power(ROPE_THETA, -jnp.arange(half, dtype=F32) * 2.0 / ROT_DIM)
    ang = jnp.arange(seq, dtype=jnp.int32).astype(F32)[:, None] * inv_freq[None, :]
    cos = jnp.cos(ang)
    sin = jnp.sin(ang)
    ones = jnp.ones((seq, ATT_HEAD_DIM - ROT_DIM), F32)
    zeros = jnp.zeros((seq, ATT_HEAD_DIM - ROT_DIM), F32)
    zh = jnp.zeros((seq, half), F32)
    cos_h = jnp.concatenate([cos, cos, ones], axis=1)
    sa_h = jnp.concatenate([-sin, zh, zeros], axis=1)
    sb_h = jnp.concatenate([zh, sin, zeros], axis=1)
    rep = LANES // ATT_HEAD_DIM
    return (jnp.tile(cos_h, (1, rep)), jnp.tile(sa_h, (1, rep)), jnp.tile(sb_h, (1, rep)))


def _input_projection(x2, pre_g, w_main, w_gate_t, tables, conv_w, conv_b, seq):
    n_tok, d_model = x2.shape
    tm = PROJ_ROWS
    seq_tiles = seq // tm
    row = lambda i: (i, 0)
    fixed = lambda i: (0, 0)
    tab = lambda i: (i % seq_tiles, 0)
    wide = jax.ShapeDtypeStruct((n_tok, ATT_WIDTH), BF16)
    n_wide = 8
    return pl.pallas_call(
        functools.partial(_proj_kernel, seq_tiles=seq_tiles),
        out_shape=(wide,) * n_wide + (jax.ShapeDtypeStruct((GATE_ROWS, n_tok), F32),),
        grid=(n_tok // tm,),
        in_specs=[
            pl.BlockSpec((tm, d_model), row),
            pl.BlockSpec((1, d_model), fixed),
            pl.BlockSpec(w_main.shape, fixed),
            pl.BlockSpec(w_gate_t.shape, fixed),
            pl.BlockSpec((tm, LANES), tab),
            pl.BlockSpec((tm, LANES), tab),
            pl.BlockSpec((tm, LANES), tab),
            pl.BlockSpec(conv_w.shape, fixed),
            pl.BlockSpec(conv_b.shape, fixed),
        ],
        out_specs=tuple(pl.BlockSpec((tm, ATT_WIDTH), row) for _ in range(n_wide))
        + (pl.BlockSpec((GATE_ROWS, tm), lambda i: (0, i)),),
        scratch_shapes=[
            pltpu.VMEM((SUBLANES, MLSTM_WIDTH), F32),
            pltpu.VMEM((SUBLANES, MLSTM_WIDTH), F32),
        ],
        compiler_params=pltpu.CompilerParams(
            dimension_semantics=("arbitrary",), vmem_limit_bytes=VMEM_LIMIT),
        name="input_projection",
    )(x2, pre_g, w_main, w_gate_t, *tables, conv_w, conv_b)


def _moba_kernel(own_tab, past_tab, qt_ref, k_ref, vt_ref, o_ref,
                 kst_hi, kst_mid, kst_lo, bias_ref, m_ref, acc_ref,
                 s0_ref, s1_ref, p0_ref, p1_ref, a0_ref, a1_ref,
                 *, nb, own_ticks, past_ticks):
    blk = MOBA_BLOCK
    hd = ATT_HEAD_DIM

    lane_head = lax.broadcasted_iota(jnp.int32, (nb, ATT_WIDTH), 1) // hd
    means = []
    for n in range(nb):
        means.append(jnp.mean(k_ref[0, n].astype(F32), axis=0, keepdims=True))
    km = jnp.concatenate(means, axis=0)
    for h in range(ATT_HEADS):
        hi, mid, lo = _split3(jnp.where(lane_head == h, km, 0.0))
        kst_hi[h * nb:(h + 1) * nb, :] = hi
        kst_mid[h * nb:(h + 1) * nb, :] = mid
        kst_lo[h * nb:(h + 1) * nb, :] = lo

    n_iota = lax.broadcasted_iota(jnp.int32, (nb, blk), 0)
    n_iota_f = n_iota.astype(F32)
    tail_iota = lax.broadcasted_iota(jnp.int32, (SUBLANES, blk), 0)
    tail_rows = jnp.where(tail_iota == 0, 0.0, NEG_BIG)

    def prepare(i, carry):
        qt = qt_ref[0, i]
        gates = _dot(kst_hi[...], qt) + _dot(kst_mid[...], qt) + _dot(kst_lo[...], qt)
        past = n_iota < i
        for h in range(ATT_HEADS):
            g = jnp.where(past, gates[h * nb:(h + 1) * nb, :], -jnp.inf)
            sel = jnp.zeros((nb, blk), jnp.bool_)
            for _ in range(min(MOBA_TOPK, nb)):
                gmax = jnp.max(g, axis=0, keepdims=True)
                first = jnp.min(jnp.where(g == gmax, n_iota_f, float(nb)), axis=0, keepdims=True)
                hit = n_iota_f == first
                sel = sel | hit
                g = jnp.where(hit, -jnp.inf, g)
            bias_ref[i, h, 0:nb, :] = jnp.where(sel & past, 0.0, NEG_BIG)
            bias_ref[i, h, nb:nb + SUBLANES, :] = tail_rows
            m_ref[i, h] = jnp.full((1, blk), NEG_BIG, F32)
            acc_ref[i, h] = jnp.zeros(acc_ref.shape[2:], F32)
        return carry

    lax.fori_loop(0, nb, prepare, 0)

    l_iota = lax.broadcasted_iota(jnp.int32, (blk, blk), 0)
    q_iota = lax.broadcasted_iota(jnp.int32, (blk, blk), 1)
    causal = l_iota <= q_iota
    row_head = lax.broadcasted_iota(jnp.int32, (LANES, blk), 0) // hd
    ones_rows = jnp.ones((2 * SUBLANES, blk), BF16)
    s_bufs = (s0_ref, s1_ref)
    p_bufs = (p0_ref, p1_ref)
    a_bufs = (a0_ref, a1_ref)
    assert len(s_bufs) == _SKEW

    def run(tab, n_ticks, own):
        def scores(tick, slot):
            qi = tab[0, tick]
            kb = k_ref[0, tab[1, tick]]
            for h in range(ATT_HEADS):
                pr = slice((h // 2) * LANES, (h // 2 + 1) * LANES)
                qpair = qt_ref[0, qi, pr, :]
                qh = jnp.where(row_head == (h % 2), qpair, jnp.zeros_like(qpair))
                st = _dot(kb[:, pr], qh)
                s_bufs[slot][h] = jnp.where(causal, st, NEG_BIG) if own else st

        def softmax_update(tick, slot):
            qi = tab[0, tick]
            mask_row = tab[2, tick]
            for h in range(ATT_HEADS):
                st = s_bufs[slot][h]
                b = bias_ref[qi, h, pl.ds(mask_row, 1), :]
                m_old = m_ref[qi, h]
                m_new = jnp.maximum(m_old, jnp.max(st, axis=0, keepdims=True) + b)
                p_bufs[slot][h] = jnp.exp2(st - (m_new - b)).astype(BF16)
                a_bufs[slot][h] = jnp.exp2(m_old - m_new)
                m_ref[qi, h] = m_new

        def values(tick, slot):
            qi = tab[0, tick]
            vtb = vt_ref[0, tab[1, tick]]
            for h in range(ATT_HEADS):
                lhs = jnp.concatenate([vtb[h * hd:(h + 1) * hd, :], ones_rows], axis=0)
                acc_ref[qi, h] = a_bufs[slot][h] * acc_ref[qi, h] + _dot(lhs, p_bufs[slot][h])

        for t in range(_SKEW):
            scores(t, t % _SKEW)
        for t in range(_SKEW, 2 * _SKEW):
            softmax_update(t - _SKEW, t % _SKEW)
            scores(t, t % _SKEW)

        def body(u, carry):
            t = _SKEW * (u + 2)
            for d in range(_SKEW):
                values(t + d - 2 * _SKEW, d)
                softmax_update(t + d - _SKEW, d)
                scores(t + d, d)
            return carry

        lax.fori_loop(0, -(-n_ticks // _SKEW), body, 0)

    run(own_tab, own_ticks, True)
    run(past_tab, past_ticks, False)

    def finalize(i, carry):
        for h in range(ATT_HEADS):
            acc = acc_ref[i, h]
            o_ref[0, i, h * hd:(h + 1) * hd, :] = (acc[:hd, :] / acc[hd:hd + 1, :]).astype(o_ref.dtype)
        return carry

    lax.fori_loop(0, nb, finalize, 0)


def _tick_table(ticks, n_ticks, pad):
    length = _SKEW * (-(-n_ticks // _SKEW) + 2)
    rows = list(ticks) + [pad] * (length - len(ticks))
    return jnp.asarray(rows, jnp.int32).T


def _moba_attention(qt4, k4, vt4):
    bsz, nb, width, blk = qt4.shape
    hn = ATT_HEADS * nb
    own = [(i, i, nb) for i in range(nb)]
    past = [(i, n, n) for i in range(nb) for n in range(i)]
    own_tab = _tick_table(own, len(own), (nb - 1, nb - 1, nb + 1))
    past_tab = _tick_table(past, len(past), (nb - 1, 0, nb + 1))
    tile_f32 = pltpu.VMEM((ATT_HEADS, blk, blk), F32)
    tile_bf16 = pltpu.VMEM((ATT_HEADS, blk, blk), BF16)
    row_f32 = pltpu.VMEM((ATT_HEADS, 1, blk), F32)
    whole = lambda b, *_: (b, 0, 0, 0)
    return pl.pallas_call(
        functools.partial(_moba_kernel, nb=nb, own_ticks=len(own), past_ticks=len(past)),
        out_shape=jax.ShapeDtypeStruct((bsz, nb, width, blk), BF16),
        grid_spec=pltpu.PrefetchScalarGridSpec(
            num_scalar_prefetch=2,
            grid=(bsz,),
            in_specs=[
                pl.BlockSpec((1, nb, width, blk), whole),
                pl.BlockSpec((1, nb, blk, width), whole),
                pl.BlockSpec((1, nb, width, blk), whole),
            ],
            out_specs=pl.BlockSpec((1, nb, width, blk), whole),
            scratch_shapes=[
                pltpu.VMEM((hn, width), BF16),
                pltpu.VMEM((hn, width), BF16),
                pltpu.VMEM((hn, width), BF16),
                pltpu.VMEM((nb, ATT_HEADS, nb + SUBLANES, blk), F32),
                pltpu.VMEM((nb, ATT_HEADS, 1, blk), F32),
                pltpu.VMEM((nb, ATT_HEADS, ATT_HEAD_DIM + 2 * SUBLANES, blk), F32),
                tile_f32, tile_f32, tile_bf16, tile_bf16, row_f32, row_f32,
            ],
        ),
        compiler_params=pltpu.CompilerParams(
            dimension_semantics=("arbitrary",), vmem_limit_bytes=VMEM_LIMIT),
        name="moba_attention",
    )(own_tab, past_tab, qt4, k4, vt4)


def _mlstm_kernel(qc_ref, kc_ref, mv_ref, og_ref, gate_ref, bias_ref, gain_ref, triu_ref, select_ref,
                  o_ref, ct_ref, n_ref, m_ref):
    c = pl.program_id(1)
    L = MLSTM_CHUNK
    hd = MLSTM_HEAD_DIM
    nh = MLSTM_HEADS

    @pl.when(c == 0)
    def _():
        ct_ref[...] = jnp.zeros_like(ct_ref)
        n_ref[...] = jnp.zeros_like(n_ref)
        m_ref[...] = jnp.zeros_like(m_ref)

    gr = gate_ref[...] + bias_ref[...]
    r_hi, r_mid, r_lo = _split3(_log_sigmoid(gr) * LOG2_E)
    triu = triu_ref[...]
    bcum = _dot(r_hi, triu) + _dot(r_mid, triu) + _dot(r_lo, triu)
    row = lax.broadcasted_iota(jnp.int32, gr.shape, 0)
    rows_ib = jnp.where(row < nh, gr * LOG2_E, bcum)
    rows_c = rows_ib[0:nh, :] - rows_ib[nh:2 * nh, :]
    lhs_t = jnp.concatenate(list(_split3(rows_ib)) + [jnp.ones((GATE_ROWS, L), BF16)], axis=0)

    t_iota = lax.broadcasted_iota(jnp.int32, (L, LANES), 0)
    s_iota = lax.broadcasted_iota(jnp.int32, (L, LANES), 1)
    zeros_tail = jnp.zeros((GATE_ROWS, 2 * LANES), BF16)

    gain = gain_ref[...]
    heads = range(MLSTM_HEADS)
    slices = [slice(h * hd, (h + 1) * hd) for h in heads]
    srow = lax.broadcasted_iota(jnp.int32, (GATE_ROWS, L), 0)

    res_all, qk_all, inter_all = [], [], []
    for h in heads:
        c = rows_c[h:h + 1, :]
        c_hi = c.astype(BF16).astype(F32)
        c_mid = (c - c_hi).astype(BF16).astype(F32)
        c_lo = c - c_hi - c_mid
        tail = jnp.where(srow == 0, c_hi, jnp.where(srow == 1, c_mid, jnp.where(srow == 2, c_lo, 0.0)))
        sel_h = jnp.concatenate(
            [select_ref[h], jnp.concatenate([tail.astype(BF16), zeros_tail], axis=1)], axis=0)
        res_all.append(lax.dot_general(lhs_t, sel_h, (((0,), (0,)), ((), ())), preferred_element_type=F32))
        qb = qc_ref[0, :, slices[h]]
        kb = kc_ref[0, :, slices[h]]
        qk_all.append(lax.dot_general(qb, kb, (((1,), (1,)), ((), ())), preferred_element_type=F32))
        inter_all.append(_dot(qb, ct_ref[h].astype(BF16)))

    s_all, wv_all, carry_all = [], [], []
    for h in heads:
        sl = slices[h]
        res = res_all[h]
        qk = qk_all[h]
        qb = qc_ref[0, :, sl]
        kb = kc_ref[0, :, sl]
        v = mv_ref[0, :, sl]
        m_prev = m_ref[h]
        n_prev = n_ref[h]
        b_col = res[:, L:L + LANES]
        i_col = res[:, L + LANES:L + 2 * LANES]

        dslabs = []
        for j in range(L // LANES):
            keep = (s_iota + j * LANES) <= t_iota
            dslabs.append(jnp.where(keep, res[:, j * LANES:(j + 1) * LANES], -jnp.inf))
        dmax = jnp.max(dslabs[0], axis=-1, keepdims=True)
        for d in dslabs[1:]:
            dmax = jnp.maximum(dmax, jnp.max(d, axis=-1, keepdims=True))
        inter = b_col + m_prev
        m_t = jnp.maximum(inter, dmax)
        w_inter = jnp.exp2(inter - m_t)
        s = jnp.concatenate([qk[:, j * LANES:(j + 1) * LANES] * jnp.exp2(d - m_t)
                             for j, d in enumerate(dslabs)], axis=1)
        nq = (w_inter * jnp.sum(qb.astype(F32) * n_prev, axis=-1, keepdims=True)
              + jnp.sum(s, axis=-1, keepdims=True))
        denom = jnp.maximum(jnp.abs(nq), jnp.exp2(-m_t))

        b_last = b_col[L - 1:L, :]
        g = b_last - b_col + i_col
        m_new = jnp.maximum(b_last + m_prev, jnp.max(g, axis=0, keepdims=True))
        decay = jnp.exp2(b_last + m_prev - m_new)
        w = jnp.exp2(g - m_new)
        n_ref[h] = decay * n_prev + jnp.sum(w * kb.astype(F32), axis=0, keepdims=True)
        m_ref[h] = m_new
        s_all.append(s.astype(BF16))
        wv_all.append((w * v.astype(F32)).astype(BF16))
        carry_all.append((w_inter, denom, decay))

    intra_all = [_dot(s_all[h], mv_ref[0, :, slices[h]]) for h in heads]
    for h in heads:
        ct_ref[h] = carry_all[h][2] * ct_ref[h] + lax.dot_general(
            kc_ref[0, :, slices[h]], wv_all[h], (((0,), (0,)), ((), ())), preferred_element_type=F32)

    for h in heads:
        sl = slices[h]
        w_inter, denom, _ = carry_all[h]
        hh = (w_inter * inter_all[h] + intra_all[h]) / denom
        mu = jnp.mean(hh, axis=-1, keepdims=True)
        var = jnp.mean(jnp.square(hh - mu), axis=-1, keepdims=True)
        hn = (hh - mu) * lax.rsqrt(var + NORM_EPS) * gain[:, sl]
        o_ref[0, :, sl] = (og_ref[0, :, sl].astype(F32) * hn).astype(o_ref.dtype)


def _mlstm(qc, kc, mv, og, gates_t, gate_bias, out_g):
    bsz, seq, width = qc.shape
    L = MLSTM_CHUNK
    nc = seq // L
    nh = MLSTM_HEADS
    triu = jnp.triu(jnp.ones((L, L), F32)).astype(BF16)
    r_in = jnp.arange(3 * GATE_ROWS)[None, :, None] % GATE_ROWS
    col = jnp.arange(L + 2 * LANES)[None, None, :]
    head = jnp.arange(nh)[:, None, None]
    select = jnp.where(col < L + LANES, r_in == nh + head, r_in == head).astype(BF16)
    tok = lambda b, c: (b, c, 0)
    fixed = lambda b, c: (0, 0)
    wide = pl.BlockSpec((1, L, width), tok)
    return pl.pallas_call(
        _mlstm_kernel,
        out_shape=jax.ShapeDtypeStruct((bsz, seq, width), BF16),
        grid=(bsz, nc),
        in_specs=[
            wide, wide, wide, wide,
            pl.BlockSpec((GATE_ROWS, L), lambda b, c: (0, b * nc + c)),
            pl.BlockSpec((GATE_ROWS, 1), fixed),
            pl.BlockSpec(out_g.shape, fixed),
            pl.BlockSpec((L, L), fixed),
            pl.BlockSpec(select.shape, lambda b, c: (0, 0, 0)),
        ],
        out_specs=wide,
        scratch_shapes=[
            pltpu.VMEM((nh, MLSTM_HEAD_DIM, MLSTM_HEAD_DIM), F32),
            pltpu.VMEM((nh, 1, MLSTM_HEAD_DIM), F32),
            pltpu.VMEM((nh, 1, LANES), F32),
        ],
        compiler_params=pltpu.CompilerParams(
            dimension_semantics=("arbitrary", "arbitrary"), vmem_limit_bytes=VMEM_LIMIT),
        name="mlstm",
    )(qc, kc, mv, og, gates_t, gate_bias, out_g, triu, select)


def _out_kernel(x_ref, o_ref, sz_ref, ym_ref, ag_ref, wa_ref, wm_ref, pg_ref, out_ref):
    o = o_ref[...].astype(F32)
    ms = jnp.mean(o * o, axis=-1, keepdims=True)
    ya = (o * lax.rsqrt(ms + NORM_EPS)) * ag_ref[...] * sz_ref[...].astype(F32)
    y = _dot(ya.astype(BF16), wa_ref[...]) + _dot(ym_ref[...], wm_ref[...])
    ms2 = jnp.mean(y * y, axis=-1, keepdims=True)
    out_ref[...] = x_ref[...] + (y * lax.rsqrt(ms2 + NORM_EPS)) * pg_ref[...]


def _output_projection(x2, o2, az, ym, attn_g, w_a, w_m, post_g):
    n_tok, d_model = x2.shape
    tm = PROJ_ROWS
    row = lambda i: (i, 0)
    fixed = lambda i: (0, 0)
    return pl.pallas_call(
        _out_kernel,
        out_shape=jax.ShapeDtypeStruct((n_tok, d_model), F32),
        grid=(n_tok // tm,),
        in_specs=[
            pl.BlockSpec((tm, d_model), row),
            pl.BlockSpec((tm, ATT_WIDTH), row),
            pl.BlockSpec((tm, ATT_WIDTH), row),
            pl.BlockSpec((tm, MLSTM_WIDTH), row),
            pl.BlockSpec((1, ATT_WIDTH), fixed),
            pl.BlockSpec(w_a.shape, fixed),
            pl.BlockSpec(w_m.shape, fixed),
            pl.BlockSpec((1, d_model), fixed),
        ],
        out_specs=pl.BlockSpec((tm, d_model), row),
        compiler_params=pltpu.CompilerParams(
            dimension_semantics=("arbitrary",), vmem_limit_bytes=VMEM_LIMIT),
        name="output_projection",
    )(x2, o2, az, ym, attn_g, w_a, w_m, post_g)


def _layer(x, pre_g, w_in, i_bias, f_bias, conv_w, conv_b, attn_g, mlstm_g, w_out, post_g):
    bsz, seq, d_model = x.shape
    assert seq % MOBA_BLOCK == 0 and seq % PROJ_ROWS == 0 and seq % MLSTM_CHUNK == 0
    nb = seq // MOBA_BLOCK
    n_tok = bsz * seq
    n_main = 4 * ATT_WIDTH + 5 * MLSTM_WIDTH
    n_gate = 2 * MLSTM_HEADS
    assert w_in.shape == (d_model, n_main + n_gate)

    w_main = w_in[:, :n_main].astype(BF16)
    w_gate_t = jnp.pad(w_in[:, n_main:].T, ((0, GATE_ROWS - n_gate), (0, 0))).astype(BF16)
    x2 = x.reshape(n_tok, d_model)
    (q, k, v, sz, qc, kc, mv, og, gates_t) = _input_projection(
        x2, pre_g[None, :], w_main, w_gate_t, _rope_tables(seq), conv_w, conv_b[None, :], seq)

    to_blocks_t = lambda t: t.reshape(bsz, nb, MOBA_BLOCK, ATT_WIDTH).transpose(0, 1, 3, 2)
    ot4 = _moba_attention(to_blocks_t(q), k.reshape(bsz, nb, MOBA_BLOCK, ATT_WIDTH), to_blocks_t(v))
    o2 = ot4.transpose(0, 1, 3, 2).reshape(n_tok, ATT_WIDTH)

    gate_bias = jnp.pad(jnp.concatenate([i_bias, f_bias]), (0, GATE_ROWS - n_gate))[:, None]
    shp = (bsz, seq, MLSTM_WIDTH)
    ym = _mlstm(qc.reshape(shp), kc.reshape(shp), mv.reshape(shp), og.reshape(shp),
                gates_t, gate_bias, mlstm_g[None, :])

    w_out_b = w_out.astype(BF16)
    out = _output_projection(x2, o2, sz, ym.reshape(n_tok, MLSTM_WIDTH), attn_g[None, :],
                             w_out_b[:ATT_WIDTH], w_out_b[ATT_WIDTH:], post_g[None, :])
    return out.reshape(bsz, seq, d_model)


def kernel(x, pre_norm_g, w_in, mlstm_i_bias, mlstm_f_bias, conv_w, conv_b, attn_out_g, mlstm_out_g,
           w_out, post_norm_g):
    for l in range(pre_norm_g.shape[0]):
        x = _layer(x, pre_norm_g[l], w_in[l], mlstm_i_bias[l], mlstm_f_bias[l], conv_w[l], conv_b[l],
                   attn_out_g[l], mlstm_out_g[l], w_out[l], post_norm_g[l])
    return x
```

```python
import functools

import jax
import jax.numpy as jnp
from jax import lax
from jax.experimental import pallas as pl
from jax.experimental.pallas import tpu as pltpu

F32 = jnp.float32
BF16 = jnp.bfloat16

ATT_HEADS = 8
ATT_HEAD_DIM = 64
ATT_WIDTH = ATT_HEADS * ATT_HEAD_DIM
ROT_DIM = ATT_HEAD_DIM // 4
ROPE_THETA = 500000.0
MOBA_BLOCK = 256
MOBA_TOPK = 3
MLSTM_HEADS = 4
MLSTM_HEAD_DIM = 128
MLSTM_WIDTH = MLSTM_HEADS * MLSTM_HEAD_DIM
CONV_WIDTH = 4
NORM_EPS = 1e-6

LANES = 128
SUBLANES = 8
GATE_ROWS = 16
PROJ_ROWS = 512
MLSTM_CHUNK = 256
NEG_BIG = -1e30
LOG2_E = 1.4426950408889634
_SKEW = 2
VMEM_LIMIT = 56 * 1024 * 1024


def _sigmoid(x):
    return 1.0 / (1.0 + jnp.exp(-x))


def _silu(x):
    return x * _sigmoid(x)


def _log_sigmoid(x):
    return jnp.minimum(x, 0.0) - jnp.log(1.0 + jnp.exp(-jnp.abs(x)))


def _split3(x):
    hi = x.astype(BF16)
    r1 = x - hi.astype(F32)
    mid = r1.astype(BF16)
    lo = (r1 - mid.astype(F32)).astype(BF16)
    return hi, mid, lo


def _dot(a, b):
    return jnp.dot(a, b, preferred_element_type=F32)


def _proj_kernel(x_ref, g_ref, w_ref, wgt_ref, cos_ref, sa_ref, sb_ref, cw_ref, cb_ref,
                 q_ref, k_ref, v_ref, sz_ref, qc_ref, kc_ref, mv_ref, og_ref, gate_ref,
                 pq_ref, pk_ref, *, seq_tiles):
    first_of_seq = (pl.program_id(0) % seq_tiles) == 0
    tm = x_ref.shape[0]
    halo = SUBLANES
    x = x_ref[...]
    ms = jnp.mean(x * x, axis=-1, keepdims=True)
    hb = ((x * lax.rsqrt(ms + NORM_EPS)) * g_ref[...]).astype(BF16)

    cos = cos_ref[...]
    sa = sa_ref[...]
    sb = sb_ref[...]

    def rope(acc):
        parts = []
        for c in range(ATT_WIDTH // LANES):
            xs = acc[:, c * LANES:(c + 1) * LANES]
            parts.append(xs * cos
                         + pltpu.roll(xs, LANES - ROT_DIM // 2, 1) * sa
                         + pltpu.roll(xs, ROT_DIM // 2, 1) * sb)
        return jnp.concatenate(parts, axis=1)

    def conv_silu(acc, prev_ref, w, b):
        prev = jnp.where(first_of_seq, 0.0, prev_ref[...])
        ext = jnp.concatenate([prev, acc], axis=0)
        prev_ref[...] = acc[tm - halo:, :]
        out = b
        for j in range(CONV_WIDTH):
            off = halo - (CONV_WIDTH - 1) + j
            out = out + w[j:j + 1, :] * ext[off:off + tm, :]
        return _silu(out)

    def proj(gi):
        return _dot(hb, w_ref[:, gi * ATT_WIDTH:(gi + 1) * ATT_WIDTH])

    cw = cw_ref[...]
    cb = cb_ref[...]
    q_ref[...] = (rope(proj(0)) * (LOG2_E * ATT_HEAD_DIM ** -0.5)).astype(q_ref.dtype)
    k_ref[...] = rope(proj(1)).astype(k_ref.dtype)
    v_ref[...] = proj(2).astype(v_ref.dtype)
    sz_ref[...] = _silu(proj(3)).astype(sz_ref.dtype)
    qc_ref[...] = conv_silu(proj(4), pq_ref, cw[:, :MLSTM_WIDTH], cb[:, :MLSTM_WIDTH]).astype(qc_ref.dtype)
    kc = conv_silu(proj(5), pk_ref, cw[:, MLSTM_WIDTH:], cb[:, MLSTM_WIDTH:])
    kc_ref[...] = (kc * (MLSTM_HEAD_DIM ** -0.5)).astype(kc_ref.dtype)
    mv_ref[...] = proj(6).astype(mv_ref.dtype)
    og_ref[...] = (_sigmoid(proj(7)) * _silu(proj(8))).astype(og_ref.dtype)
    gate_ref[...] = lax.dot_general(wgt_ref[...], hb, (((1,), (1,)), ((), ())),
                                    preferred_element_type=F32)


def _rope_tables(seq):
    half = ROT_DIM // 2
    inv_freq = jnp.power(ROPE_THETA, -jnp.arange(half, dtype=F32) * 2.0 / ROT_DIM)
    ang = jnp.arange(seq, dtype=jnp.int32).astype(F32)[:, None] * inv_freq[None, :]
    cos = jnp.cos(ang)
    sin = jnp.sin(ang)
    ones = jnp.ones((seq, ATT_HEAD_DIM - ROT_DIM), F32)
    zeros = jnp.zeros((seq, ATT_HEAD_DIM - ROT_DIM), F32)
    zh = jnp.zeros((seq, half), F32)
    cos_h = jnp.concatenate([cos, cos, ones], axis=1)
    sa_h = jnp.concatenate([-sin, zh, zeros], axis=1)
    sb_h = jnp.concatenate([zh, sin, zeros], axis=1)
    rep = LANES // ATT_HEAD_DIM
    return (jnp.tile(cos_h, (1, rep)), jnp.tile(sa_h, (1, rep)), jnp.tile(sb_h, (1, rep)))


def _input_projection(x2, pre_g, w_main, w_gate_t, tables, conv_w, conv_b, seq):
    n_tok, d_model = x2.shape
    tm = PROJ_ROWS
    seq_tiles = seq // tm
    row = lambda i: (i, 0)
    fixed = lambda i: (0, 0)
    tab = lambda i: (i % seq_tiles, 0)
    wide = jax.ShapeDtypeStruct((n_tok, ATT_WIDTH), BF16)
    n_wide = 8
    return pl.pallas_call(
        functools.partial(_proj_kernel, seq_tiles=seq_tiles),
        out_shape=(wide,) * n_wide + (jax.ShapeDtypeStruct((GATE_ROWS, n_tok), F32),),
        grid=(n_tok // tm,),
        in_specs=[
            pl.BlockSpec((tm, d_model), row),
            pl.BlockSpec((1, d_model), fixed),
            pl.BlockSpec(w_main.shape, fixed),
            pl.BlockSpec(w_gate_t.shape, fixed),
            pl.BlockSpec((tm, LANES), tab),
            pl.BlockSpec((tm, LANES), tab),
            pl.BlockSpec((tm, LANES), tab),
            pl.BlockSpec(conv_w.shape, fixed),
            pl.BlockSpec(conv_b.shape, fixed),
        ],
        out_specs=tuple(pl.BlockSpec((tm, ATT_WIDTH), row) for _ in range(n_wide))
        + (pl.BlockSpec((GATE_ROWS, tm), lambda i: (0, i)),),
        scratch_shapes=[
            pltpu.VMEM((SUBLANES, MLSTM_WIDTH), F32),
            pltpu.VMEM((SUBLANES, MLSTM_WIDTH), F32),
        ],
        compiler_params=pltpu.CompilerParams(
            dimension_semantics=("arbitrary",), vmem_limit_bytes=VMEM_LIMIT),
        name="input_projection",
    )(x2, pre_g, w_main, w_gate_t, *tables, conv_w, conv_b)


def _moba_kernel(own_tab, past_tab, qt_ref, k_ref, vt_ref, o_ref,
                 kst_hi, kst_mid, kst_lo, bias_ref, m_ref, acc_ref,
                 *stage_bufs, nb, own_ticks, past_ticks):
    blk = MOBA_BLOCK
    hd = ATT_HEAD_DIM

    lane_head = lax.broadcasted_iota(jnp.int32, (nb, ATT_WIDTH), 1) // hd
    means = []
    for n in range(nb):
        means.append(jnp.mean(k_ref[0, n].astype(F32), axis=0, keepdims=True))
    km = jnp.concatenate(means, axis=0)
    for h in range(ATT_HEADS):
        hi, mid, lo = _split3(jnp.where(lane_head == h, km, 0.0))
        kst_hi[h * nb:(h + 1) * nb, :] = hi
        kst_mid[h * nb:(h + 1) * nb, :] = mid
        kst_lo[h * nb:(h + 1) * nb, :] = lo

    n_iota = lax.broadcasted_iota(jnp.int32, (nb, blk), 0)
    n_iota_f = n_iota.astype(F32)
    tail_iota = lax.broadcasted_iota(jnp.int32, (SUBLANES, blk), 0)
    tail_rows = jnp.where(tail_iota == 0, 0.0, NEG_BIG)

    def prepare(i, carry):
        qt = qt_ref[0, i]
        gates = _dot(kst_hi[...], qt) + _dot(kst_mid[...], qt) + _dot(kst_lo[...], qt)
        past = n_iota < i
        for h in range(ATT_HEADS):
            g = jnp.where(past, gates[h * nb:(h + 1) * nb, :], -jnp.inf)
            sel = jnp.zeros((nb, blk), jnp.bool_)
            for _ in range(min(MOBA_TOPK, nb)):
                gmax = jnp.max(g, axis=0, keepdims=True)
                first = jnp.min(jnp.where(g == gmax, n_iota_f, float(nb)), axis=0, keepdims=True)
                hit = n_iota_f == first
                sel = sel | hit
                g = jnp.where(hit, -jnp.inf, g)
            bias_ref[i, h, 0:nb, :] = jnp.where(sel & past, 0.0, NEG_BIG)
            bias_ref[i, h, nb:nb + SUBLANES, :] = tail_rows
            m_ref[i, h] = jnp.full((1, blk), NEG_BIG, F32)
            acc_ref[i, h] = jnp.zeros(acc_ref.shape[2:], F32)
        return carry

    lax.fori_loop(0, nb, prepare, 0)

    l_iota = lax.broadcasted_iota(jnp.int32, (blk, blk), 0)
    q_iota = lax.broadcasted_iota(jnp.int32, (blk, blk), 1)
    causal = l_iota <= q_iota
    row_head = lax.broadcasted_iota(jnp.int32, (LANES, blk), 0) // hd
    ones_rows = jnp.ones((2 * SUBLANES, blk), BF16)
    s_bufs, c_bufs, p_bufs, a_bufs = (stage_bufs[j * _SKEW:(j + 1) * _SKEW] for j in range(4))

    def run(tab, n_ticks, own):
        def scores(tick, slot):
            qi = tab[0, tick]
            kb = k_ref[0, tab[1, tick]]
            for h in range(ATT_HEADS):
                pr = slice((h // 2) * LANES, (h // 2 + 1) * LANES)
                qpair = qt_ref[0, qi, pr, :]
                qh = jnp.where(row_head == (h % 2), qpair, jnp.zeros_like(qpair))
                st = _dot(kb[:, pr], qh)
                st = jnp.where(causal, st, NEG_BIG) if own else st
                s_bufs[slot][h] = st
                c_bufs[slot][h] = jnp.max(st.reshape(blk // SUBLANES, SUBLANES, blk), axis=0)

        def softmax_update(tick, slot):
            qi = tab[0, tick]
            mask_row = tab[2, tick]
            for h in range(ATT_HEADS):
                st = s_bufs[slot][h]
                b = bias_ref[qi, h, pl.ds(mask_row, 1), :]
                m_old = m_ref[qi, h]
                m_new = jnp.maximum(m_old, jnp.max(c_bufs[slot][h], axis=0, keepdims=True) + b)
                p_bufs[slot][h] = jnp.exp2(st - (m_new - b)).astype(BF16)
                a_bufs[slot][h] = jnp.exp2(m_old - m_new)
                m_ref[qi, h] = m_new

        def values(tick, slot):
            qi = tab[0, tick]
            vtb = vt_ref[0, tab[1, tick]]
            for h in range(ATT_HEADS):
                lhs = jnp.concatenate([vtb[h * hd:(h + 1) * hd, :], ones_rows], axis=0)
                acc_ref[qi, h] = a_bufs[slot][h] * acc_ref[qi, h] + _dot(lhs, p_bufs[slot][h])

        for t in range(_SKEW):
            scores(t, t % _SKEW)
        for t in range(_SKEW, 2 * _SKEW):
            softmax_update(t - _SKEW, t % _SKEW)
            scores(t, t % _SKEW)

        def body(u, carry):
            t = _SKEW * (u + 2)
            for d in range(_SKEW):
                values(t + d - 2 * _SKEW, d)
                softmax_update(t + d - _SKEW, d)
                scores(t + d, d)
            return carry

        lax.fori_loop(0, -(-n_ticks // _SKEW), body, 0)

    run(own_tab, own_ticks, True)
    run(past_tab, past_ticks, False)

    def finalize(i, carry):
        for h in range(ATT_HEADS):
            acc = acc_ref[i, h]
            o_ref[0, i, h * hd:(h + 1) * hd, :] = (acc[:hd, :] / acc[hd:hd + 1, :]).astype(o_ref.dtype)
        return carry

    lax.fori_loop(0, nb, finalize, 0)


def _tick_table(ticks, n_ticks, pad):
    length = _SKEW * (-(-n_ticks // _SKEW) + 2)
    rows = list(ticks) + [pad] * (length - len(ticks))
    return jnp.asarray(rows, jnp.int32).T


def _moba_attention(qt4, k4, vt4):
    bsz, nb, width, blk = qt4.shape
    hn = ATT_HEADS * nb
    own = [(i, i, nb) for i in range(nb)]
    past = [(i, n, n) for i in range(nb) for n in range(i)]
    own_tab = _tick_table(own, len(own), (nb - 1, nb - 1, nb + 1))
    past_tab = _tick_table(past, len(past), (nb - 1, 0, nb + 1))
    tile_f32 = pltpu.VMEM((ATT_HEADS, blk, blk), F32)
    tile_bf16 = pltpu.VMEM((ATT_HEADS, blk, blk), BF16)
    row_f32 = pltpu.VMEM((ATT_HEADS, 1, blk), F32)
    part_f32 = pltpu.VMEM((ATT_HEADS, SUBLANES, blk), F32)
    whole = lambda b, *_: (b, 0, 0, 0)
    return pl.pallas_call(
        functools.partial(_moba_kernel, nb=nb, own_ticks=len(own), past_ticks=len(past)),
        out_shape=jax.ShapeDtypeStruct((bsz, nb, width, blk), BF16),
        grid_spec=pltpu.PrefetchScalarGridSpec(
            num_scalar_prefetch=2,
            grid=(bsz,),
            in_specs=[
                pl.BlockSpec((1, nb, width, blk), whole),
                pl.BlockSpec((1, nb, blk, width), whole),
                pl.BlockSpec((1, nb, width, blk), whole),
            ],
            out_specs=pl.BlockSpec((1, nb, width, blk), whole),
            scratch_shapes=[
                pltpu.VMEM((hn, width), BF16),
                pltpu.VMEM((hn, width), BF16),
                pltpu.VMEM((hn, width), BF16),
                pltpu.VMEM((nb, ATT_HEADS, nb + SUBLANES, blk), F32),
                pltpu.VMEM((nb, ATT_HEADS, 1, blk), F32),
                pltpu.VMEM((nb, ATT_HEADS, ATT_HEAD_DIM + 2 * SUBLANES, blk), F32),
                *([tile_f32] * _SKEW + [part_f32] * _SKEW + [tile_bf16] * _SKEW + [row_f32] * _SKEW),
            ],
        ),
        compiler_params=pltpu.CompilerParams(
            dimension_semantics=("arbitrary",), vmem_limit_bytes=VMEM_LIMIT),
        name="moba_attention",
    )(own_tab, past_tab, qt4, k4, vt4)


def _mlstm_kernel(qc_ref, kc_ref, mv_ref, og_ref, gate_ref, bias_ref, gain_ref, triu_ref, select_ref,
                  o_ref, ct_ref, n_ref, m_ref):
    c = pl.program_id(1)
    L = MLSTM_CHUNK
    hd = MLSTM_HEAD_DIM
    nh = MLSTM_HEADS

    @pl.when(c == 0)
    def _():
        ct_ref[...] = jnp.zeros_like(ct_ref)
        n_ref[...] = jnp.zeros_like(n_ref)
        m_ref[...] = jnp.zeros_like(m_ref)

    gr = gate_ref[...] + bias_ref[...]
    r_hi, r_mid, r_lo = _split3(_log_sigmoid(gr) * LOG2_E)
    triu = triu_ref[...]
    bcum = _dot(r_hi, triu) + _dot(r_mid, triu) + _dot(r_lo, triu)
    row = lax.broadcasted_iota(jnp.int32, gr.shape, 0)
    rows_ib = jnp.where(row < nh, gr * LOG2_E, bcum)
    rows_c = rows_ib[0:nh, :] - rows_ib[nh:2 * nh, :]
    lhs_t = jnp.concatenate(list(_split3(rows_ib)) + [jnp.ones((GATE_ROWS, L), BF16)], axis=0)

    t_iota = lax.broadcasted_iota(jnp.int32, (L, LANES), 0)
    s_iota = lax.broadcasted_iota(jnp.int32, (L, LANES), 1)
    zeros_tail = jnp.zeros((GATE_ROWS, 2 * LANES), BF16)

    gain = gain_ref[...]
    heads = range(MLSTM_HEADS)
    slices = [slice(h * hd, (h + 1) * hd) for h in heads]
    srow = lax.broadcasted_iota(jnp.int32, (GATE_ROWS, L), 0)

    res_all, qk_all, inter_all = [], [], []
    for h in heads:
        c = rows_c[h:h + 1, :]
        c_hi = c.astype(BF16).astype(F32)
        c_mid = (c - c_hi).astype(BF16).astype(F32)
        c_lo = c - c_hi - c_mid
        tail = jnp.where(srow == 0, c_hi, jnp.where(srow == 1, c_mid, jnp.where(srow == 2, c_lo, 0.0)))
        sel_h = jnp.concatenate(
            [select_ref[h], jnp.concatenate([tail.astype(BF16), zeros_tail], axis=1)], axis=0)
        res_all.append(lax.dot_general(lhs_t, sel_h, (((0,), (0,)), ((), ())), preferred_element_type=F32))
        qb = qc_ref[0, :, slices[h]]
        kb = kc_ref[0, :, slices[h]]
        qk_all.append(lax.dot_general(qb, kb, (((1,), (1,)), ((), ())), preferred_element_type=F32))
        inter_all.append(_dot(qb, ct_ref[h].astype(BF16)))

    s_all, wv_all, carry_all = [], [], []
    for h in heads:
        sl = slices[h]
        res = res_all[h]
        qk = qk_all[h]
        qb = qc_ref[0, :, sl]
        kb = kc_ref[0, :, sl]
        v = mv_ref[0, :, sl]
        m_prev = m_ref[h]
        n_prev = n_ref[h]
        b_col = res[:, L:L + LANES]
        i_col = res[:, L + LANES:L + 2 * LANES]

        dslabs = []
        for j in range(L // LANES):
            keep = (s_iota + j * LANES) <= t_iota
            dslabs.append(jnp.where(keep, res[:, j * LANES:(j + 1) * LANES], -jnp.inf))
        dmax = jnp.max(dslabs[0], axis=-1, keepdims=True)
        for d in dslabs[1:]:
            dmax = jnp.maximum(dmax, jnp.max(d, axis=-1, keepdims=True))
        inter = b_col + m_prev
        m_t = jnp.maximum(inter, dmax)
        w_inter = jnp.exp2(inter - m_t)
        s = jnp.concatenate([qk[:, j * LANES:(j + 1) * LANES] * jnp.exp2(d - m_t)
                             for j, d in enumerate(dslabs)], axis=1)
        nq = (w_inter * jnp.sum(qb.astype(F32) * n_prev, axis=-1, keepdims=True)
              + jnp.sum(s, axis=-1, keepdims=True))
        denom = jnp.maximum(jnp.abs(nq), jnp.exp2(-m_t))

        b_last = b_col[L - 1:L, :]
        g = b_last - b_col + i_col
        m_new = jnp.maximum(b_last + m_prev, jnp.max(g, axis=0, keepdims=True))
        decay = jnp.exp2(b_last + m_prev - m_new)
        w = jnp.exp2(g - m_new)
        n_ref[h] = decay * n_prev + jnp.sum(w * kb.astype(F32), axis=0, keepdims=True)
        m_ref[h] = m_new
        s_all.append(s.astype(BF16))
        wv_all.append((w * v.astype(F32)).astype(BF16))
        carry_all.append((w_inter, denom, decay))

    intra_all = [_dot(s_all[h], mv_ref[0, :, slices[h]]) for h in heads]
    for h in heads:
        ct_ref[h] = carry_all[h][2] * ct_ref[h] + lax.dot_general(
            kc_ref[0, :, slices[h]], wv_all[h], (((0,), (0,)), ((), ())), preferred_element_type=F32)

    for h in heads:
        sl = slices[h]
        w_inter, denom, _ = carry_all[h]
        hh = (w_inter * inter_all[h] + intra_all[h]) / denom
        mu = jnp.mean(hh, axis=-1, keepdims=True)
        var = jnp.mean(jnp.square(hh - mu), axis=-1, keepdims=True)
        hn = (hh - mu) * lax.rsqrt(var + NORM_EPS) * gain[:, sl]
        o_ref[0, :, sl] = (og_ref[0, :, sl].astype(F32) * hn).astype(o_ref.dtype)


def _mlstm(qc, kc, mv, og, gates_t, gate_bias, out_g):
    bsz, seq, width = qc.shape
    L = MLSTM_CHUNK
    nc = seq // L
    nh = MLSTM_HEADS
    triu = jnp.triu(jnp.ones((L, L), F32)).astype(BF16)
    r_in = jnp.arange(3 * GATE_ROWS)[None, :, None] % GATE_ROWS
    col = jnp.arange(L + 2 * LANES)[None, None, :]
    head = jnp.arange(nh)[:, None, None]
    select = jnp.where(col < L + LANES, r_in == nh + head, r_in == head).astype(BF16)
    tok = lambda b, c: (b, c, 0)
    fixed = lambda b, c: (0, 0)
    wide = pl.BlockSpec((1, L, width), tok)
    return pl.pallas_call(
        _mlstm_kernel,
        out_shape=jax.ShapeDtypeStruct((bsz, seq, width), BF16),
        grid=(bsz, nc),
        in_specs=[
            wide, wide, wide, wide,
            pl.BlockSpec((GATE_ROWS, L), lambda b, c: (0, b * nc + c)),
            pl.BlockSpec((GATE_ROWS, 1), fixed),
            pl.BlockSpec(out_g.shape, fixed),
            pl.BlockSpec((L, L), fixed),
            pl.BlockSpec(select.shape, lambda b, c: (0, 0, 0)),
        ],
        out_specs=wide,
        scratch_shapes=[
            pltpu.VMEM((nh, MLSTM_HEAD_DIM, MLSTM_HEAD_DIM), F32),
            pltpu.VMEM((nh, 1, MLSTM_HEAD_DIM), F32),
            pltpu.VMEM((nh, 1, LANES), F32),
        ],
        compiler_params=pltpu.CompilerParams(
            dimension_semantics=("arbitrary", "arbitrary"), vmem_limit_bytes=VMEM_LIMIT),
        name="mlstm",
    )(qc, kc, mv, og, gates_t, gate_bias, out_g, triu, select)


def _out_kernel(x_ref, o_ref, sz_ref, ym_ref, ag_ref, wa_ref, wm_ref, pg_ref, out_ref):
    o = o_ref[...].astype(F32)
    ms = jnp.mean(o * o, axis=-1, keepdims=True)
    ya = (o * lax.rsqrt(ms + NORM_EPS)) * ag_ref[...] * sz_ref[...].astype(F32)
    y = _dot(ya.astype(BF16), wa_ref[...]) + _dot(ym_ref[...], wm_ref[...])
    ms2 = jnp.mean(y * y, axis=-1, keepdims=True)
    out_ref[...] = x_ref[...] + (y * lax.rsqrt(ms2 + NORM_EPS)) * pg_ref[...]


def _output_projection(x2, o2, az, ym, attn_g, w_a, w_m, post_g):
    n_tok, d_model = x2.shape
    tm = PROJ_ROWS
    row = lambda i: (i, 0)
    fixed = lambda i: (0, 0)
    return pl.pallas_call(
        _out_kernel,
        out_shape=jax.ShapeDtypeStruct((n_tok, d_model), F32),
        grid=(n_tok // tm,),
        in_specs=[
            pl.BlockSpec((tm, d_model), row),
            pl.BlockSpec((tm, ATT_WIDTH), row),
            pl.BlockSpec((tm, ATT_WIDTH), row),
            pl.BlockSpec((tm, MLSTM_WIDTH), row),
            pl.BlockSpec((1, ATT_WIDTH), fixed),
            pl.BlockSpec(w_a.shape, fixed),
            pl.BlockSpec(w_m.shape, fixed),
            pl.BlockSpec((1, d_model), fixed),
        ],
        out_specs=pl.BlockSpec((tm, d_model), row),
        compiler_params=pltpu.CompilerParams(
            dimension_semantics=("arbitrary",), vmem_limit_bytes=VMEM_LIMIT),
        name="output_projection",
    )(x2, o2, az, ym, attn_g, w_a, w_m, post_g)


def _layer(x, pre_g, w_in, i_bias, f_bias, conv_w, conv_b, attn_g, mlstm_g, w_out, post_g):
    bsz, seq, d_model = x.shape
    assert seq % MOBA_BLOCK == 0 and seq % PROJ_ROWS == 0 and seq % MLSTM_CHUNK == 0
    nb = seq // MOBA_BLOCK
    n_tok = bsz * seq
    n_main = 4 * ATT_WIDTH + 5 * MLSTM_WIDTH
    n_gate = 2 * MLSTM_HEADS
    assert w_in.shape == (d_model, n_main + n_gate)

    w_main = w_in[:, :n_main].astype(BF16)
    w_gate_t = jnp.pad(w_in[:, n_main:].T, ((0, GATE_ROWS - n_gate), (0, 0))).astype(BF16)
    x2 = x.reshape(n_tok, d_model)
    (q, k, v, sz, qc, kc, mv, og, gates_t) = _input_projection(
        x2, pre_g[None, :], w_main, w_gate_t, _rope_tables(seq), conv_w, conv_b[None, :], seq)

    to_blocks_t = lambda t: t.reshape(bsz, nb, MOBA_BLOCK, ATT_WIDTH).transpose(0, 1, 3, 2)
    ot4 = _moba_attention(to_blocks_t(q), k.reshape(bsz, nb, MOBA_BLOCK, ATT_WIDTH), to_blocks_t(v))
    o2 = ot4.transpose(0, 1, 3, 2).reshape(n_tok, ATT_WIDTH)

    gate_bias = jnp.pad(jnp.concatenate([i_bias, f_bias]), (0, GATE_ROWS - n_gate))[:, None]
    shp = (bsz, seq, MLSTM_WIDTH)
    ym = _mlstm(qc.reshape(shp), kc.reshape(shp), mv.reshape(shp), og.reshape(shp),
                gates_t, gate_bias, mlstm_g[None, :])

    w_out_b = w_out.astype(BF16)
    out = _output_projection(x2, o2, sz, ym.reshape(n_tok, MLSTM_WIDTH), attn_g[None, :],
                             w_out_b[:ATT_WIDTH], w_out_b[ATT_WIDTH:], post_g[None, :])
    return out.reshape(bsz, seq, d_model)


def kernel(x, pre_norm_g, w_in, mlstm_i_bias, mlstm_f_bias, conv_w, conv_b, attn_out_g, mlstm_out_g,
           w_out, post_norm_g):
    for l in range(pre_norm_g.shape[0]):
        x = _layer(x, pre_norm_g[l], w_in[l], mlstm_i_bias[l], mlstm_f_bias[l], conv_w[l], conv_b[l],
                   attn_out_g[l], mlstm_out_g[l], w_out[l], post_norm_g[l])
    return x
```

```python
import functools

import jax
import jax.numpy as jnp
from jax import lax
from jax.experimental import pallas as pl
from jax.experimental.pallas import tpu as pltpu

F32 = jnp.float32
BF16 = jnp.bfloat16

ATT_HEADS = 8
ATT_HEAD_DIM = 64
ATT_WIDTH = ATT_HEADS * ATT_HEAD_DIM
ROT_DIM = ATT_HEAD_DIM // 4
ROPE_THETA = 500000.0
MOBA_BLOCK = 256
MOBA_TOPK = 3
MLSTM_HEADS = 4
MLSTM_HEAD_DIM = 128
MLSTM_WIDTH = MLSTM_HEADS * MLSTM_HEAD_DIM
CONV_WIDTH = 4
NORM_EPS = 1e-6

LANES = 128
SUBLANES = 8
GATE_ROWS = 16
PROJ_ROWS = 512
PROJ_EPILOGUE_ROWS = 32
MLSTM_CHUNK = 256
NEG_BIG = -1e30
LOG2_E = 1.4426950408889634
_SKEW = 2
VMEM_LIMIT = 56 * 1024 * 1024


def _sigmoid(x):
    return 1.0 / (1.0 + jnp.exp(-x))


def _silu(x):
    return x * _sigmoid(x)


def _log_sigmoid(x):
    return jnp.minimum(x, 0.0) - jnp.log(1.0 + jnp.exp(-jnp.abs(x)))


def _split3(x):
    hi = x.astype(BF16)
    r1 = x - hi.astype(F32)
    mid = r1.astype(BF16)
    lo = (r1 - mid.astype(F32)).astype(BF16)
    return hi, mid, lo


def _dot(a, b):
    return jnp.dot(a, b, preferred_element_type=F32)


def _proj_kernel(x_ref, g_ref, w_ref, wgt_ref, cos_ref, sa_ref, sb_ref, cw_ref, cb_ref,
                 q_ref, k_ref, v_ref, sz_ref, qc_ref, kc_ref, mv_ref, og_ref, gate_ref,
                 pq_ref, pk_ref, acc0_ref, acc1_ref, gt0_ref, gt1_ref, *, seq_tiles, n_tiles):
    s = pl.program_id(0)
    first_of_seq = ((s - 1) % seq_tiles) == 0
    tm = x_ref.shape[0]
    halo = SUBLANES
    n_groups = w_ref.shape[1] // ATT_WIDTH

    def matmul_parts(acc_ref, gt_ref):
        x = x_ref[...]
        ms = jnp.mean(x * x, axis=-1, keepdims=True)
        hb = ((x * lax.rsqrt(ms + NORM_EPS)) * g_ref[...]).astype(BF16)

        def group(gi):
            acc_ref[gi] = _dot(hb, w_ref[:, gi * ATT_WIDTH:(gi + 1) * ATT_WIDTH])

        def gates():
            gt_ref[...] = lax.dot_general(wgt_ref[...], hb, (((1,), (1,)), ((), ())),
                                          preferred_element_type=F32)

        return [functools.partial(group, gi) for gi in range(n_groups)] + [gates]

    def rope(acc, rs):
        cos = cos_ref[rs, :]
        sa = sa_ref[rs, :]
        sb = sb_ref[rs, :]
        parts = []
        for c in range(ATT_WIDTH // LANES):
            xs = acc[:, c * LANES:(c + 1) * LANES]
            parts.append(xs * cos
                         + pltpu.roll(xs, LANES - ROT_DIM // 2, 1) * sa
                         + pltpu.roll(xs, ROT_DIM // 2, 1) * sb)
        return jnp.concatenate(parts, axis=1)

    def conv_silu(acc_ref, gi, r, prev_ref, w, b):
        rows = PROJ_EPILOGUE_ROWS
        if r == 0:
            prev = jnp.where(first_of_seq, 0.0, prev_ref[...])
        else:
            prev = acc_ref[gi, r - halo:r, :]
        ext = jnp.concatenate([prev, acc_ref[gi, r:r + rows, :]], axis=0)
        out = b
        for j in range(CONV_WIDTH):
            off = halo - (CONV_WIDTH - 1) + j
            out = out + w[j:j + 1, :] * ext[off:off + rows, :]
        return _silu(out)

    def epilogue_parts(acc_ref, gt_ref):
        cw = cw_ref[...]
        cb = cb_ref[...]

        def chunk(r):
            rs = slice(r, r + PROJ_EPILOGUE_ROWS)
            q_ref[rs, :] = (rope(acc_ref[0, rs, :], rs) * (LOG2_E * ATT_HEAD_DIM ** -0.5)).astype(q_ref.dtype)
            k_ref[rs, :] = rope(acc_ref[1, rs, :], rs).astype(k_ref.dtype)
            v_ref[rs, :] = acc_ref[2, rs, :].astype(v_ref.dtype)
            sz_ref[rs, :] = _silu(acc_ref[3, rs, :]).astype(sz_ref.dtype)
            qc = conv_silu(acc_ref, 4, r, pq_ref, cw[:, :MLSTM_WIDTH], cb[:, :MLSTM_WIDTH])
            qc_ref[rs, :] = qc.astype(qc_ref.dtype)
            kc = conv_silu(acc_ref, 5, r, pk_ref, cw[:, MLSTM_WIDTH:], cb[:, MLSTM_WIDTH:])
            kc_ref[rs, :] = (kc * (MLSTM_HEAD_DIM ** -0.5)).astype(kc_ref.dtype)
            mv_ref[rs, :] = acc_ref[6, rs, :].astype(mv_ref.dtype)
            og_ref[rs, :] = (_sigmoid(acc_ref[7, rs, :]) * _silu(acc_ref[8, rs, :])).astype(og_ref.dtype)
        def tail():
            pq_ref[...] = acc_ref[4, tm - halo:, :]
            pk_ref[...] = acc_ref[5, tm - halo:, :]
            gate_ref[...] = gt_ref[...]

        return [functools.partial(chunk, r) for r in range(0, tm, PROJ_EPILOGUE_ROWS)] + [tail]

    def step(mat_bufs, epi_bufs):
        mats = matmul_parts(*mat_bufs) if mat_bufs is not None else []
        epis = epilogue_parts(*epi_bufs) if epi_bufs is not None else []
        per_mat = -(-len(epis) // max(len(mats), 1))
        for part in mats:
            part()
            for e in epis[:per_mat]:
                e()
            epis = epis[per_mat:]
        for e in epis:
            e()

    bufs = ((acc0_ref, gt0_ref), (acc1_ref, gt1_ref))

    @pl.when(s == 0)
    def _():
        step(bufs[0], None)

    for parity in range(2):
        @pl.when((s > 0) & (s < n_tiles) & (s % 2 == parity))
        def _():
            step(bufs[parity], bufs[1 - parity])

    @pl.when(s == n_tiles)
    def _():
        step(None, bufs[(n_tiles - 1) % 2])


def _rope_tables(seq):
    half = ROT_DIM // 2
    inv_freq = jnp.power(ROPE_THETA, -jnp.arange(half, dtype=F32) * 2.0 / ROT_DIM)
    ang = jnp.arange(seq, dtype=jnp.int32).astype(F32)[:, None] * inv_freq[None, :]
    cos = jnp.cos(ang)
    sin = jnp.sin(ang)
    ones = jnp.ones((seq, ATT_HEAD_DIM - ROT_DIM), F32)
    zeros = jnp.zeros((seq, ATT_HEAD_DIM - ROT_DIM), F32)
    zh = jnp.zeros((seq, half), F32)
    cos_h = jnp.concatenate([cos, cos, ones], axis=1)
    sa_h = jnp.concatenate([-sin, zh, zeros], axis=1)
    sb_h = jnp.concatenate([zh, sin, zeros], axis=1)
    rep = LANES // ATT_HEAD_DIM
    return (jnp.tile(cos_h, (1, rep)), jnp.tile(sa_h, (1, rep)), jnp.tile(sb_h, (1, rep)))


def _input_projection(x2, pre_g, w_main, w_gate_t, tables, conv_w, conv_b, seq):
    n_tok, d_model = x2.shape
    tm = PROJ_ROWS
    seq_tiles = seq // tm
    n_tiles = n_tok // tm
    n_groups = w_main.shape[1] // ATT_WIDTH
    row_in = lambda s: (jnp.minimum(s, n_tiles - 1), 0)
    row_out = lambda s: (jnp.maximum(s - 1, 0), 0)
    fixed = lambda s: (0, 0)
    tab = lambda s: (jnp.maximum(s - 1, 0) % seq_tiles, 0)
    wide = jax.ShapeDtypeStruct((n_tok, ATT_WIDTH), BF16)
    n_wide = 8
    return pl.pallas_call(
        functools.partial(_proj_kernel, seq_tiles=seq_tiles, n_tiles=n_tiles),
        out_shape=(wide,) * n_wide + (jax.ShapeDtypeStruct((GATE_ROWS, n_tok), F32),),
        grid=(n_tiles + 1,),
        in_specs=[
            pl.BlockSpec((tm, d_model), row_in),
            pl.BlockSpec((1, d_model), fixed),
            pl.BlockSpec(w_main.shape, fixed),
            pl.BlockSpec(w_gate_t.shape, fixed),
            pl.BlockSpec((tm, LANES), tab),
            pl.BlockSpec((tm, LANES), tab),
            pl.BlockSpec((tm, LANES), tab),
            pl.BlockSpec(conv_w.shape, fixed),
            pl.BlockSpec(conv_b.shape, fixed),
        ],
        out_specs=tuple(pl.BlockSpec((tm, ATT_WIDTH), row_out) for _ in range(n_wide))
        + (pl.BlockSpec((GATE_ROWS, tm), lambda s: (0, jnp.maximum(s - 1, 0))),),
        scratch_shapes=[
            pltpu.VMEM((SUBLANES, MLSTM_WIDTH), F32),
            pltpu.VMEM((SUBLANES, MLSTM_WIDTH), F32),
            pltpu.VMEM((n_groups, tm, ATT_WIDTH), F32),
            pltpu.VMEM((n_groups, tm, ATT_WIDTH), F32),
            pltpu.VMEM((GATE_ROWS, tm), F32),
            pltpu.VMEM((GATE_ROWS, tm), F32),
        ],
        compiler_params=pltpu.CompilerParams(
            dimension_semantics=("arbitrary",), vmem_limit_bytes=VMEM_LIMIT),
        name="input_projection",
    )(x2, pre_g, w_main, w_gate_t, *tables, conv_w, conv_b)


def _moba_kernel(own_tab, past_tab, qt_ref, k_ref, vt_ref, o_ref,
                 kst_hi, kst_mid, kst_lo, bias_ref, m_ref, acc_ref,
                 *stage_bufs, nb, own_ticks, past_ticks):
    blk = MOBA_BLOCK
    hd = ATT_HEAD_DIM

    lane_head = lax.broadcasted_iota(jnp.int32, (nb, ATT_WIDTH), 1) // hd
    means = []
    for n in range(nb):
        means.append(jnp.mean(k_ref[0, n].astype(F32), axis=0, keepdims=True))
    km = jnp.concatenate(means, axis=0)
    for h in range(ATT_HEADS):
        hi, mid, lo = _split3(jnp.where(lane_head == h, km, 0.0))
        kst_hi[h * nb:(h + 1) * nb, :] = hi
        kst_mid[h * nb:(h + 1) * nb, :] = mid
        kst_lo[h * nb:(h + 1) * nb, :] = lo

    n_iota = lax.broadcasted_iota(jnp.int32, (nb, blk), 0)
    n_iota_f = n_iota.astype(F32)
    tail_iota = lax.broadcasted_iota(jnp.int32, (SUBLANES, blk), 0)
    tail_rows = jnp.where(tail_iota == 0, 0.0, NEG_BIG)

    def prepare(i, carry):
        qt = qt_ref[0, i]
        gates = _dot(kst_hi[...], qt) + _dot(kst_mid[...], qt) + _dot(kst_lo[...], qt)
        past = n_iota < i
        for h in range(ATT_HEADS):
            g = jnp.where(past, gates[h * nb:(h + 1) * nb, :], -jnp.inf)
            sel = jnp.zeros((nb, blk), jnp.bool_)
            for _ in range(min(MOBA_TOPK, nb)):
                gmax = jnp.max(g, axis=0, keepdims=True)
                first = jnp.min(jnp.where(g == gmax, n_iota_f, float(nb)), axis=0, keepdims=True)
                hit = n_iota_f == first
                sel = sel | hit
                g = jnp.where(hit, -jnp.inf, g)
            bias_ref[i, h, 0:nb, :] = jnp.where(sel & past, 0.0, NEG_BIG)
            bias_ref[i, h, nb:nb + SUBLANES, :] = tail_rows
            m_ref[i, h] = jnp.full((1, blk), NEG_BIG, F32)
            acc_ref[i, h] = jnp.zeros(acc_ref.shape[2:], F32)
        return carry

    lax.fori_loop(0, nb, prepare, 0)

    l_iota = lax.broadcasted_iota(jnp.int32, (blk, blk), 0)
    q_iota = lax.broadcasted_iota(jnp.int32, (blk, blk), 1)
    causal = l_iota <= q_iota
    row_head = lax.broadcasted_iota(jnp.int32, (LANES, blk), 0) // hd
    ones_rows = jnp.ones((2 * SUBLANES, blk), BF16)
    s_bufs, c_bufs, p_bufs, a_bufs = (stage_bufs[j * _SKEW:(j + 1) * _SKEW] for j in range(4))

    def run(tab, n_ticks, own):
        def scores(tick, slot):
            qi = tab[0, tick]
            kb = k_ref[0, tab[1, tick]]
            for h in range(ATT_HEADS):
                pr = slice((h // 2) * LANES, (h // 2 + 1) * LANES)
                qpair = qt_ref[0, qi, pr, :]
                qh = jnp.where(row_head == (h % 2), qpair, jnp.zeros_like(qpair))
                st = _dot(kb[:, pr], qh)
                st = jnp.where(causal, st, NEG_BIG) if own else st
                s_bufs[slot][h] = st
                c_bufs[slot][h] = jnp.max(st.reshape(blk // SUBLANES, SUBLANES, blk), axis=0)

        def softmax_update(tick, slot):
            qi = tab[0, tick]
            mask_row = tab[2, tick]
            for h in range(ATT_HEADS):
                st = s_bufs[slot][h]
                b = bias_ref[qi, h, pl.ds(mask_row, 1), :]
                m_old = m_ref[qi, h]
                m_new = jnp.maximum(m_old, jnp.max(c_bufs[slot][h], axis=0, keepdims=True) + b)
                p_bufs[slot][h] = jnp.exp2(st - (m_new - b)).astype(BF16)
                a_bufs[slot][h] = jnp.exp2(m_old - m_new)
                m_ref[qi, h] = m_new

        def values(tick, slot):
            qi = tab[0, tick]
            vtb = vt_ref[0, tab[1, tick]]
            for h in range(ATT_HEADS):
                lhs = jnp.concatenate([vtb[h * hd:(h + 1) * hd, :], ones_rows], axis=0)
                acc_ref[qi, h] = a_bufs[slot][h] * acc_ref[qi, h] + _dot(lhs, p_bufs[slot][h])

        for t in range(_SKEW):
            scores(t, t % _SKEW)
        for t in range(_SKEW, 2 * _SKEW):
            softmax_update(t - _SKEW, t % _SKEW)
            scores(t, t % _SKEW)

        def body(u, carry):
            t = _SKEW * (u + 2)
            for d in range(_SKEW):
                values(t + d - 2 * _SKEW, d)
                softmax_update(t + d - _SKEW, d)
                scores(t + d, d)
            return carry

        lax.fori_loop(0, -(-n_ticks // _SKEW), body, 0)

    run(own_tab, own_ticks, True)
    run(past_tab, past_ticks, False)

    def finalize(i, carry):
        for h in range(ATT_HEADS):
            acc = acc_ref[i, h]
            o_ref[0, i, h * hd:(h + 1) * hd, :] = (acc[:hd, :] / acc[hd:hd + 1, :]).astype(o_ref.dtype)
        return carry

    lax.fori_loop(0, nb, finalize, 0)


def _tick_table(ticks, n_ticks, pad):
    length = _SKEW * (-(-n_ticks // _SKEW) + 2)
    rows = list(ticks) + [pad] * (length - len(ticks))
    return jnp.asarray(rows, jnp.int32).T


def _moba_attention(qt4, k4, vt4):
    bsz, nb, width, blk = qt4.shape
    hn = ATT_HEADS * nb
    own = [(i, i, nb) for i in range(nb)]
    past = [(i, n, n) for i in range(nb) for n in range(i)]
    own_tab = _tick_table(own, len(own), (nb - 1, nb - 1, nb + 1))
    past_tab = _tick_table(past, len(past), (nb - 1, 0, nb + 1))
    tile_f32 = pltpu.VMEM((ATT_HEADS, blk, blk), F32)
    tile_bf16 = pltpu.VMEM((ATT_HEADS, blk, blk), BF16)
    row_f32 = pltpu.VMEM((ATT_HEADS, 1, blk), F32)
    part_f32 = pltpu.VMEM((ATT_HEADS, SUBLANES, blk), F32)
    whole = lambda b, *_: (b, 0, 0, 0)
    return pl.pallas_call(
        functools.partial(_moba_kernel, nb=nb, own_ticks=len(own), past_ticks=len(past)),
        out_shape=jax.ShapeDtypeStruct((bsz, nb, width, blk), BF16),
        grid_spec=pltpu.PrefetchScalarGridSpec(
            num_scalar_prefetch=2,
            grid=(bsz,),
            in_specs=[
                pl.BlockSpec((1, nb, width, blk), whole),
                pl.BlockSpec((1, nb, blk, width), whole),
                pl.BlockSpec((1, nb, width, blk), whole),
            ],
            out_specs=pl.BlockSpec((1, nb, width, blk), whole),
            scratch_shapes=[
                pltpu.VMEM((hn, width), BF16),
                pltpu.VMEM((hn, width), BF16),
                pltpu.VMEM((hn, width), BF16),
                pltpu.VMEM((nb, ATT_HEADS, nb + SUBLANES, blk), F32),
                pltpu.VMEM((nb, ATT_HEADS, 1, blk), F32),
                pltpu.VMEM((nb, ATT_HEADS, ATT_HEAD_DIM + 2 * SUBLANES, blk), F32),
                *([tile_f32] * _SKEW + [part_f32] * _SKEW + [tile_bf16] * _SKEW + [row_f32] * _SKEW),
            ],
        ),
        compiler_params=pltpu.CompilerParams(
            dimension_semantics=("arbitrary",), vmem_limit_bytes=VMEM_LIMIT),
        name="moba_attention",
    )(own_tab, past_tab, qt4, k4, vt4)


def _mlstm_kernel(qc_ref, kc_ref, mv_ref, og_ref, gate_ref, bias_ref, gain_ref, triu_ref, select_ref,
                  o_ref, ct_ref, n_ref, m_ref):
    c = pl.program_id(1)
    L = MLSTM_CHUNK
    hd = MLSTM_HEAD_DIM
    nh = MLSTM_HEADS

    @pl.when(c == 0)
    def _():
        ct_ref[...] = jnp.zeros_like(ct_ref)
        n_ref[...] = jnp.zeros_like(n_ref)
        m_ref[...] = jnp.zeros_like(m_ref)

    gr = gate_ref[...] + bias_ref[...]
    r_hi, r_mid, r_lo = _split3(_log_sigmoid(gr) * LOG2_E)
    triu = triu_ref[...]
    bcum = _dot(r_hi, triu) + _dot(r_mid, triu) + _dot(r_lo, triu)
    row = lax.broadcasted_iota(jnp.int32, gr.shape, 0)
    rows_ib = jnp.where(row < nh, gr * LOG2_E, bcum)
    rows_c = rows_ib[0:nh, :] - rows_ib[nh:2 * nh, :]
    lhs_t = jnp.concatenate(list(_split3(rows_ib)) + [jnp.ones((GATE_ROWS, L), BF16)], axis=0)

    t_iota = lax.broadcasted_iota(jnp.int32, (L, LANES), 0)
    s_iota = lax.broadcasted_iota(jnp.int32, (L, LANES), 1)
    zeros_tail = jnp.zeros((GATE_ROWS, 2 * LANES), BF16)

    gain = gain_ref[...]
    heads = range(MLSTM_HEADS)
    slices = [slice(h * hd, (h + 1) * hd) for h in heads]
    srow = lax.broadcasted_iota(jnp.int32, (GATE_ROWS, L), 0)

    res_all, qk_all, inter_all = [], [], []
    for h in heads:
        c = rows_c[h:h + 1, :]
        c_hi = c.astype(BF16).astype(F32)
        c_mid = (c - c_hi).astype(BF16).astype(F32)
        c_lo = c - c_hi - c_mid
        tail = jnp.where(srow == 0, c_hi, jnp.where(srow == 1, c_mid, jnp.where(srow == 2, c_lo, 0.0)))
        sel_h = jnp.concatenate(
            [select_ref[h], jnp.concatenate([tail.astype(BF16), zeros_tail], axis=1)], axis=0)
        res_all.append(lax.dot_general(lhs_t, sel_h, (((0,), (0,)), ((), ())), preferred_element_type=F32))
        qb = qc_ref[0, :, slices[h]]
        kb = kc_ref[0, :, slices[h]]
        qk_all.append(lax.dot_general(qb, kb, (((1,), (1,)), ((), ())), preferred_element_type=F32))
        inter_all.append(_dot(qb, ct_ref[h].astype(BF16)))

    s_all, wv_all, carry_all = [], [], []
    for h in heads:
        sl = slices[h]
        res = res_all[h]
        qk = qk_all[h]
        qb = qc_ref[0, :, sl]
        kb = kc_ref[0, :, sl]
        v = mv_ref[0, :, sl]
        m_prev = m_ref[h]
        n_prev = n_ref[h]
        b_col = res[:, L:L + LANES]
        i_col = res[:, L + LANES:L + 2 * LANES]

        dslabs = []
        for j in range(L // LANES):
            keep = (s_iota + j * LANES) <= t_iota
            dslabs.append(jnp.where(keep, res[:, j * LANES:(j + 1) * LANES], -jnp.inf))
        dmax = jnp.max(dslabs[0], axis=-1, keepdims=True)
        for d in dslabs[1:]:
            dmax = jnp.maximum(dmax, jnp.max(d, axis=-1, keepdims=True))
        inter = b_col + m_prev
        m_t = jnp.maximum(inter, dmax)
        w_inter = jnp.exp2(inter - m_t)
        s = jnp.concatenate([qk[:, j * LANES:(j + 1) * LANES] * jnp.exp2(d - m_t)
                             for j, d in enumerate(dslabs)], axis=1)
        nq = (w_inter * jnp.sum(qb.astype(F32) * n_prev, axis=-1, keepdims=True)
              + jnp.sum(s, axis=-1, keepdims=True))
        denom = jnp.maximum(jnp.abs(nq), jnp.exp2(-m_t))

        b_last = b_col[L - 1:L, :]
        g = b_last - b_col + i_col
        m_new = jnp.maximum(b_last + m_prev, jnp.max(g, axis=0, keepdims=True))
        decay = jnp.exp2(b_last + m_prev - m_new)
        w = jnp.exp2(g - m_new)
        n_ref[h] = decay * n_prev + jnp.sum(w * kb.astype(F32), axis=0, keepdims=True)
        m_ref[h] = m_new
        s_all.append(s.astype(BF16))
        wv_all.append((w * v.astype(F32)).astype(BF16))
        carry_all.append((w_inter, denom, decay))

    intra_all = [_dot(s_all[h], mv_ref[0, :, slices[h]]) for h in heads]
    for h in heads:
        ct_ref[h] = carry_all[h][2] * ct_ref[h] + lax.dot_general(
            kc_ref[0, :, slices[h]], wv_all[h], (((0,), (0,)), ((), ())), preferred_element_type=F32)

    for h in heads:
        sl = slices[h]
        w_inter, denom, _ = carry_all[h]
        hh = (w_inter * inter_all[h] + intra_all[h]) / denom
        mu = jnp.mean(hh, axis=-1, keepdims=True)
        var = jnp.mean(jnp.square(hh - mu), axis=-1, keepdims=True)
        hn = (hh - mu) * lax.rsqrt(var + NORM_EPS) * gain[:, sl]
        o_ref[0, :, sl] = (og_ref[0, :, sl].astype(F32) * hn).astype(o_ref.dtype)


def _mlstm(qc, kc, mv, og, gates_t, gate_bias, out_g):
    bsz, seq, width = qc.shape
    L = MLSTM_CHUNK
    nc = seq // L
    nh = MLSTM_HEADS
    triu = jnp.triu(jnp.ones((L, L), F32)).astype(BF16)
    r_in = jnp.arange(3 * GATE_ROWS)[None, :, None] % GATE_ROWS
    col = jnp.arange(L + 2 * LANES)[None, None, :]
    head = jnp.arange(nh)[:, None, None]
    select = jnp.where(col < L + LANES, r_in == nh + head, r_in == head).astype(BF16)
    tok = lambda b, c: (b, c, 0)
    fixed = lambda b, c: (0, 0)
    wide = pl.BlockSpec((1, L, width), tok)
    return pl.pallas_call(
        _mlstm_kernel,
        out_shape=jax.ShapeDtypeStruct((bsz, seq, width), BF16),
        grid=(bsz, nc),
        in_specs=[
            wide, wide, wide, wide,
            pl.BlockSpec((GATE_ROWS, L), lambda b, c: (0, b * nc + c)),
            pl.BlockSpec((GATE_ROWS, 1), fixed),
            pl.BlockSpec(out_g.shape, fixed),
            pl.BlockSpec((L, L), fixed),
            pl.BlockSpec(select.shape, lambda b, c: (0, 0, 0)),
        ],
        out_specs=wide,
        scratch_shapes=[
            pltpu.VMEM((nh, MLSTM_HEAD_DIM, MLSTM_HEAD_DIM), F32),
            pltpu.VMEM((nh, 1, MLSTM_HEAD_DIM), F32),
            pltpu.VMEM((nh, 1, LANES), F32),
        ],
        compiler_params=pltpu.CompilerParams(
            dimension_semantics=("arbitrary", "arbitrary"), vmem_limit_bytes=VMEM_LIMIT),
        name="mlstm",
    )(qc, kc, mv, og, gates_t, gate_bias, out_g, triu, select)


def _out_kernel(x_ref, o_ref, sz_ref, ym_ref, ag_ref, wa_ref, wm_ref, pg_ref, out_ref):
    o = o_ref[...].astype(F32)
    ms = jnp.mean(o * o, axis=-1, keepdims=True)
    ya = (o * lax.rsqrt(ms + NORM_EPS)) * ag_ref[...] * sz_ref[...].astype(F32)
    y = _dot(ya.astype(BF16), wa_ref[...]) + _dot(ym_ref[...], wm_ref[...])
    ms2 = jnp.mean(y * y, axis=-1, keepdims=True)
    out_ref[...] = x_ref[...] + (y * lax.rsqrt(ms2 + NORM_EPS)) * pg_ref[...]


def _output_projection(x2, o2, az, ym, attn_g, w_a, w_m, post_g):
    n_tok, d_model = x2.shape
    tm = PROJ_ROWS
    row = lambda i: (i, 0)
    fixed = lambda i: (0, 0)
    return pl.pallas_call(
        _out_kernel,
        out_shape=jax.ShapeDtypeStruct((n_tok, d_model), F32),
        grid=(n_tok // tm,),
        in_specs=[
            pl.BlockSpec((tm, d_model), row),
            pl.BlockSpec((tm, ATT_WIDTH), row),
            pl.BlockSpec((tm, ATT_WIDTH), row),
            pl.BlockSpec((tm, MLSTM_WIDTH), row),
            pl.BlockSpec((1, ATT_WIDTH), fixed),
            pl.BlockSpec(w_a.shape, fixed),
            pl.BlockSpec(w_m.shape, fixed),
            pl.BlockSpec((1, d_model), fixed),
        ],
        out_specs=pl.BlockSpec((tm, d_model), row),
        compiler_params=pltpu.CompilerParams(
            dimension_semantics=("arbitrary",), vmem_limit_bytes=VMEM_LIMIT),
        name="output_projection",
    )(x2, o2, az, ym, attn_g, w_a, w_m, post_g)


def _layer(x, pre_g, w_in, i_bias, f_bias, conv_w, conv_b, attn_g, mlstm_g, w_out, post_g):
    bsz, seq, d_model = x.shape
    assert seq % MOBA_BLOCK == 0 and seq % PROJ_ROWS == 0 and seq % MLSTM_CHUNK == 0
    nb = seq // MOBA_BLOCK
    n_tok = bsz * seq
    n_main = 4 * ATT_WIDTH + 5 * MLSTM_WIDTH
    n_gate = 2 * MLSTM_HEADS
    assert w_in.shape == (d_model, n_main + n_gate)

    w_main = w_in[:, :n_main].astype(BF16)
    w_gate_t = jnp.pad(w_in[:, n_main:].T, ((0, GATE_ROWS - n_gate), (0, 0))).astype(BF16)
    x2 = x.reshape(n_tok, d_model)
    (q, k, v, sz, qc, kc, mv, og, gates_t) = _input_projection(
        x2, pre_g[None, :], w_main, w_gate_t, _rope_tables(seq), conv_w, conv_b[None, :], seq)

    to_blocks_t = lambda t: t.reshape(bsz, nb, MOBA_BLOCK, ATT_WIDTH).transpose(0, 1, 3, 2)
    ot4 = _moba_attention(to_blocks_t(q), k.reshape(bsz, nb, MOBA_BLOCK, ATT_WIDTH), to_blocks_t(v))
    o2 = ot4.transpose(0, 1, 3, 2).reshape(n_tok, ATT_WIDTH)

    gate_bias = jnp.pad(jnp.concatenate([i_bias, f_bias]), (0, GATE_ROWS - n_gate))[:, None]
    shp = (bsz, seq, MLSTM_WIDTH)
    ym = _mlstm(qc.reshape(shp), kc.reshape(shp), mv.reshape(shp), og.reshape(shp),
                gates_t, gate_bias, mlstm_g[None, :])

    w_out_b = w_out.astype(BF16)
    out = _output_projection(x2, o2, sz, ym.reshape(n_tok, MLSTM_WIDTH), attn_g[None, :],
                             w_out_b[:ATT_WIDTH], w_out_b[ATT_WIDTH:], post_g[None, :])
    return out.reshape(bsz, seq, d_model)


def kernel(x, pre_norm_g, w_in, mlstm_i_bias, mlstm_f_bias, conv_w, conv_b, attn_out_g, mlstm_out_g,
           w_out, post_norm_g):
    for l in range(pre_norm_g.shape[0]):
        x = _layer(x, pre_norm_g[l], w_in[l], mlstm_i_bias[l], mlstm_f_bias[l], conv_w[l], conv_b[l],
                   attn_out_g[l], mlstm_out_g[l], w_out[l], post_norm_g[l])
    return x
```

```python
import functools

import jax
import jax.numpy as jnp
from jax import lax
from jax.experimental import pallas as pl
from jax.experimental.pallas import tpu as pltpu

F32 = jnp.float32
BF16 = jnp.bfloat16

ATT_HEADS = 8
ATT_HEAD_DIM = 64
ATT_WIDTH = ATT_HEADS * ATT_HEAD_DIM
ROT_DIM = ATT_HEAD_DIM // 4
ROPE_THETA = 500000.0
MOBA_BLOCK = 256
MOBA_TOPK = 3
MLSTM_HEADS = 4
MLSTM_HEAD_DIM = 128
MLSTM_WIDTH = MLSTM_HEADS * MLSTM_HEAD_DIM
CONV_WIDTH = 4
NORM_EPS = 1e-6

LANES = 128
SUBLANES = 8
GATE_ROWS = 16
PROJ_ROWS = 512
PROJ_EPILOGUE_ROWS = 32
OUT_ROWS = 1024
MLSTM_CHUNK = 256
MLSTM_CHUNKS_PER_STEP = 2
NEG_BIG = -1e30
LOG2_E = 1.4426950408889634
_SKEW = 2
VMEM_LIMIT = 58 * 1024 * 1024


def _sigmoid(x):
    return 1.0 / (1.0 + jnp.exp(-x))


def _silu(x):
    return x * _sigmoid(x)


def _log_sigmoid(x):
    return jnp.minimum(x, 0.0) - jnp.log(1.0 + jnp.exp(-jnp.abs(x)))


def _split3(x):
    hi = x.astype(BF16)
    r1 = x - hi.astype(F32)
    mid = r1.astype(BF16)
    lo = (r1 - mid.astype(F32)).astype(BF16)
    return hi, mid, lo


def _dot(a, b):
    return jnp.dot(a, b, preferred_element_type=F32)


def _proj_kernel(x_ref, g_ref, w_ref, wgt_ref, cos_ref, sa_ref, sb_ref, cw_ref, cb_ref,
                 q_ref, k_ref, v_ref, sz_ref, qc_ref, kc_ref, mv_ref, og_ref, gate_ref,
                 pq_ref, pk_ref, acc0_ref, acc1_ref, gt0_ref, gt1_ref, *, seq_tiles, n_tiles):
    s = pl.program_id(0)
    first_of_seq = ((s - 1) % seq_tiles) == 0
    tm = x_ref.shape[0]
    halo = SUBLANES
    n_groups = w_ref.shape[1] // ATT_WIDTH

    def matmul_parts(acc_ref, gt_ref):
        x = x_ref[...]
        ms = jnp.mean(x * x, axis=-1, keepdims=True)
        hb = ((x * lax.rsqrt(ms + NORM_EPS)) * g_ref[...]).astype(BF16)

        def group(gi):
            acc_ref[gi] = _dot(hb, w_ref[:, gi * ATT_WIDTH:(gi + 1) * ATT_WIDTH])

        def gates():
            gt_ref[...] = lax.dot_general(wgt_ref[...], hb, (((1,), (1,)), ((), ())),
                                          preferred_element_type=F32)

        return [functools.partial(group, gi) for gi in range(n_groups)] + [gates]

    def rope(acc, rs):
        cos = cos_ref[rs, :]
        sa = sa_ref[rs, :]
        sb = sb_ref[rs, :]
        parts = []
        for c in range(ATT_WIDTH // LANES):
            xs = acc[:, c * LANES:(c + 1) * LANES]
            parts.append(xs * cos
                         + pltpu.roll(xs, LANES - ROT_DIM // 2, 1) * sa
                         + pltpu.roll(xs, ROT_DIM // 2, 1) * sb)
        return jnp.concatenate(parts, axis=1)

    def conv_silu(acc_ref, gi, r, prev_ref, w, b):
        rows = PROJ_EPILOGUE_ROWS
        if r == 0:
            prev = jnp.where(first_of_seq, 0.0, prev_ref[...])
        else:
            prev = acc_ref[gi, r - halo:r, :]
        ext = jnp.concatenate([prev, acc_ref[gi, r:r + rows, :]], axis=0)
        out = b
        for j in range(CONV_WIDTH):
            off = halo - (CONV_WIDTH - 1) + j
            out = out + w[j:j + 1, :] * ext[off:off + rows, :]
        return _silu(out)

    def epilogue_parts(acc_ref, gt_ref):
        cw = cw_ref[...]
        cb = cb_ref[...]

        def chunk(r):
            rs = slice(r, r + PROJ_EPILOGUE_ROWS)
            q_ref[rs, :] = (rope(acc_ref[0, rs, :], rs) * (LOG2_E * ATT_HEAD_DIM ** -0.5)).astype(q_ref.dtype)
            k_ref[rs, :] = rope(acc_ref[1, rs, :], rs).astype(k_ref.dtype)
            v_ref[rs, :] = acc_ref[2, rs, :].astype(v_ref.dtype)
            sz_ref[rs, :] = _silu(acc_ref[3, rs, :]).astype(sz_ref.dtype)
            qc = conv_silu(acc_ref, 4, r, pq_ref, cw[:, :MLSTM_WIDTH], cb[:, :MLSTM_WIDTH])
            qc_ref[rs, :] = qc.astype(qc_ref.dtype)
            kc = conv_silu(acc_ref, 5, r, pk_ref, cw[:, MLSTM_WIDTH:], cb[:, MLSTM_WIDTH:])
            kc_ref[rs, :] = (kc * (MLSTM_HEAD_DIM ** -0.5)).astype(kc_ref.dtype)
            mv_ref[rs, :] = acc_ref[6, rs, :].astype(mv_ref.dtype)
            og_ref[rs, :] = (_sigmoid(acc_ref[7, rs, :]) * _silu(acc_ref[8, rs, :])).astype(og_ref.dtype)
        def tail():
            pq_ref[...] = acc_ref[4, tm - halo:, :]
            pk_ref[...] = acc_ref[5, tm - halo:, :]
            gate_ref[...] = gt_ref[...]

        return [functools.partial(chunk, r) for r in range(0, tm, PROJ_EPILOGUE_ROWS)] + [tail]

    def step(mat_bufs, epi_bufs):
        mats = matmul_parts(*mat_bufs) if mat_bufs is not None else []
        epis = epilogue_parts(*epi_bufs) if epi_bufs is not None else []
        per_mat = -(-len(epis) // max(len(mats), 1))
        for part in mats:
            part()
            for e in epis[:per_mat]:
                e()
            epis = epis[per_mat:]
        for e in epis:
            e()

    bufs = ((acc0_ref, gt0_ref), (acc1_ref, gt1_ref))

    @pl.when(s == 0)
    def _():
        step(bufs[0], None)

    for parity in range(2):
        @pl.when((s > 0) & (s < n_tiles) & (s % 2 == parity))
        def _():
            step(bufs[parity], bufs[1 - parity])

    @pl.when(s == n_tiles)
    def _():
        step(None, bufs[(n_tiles - 1) % 2])


def _rope_tables(seq):
    half = ROT_DIM // 2
    inv_freq = jnp.power(ROPE_THETA, -jnp.arange(half, dtype=F32) * 2.0 / ROT_DIM)
    ang = jnp.arange(seq, dtype=jnp.int32).astype(F32)[:, None] * inv_freq[None, :]
    cos = jnp.cos(ang)
    sin = jnp.sin(ang)
    ones = jnp.ones((seq, ATT_HEAD_DIM - ROT_DIM), F32)
    zeros = jnp.zeros((seq, ATT_HEAD_DIM - ROT_DIM), F32)
    zh = jnp.zeros((seq, half), F32)
    cos_h = jnp.concatenate([cos, cos, ones], axis=1)
    sa_h = jnp.concatenate([-sin, zh, zeros], axis=1)
    sb_h = jnp.concatenate([zh, sin, zeros], axis=1)
    rep = LANES // ATT_HEAD_DIM
    return (jnp.tile(cos_h, (1, rep)), jnp.tile(sa_h, (1, rep)), jnp.tile(sb_h, (1, rep)))


def _input_projection(x2, pre_g, w_main, w_gate_t, tables, conv_w, conv_b, seq):
    n_tok, d_model = x2.shape
    tm = PROJ_ROWS
    seq_tiles = seq // tm
    n_tiles = n_tok // tm
    n_groups = w_main.shape[1] // ATT_WIDTH
    row_in = lambda s: (jnp.minimum(s, n_tiles - 1), 0)
    row_out = lambda s: (jnp.maximum(s - 1, 0), 0)
    fixed = lambda s: (0, 0)
    tab = lambda s: (jnp.maximum(s - 1, 0) % seq_tiles, 0)
    wide = jax.ShapeDtypeStruct((n_tok, ATT_WIDTH), BF16)
    n_wide = 8
    return pl.pallas_call(
        functools.partial(_proj_kernel, seq_tiles=seq_tiles, n_tiles=n_tiles),
        out_shape=(wide,) * n_wide + (jax.ShapeDtypeStruct((GATE_ROWS, n_tok), F32),),
        grid=(n_tiles + 1,),
        in_specs=[
            pl.BlockSpec((tm, d_model), row_in),
            pl.BlockSpec((1, d_model), fixed),
            pl.BlockSpec(w_main.shape, fixed),
            pl.BlockSpec(w_gate_t.shape, fixed),
            pl.BlockSpec((tm, LANES), tab),
            pl.BlockSpec((tm, LANES), tab),
            pl.BlockSpec((tm, LANES), tab),
            pl.BlockSpec(conv_w.shape, fixed),
            pl.BlockSpec(conv_b.shape, fixed),
        ],
        out_specs=tuple(pl.BlockSpec((tm, ATT_WIDTH), row_out) for _ in range(n_wide))
        + (pl.BlockSpec((GATE_ROWS, tm), lambda s: (0, jnp.maximum(s - 1, 0))),),
        scratch_shapes=[
            pltpu.VMEM((SUBLANES, MLSTM_WIDTH), F32),
            pltpu.VMEM((SUBLANES, MLSTM_WIDTH), F32),
            pltpu.VMEM((n_groups, tm, ATT_WIDTH), F32),
            pltpu.VMEM((n_groups, tm, ATT_WIDTH), F32),
            pltpu.VMEM((GATE_ROWS, tm), F32),
            pltpu.VMEM((GATE_ROWS, tm), F32),
        ],
        compiler_params=pltpu.CompilerParams(
            dimension_semantics=("arbitrary",), vmem_limit_bytes=VMEM_LIMIT),
        name="input_projection",
    )(x2, pre_g, w_main, w_gate_t, *tables, conv_w, conv_b)


def _moba_kernel(own_tab, past_tab, qt_ref, k_ref, vt_ref, o_ref,
                 kst_hi, kst_mid, kst_lo, qh_ref, bias_ref, m_ref, acc_ref,
                 *stage_bufs, nb, own_ticks, past_ticks):
    blk = MOBA_BLOCK
    hd = ATT_HEAD_DIM

    lane_head = lax.broadcasted_iota(jnp.int32, (nb, ATT_WIDTH), 1) // hd
    means = []
    for n in range(nb):
        means.append(jnp.mean(k_ref[0, n].astype(F32), axis=0, keepdims=True))
    km = jnp.concatenate(means, axis=0)
    for h in range(ATT_HEADS):
        hi, mid, lo = _split3(jnp.where(lane_head == h, km, 0.0))
        kst_hi[h * nb:(h + 1) * nb, :] = hi
        kst_mid[h * nb:(h + 1) * nb, :] = mid
        kst_lo[h * nb:(h + 1) * nb, :] = lo

    row_head = lax.broadcasted_iota(jnp.int32, (LANES, blk), 0) // hd
    n_iota = lax.broadcasted_iota(jnp.int32, (nb, blk), 0)
    n_iota_f = n_iota.astype(F32)
    tail_iota = lax.broadcasted_iota(jnp.int32, (SUBLANES, blk), 0)
    tail_rows = jnp.where(tail_iota == 0, 0.0, NEG_BIG)

    def prepare(i, carry):
        qt = qt_ref[0, i]
        gates = _dot(kst_hi[...], qt) + _dot(kst_mid[...], qt) + _dot(kst_lo[...], qt)
        past = n_iota < i
        for h in range(ATT_HEADS):
            g = jnp.where(past, gates[h * nb:(h + 1) * nb, :], -jnp.inf)
            sel = jnp.zeros((nb, blk), jnp.bool_)
            for _ in range(min(MOBA_TOPK, nb)):
                gmax = jnp.max(g, axis=0, keepdims=True)
                first = jnp.min(jnp.where(g == gmax, n_iota_f, float(nb)), axis=0, keepdims=True)
                hit = n_iota_f == first
                sel = sel | hit
                g = jnp.where(hit, -jnp.inf, g)
            bias_ref[i, h, 0:nb, :] = jnp.where(sel & past, 0.0, NEG_BIG)
            bias_ref[i, h, nb:nb + SUBLANES, :] = tail_rows
            m_ref[i, h] = jnp.full((1, blk), NEG_BIG, F32)
            acc_ref[i, h] = jnp.zeros(acc_ref.shape[2:], F32)
            qpair = qt[(h // 2) * LANES:(h // 2 + 1) * LANES, :]
            qh_ref[i, h] = jnp.where(row_head == (h % 2), qpair, jnp.zeros_like(qpair))
        return carry

    lax.fori_loop(0, nb, prepare, 0)

    l_iota = lax.broadcasted_iota(jnp.int32, (blk, blk), 0)
    q_iota = lax.broadcasted_iota(jnp.int32, (blk, blk), 1)
    causal = l_iota <= q_iota
    ones_rows = jnp.ones((2 * SUBLANES, blk), BF16)
    s_bufs, c_bufs, p_bufs, a_bufs = (stage_bufs[j * _SKEW:(j + 1) * _SKEW] for j in range(4))

    def run(tab, n_ticks, own):
        def scores(tick, slot):
            qi = tab[0, tick]
            kb = k_ref[0, tab[1, tick]]
            for h in range(ATT_HEADS):
                pr = slice((h // 2) * LANES, (h // 2 + 1) * LANES)
                st = _dot(kb[:, pr], qh_ref[qi, h])
                st = jnp.where(causal, st, NEG_BIG) if own else st
                s_bufs[slot][h] = st
                c_bufs[slot][h] = jnp.max(st.reshape(blk // SUBLANES, SUBLANES, blk), axis=0)

        def softmax_update(tick, slot):
            qi = tab[0, tick]
            mask_row = tab[2, tick]
            for h in range(ATT_HEADS):
                st = s_bufs[slot][h]
                b = bias_ref[qi, h, pl.ds(mask_row, 1), :]
                m_old = m_ref[qi, h]
                m_new = jnp.maximum(m_old, jnp.max(c_bufs[slot][h], axis=0, keepdims=True) + b)
                p_bufs[slot][h] = jnp.exp2(st - (m_new - b)).astype(BF16)
                a_bufs[slot][h] = jnp.exp2(m_old - m_new)
                m_ref[qi, h] = m_new

        def values(tick, slot):
            qi = tab[0, tick]
            vtb = vt_ref[0, tab[1, tick]]
            for h in range(ATT_HEADS):
                lhs = jnp.concatenate([vtb[h * hd:(h + 1) * hd, :], ones_rows], axis=0)
                acc_ref[qi, h] = a_bufs[slot][h] * acc_ref[qi, h] + _dot(lhs, p_bufs[slot][h])

        for t in range(_SKEW):
            scores(t, t % _SKEW)
        for t in range(_SKEW, 2 * _SKEW):
            softmax_update(t - _SKEW, t % _SKEW)
            scores(t, t % _SKEW)

        def body(u, carry):
            t = _SKEW * (u + 2)
            for d in range(_SKEW):
                values(t + d - 2 * _SKEW, d)
                softmax_update(t + d - _SKEW, d)
                scores(t + d, d)
            return carry

        lax.fori_loop(0, -(-n_ticks // _SKEW), body, 0)

    run(own_tab, own_ticks, True)
    run(past_tab, past_ticks, False)

    def finalize(i, carry):
        for h in range(ATT_HEADS):
            acc = acc_ref[i, h]
            o_ref[0, i, h * hd:(h + 1) * hd, :] = (acc[:hd, :] / acc[hd:hd + 1, :]).astype(o_ref.dtype)
        return carry

    lax.fori_loop(0, nb, finalize, 0)


def _tick_table(ticks, n_ticks, pad):
    length = _SKEW * (-(-n_ticks // _SKEW) + 2)
    rows = list(ticks) + [pad] * (length - len(ticks))
    return jnp.asarray(rows, jnp.int32).T


def _moba_attention(qt4, k4, vt4):
    bsz, nb, width, blk = qt4.shape
    hn = ATT_HEADS * nb
    own = [(i, i, nb) for i in range(nb)]
    past = [(i, n, n) for i in range(nb) for n in range(i)]
    own_tab = _tick_table(own, len(own), (nb - 1, nb - 1, nb + 1))
    past_tab = _tick_table(past, len(past), (nb - 1, 0, nb + 1))
    tile_f32 = pltpu.VMEM((ATT_HEADS, blk, blk), F32)
    tile_bf16 = pltpu.VMEM((ATT_HEADS, blk, blk), BF16)
    row_f32 = pltpu.VMEM((ATT_HEADS, 1, blk), F32)
    part_f32 = pltpu.VMEM((ATT_HEADS, SUBLANES, blk), F32)
    whole = lambda b, *_: (b, 0, 0, 0)
    return pl.pallas_call(
        functools.partial(_moba_kernel, nb=nb, own_ticks=len(own), past_ticks=len(past)),
        out_shape=jax.ShapeDtypeStruct((bsz, nb, width, blk), BF16),
        grid_spec=pltpu.PrefetchScalarGridSpec(
            num_scalar_prefetch=2,
            grid=(bsz,),
            in_specs=[
                pl.BlockSpec((1, nb, width, blk), whole, pipeline_mode=pl.Buffered(1)),
                pl.BlockSpec((1, nb, blk, width), whole),
                pl.BlockSpec((1, nb, width, blk), whole),
            ],
            out_specs=pl.BlockSpec((1, nb, width, blk), whole),
            scratch_shapes=[
                pltpu.VMEM((hn, width), BF16),
                pltpu.VMEM((hn, width), BF16),
                pltpu.VMEM((hn, width), BF16),
                pltpu.VMEM((nb, ATT_HEADS, LANES, blk), BF16),
                pltpu.VMEM((nb, ATT_HEADS, nb + SUBLANES, blk), F32),
                pltpu.VMEM((nb, ATT_HEADS, 1, blk), F32),
                pltpu.VMEM((nb, ATT_HEADS, ATT_HEAD_DIM + 2 * SUBLANES, blk), F32),
                *([tile_f32] * _SKEW + [part_f32] * _SKEW + [tile_bf16] * _SKEW + [row_f32] * _SKEW),
            ],
        ),
        compiler_params=pltpu.CompilerParams(
            dimension_semantics=("arbitrary",), vmem_limit_bytes=VMEM_LIMIT),
        name="moba_attention",
    )(own_tab, past_tab, qt4, k4, vt4)


def _mlstm_kernel(qc_ref, kc_ref, mv_ref, og_ref, gate_ref, bias_ref, gain_ref, triu_ref, select_ref,
                  o_ref, ct_ref, m_ref):
    @pl.when(pl.program_id(1) == 0)
    def _():
        ct_ref[...] = jnp.zeros_like(ct_ref)
        m_ref[...] = jnp.zeros_like(m_ref)

    L = MLSTM_CHUNK
    for j in range(MLSTM_CHUNKS_PER_STEP):
        ts = slice(j * L, (j + 1) * L)
        _mlstm_chunk(qc_ref.at[:, ts, :], kc_ref.at[:, ts, :], mv_ref.at[:, ts, :], og_ref.at[:, ts, :],
                     gate_ref.at[:, ts], bias_ref, gain_ref, triu_ref, select_ref,
                     o_ref.at[:, ts, :], ct_ref, m_ref)


def _mlstm_chunk(qc_ref, kc_ref, mv_ref, og_ref, gate_ref, bias_ref, gain_ref, triu_ref, select_ref,
                 o_ref, ct_ref, m_ref):
    L = MLSTM_CHUNK
    hd = MLSTM_HEAD_DIM
    nh = MLSTM_HEADS

    gr = gate_ref[...] + bias_ref[...]
    r_hi, r_mid, r_lo = _split3(_log_sigmoid(gr) * LOG2_E)
    triu = triu_ref[...]
    bcum = _dot(r_hi, triu) + _dot(r_mid, triu) + _dot(r_lo, triu)
    row = lax.broadcasted_iota(jnp.int32, gr.shape, 0)
    rows_ib = jnp.where(row < nh, gr * LOG2_E, bcum)
    rows_c = rows_ib[0:nh, :] - rows_ib[nh:2 * nh, :]
    lhs_t = jnp.concatenate(list(_split3(rows_ib)) + [jnp.ones((GATE_ROWS, L), BF16)], axis=0)

    t_iota = lax.broadcasted_iota(jnp.int32, (L, LANES), 0)
    s_iota = lax.broadcasted_iota(jnp.int32, (L, LANES), 1)
    zeros_tail = jnp.zeros((GATE_ROWS, 2 * LANES), BF16)

    gain = gain_ref[...]
    heads = range(MLSTM_HEADS)
    slices = [slice(h * hd, (h + 1) * hd) for h in heads]
    srow = lax.broadcasted_iota(jnp.int32, (GATE_ROWS, L), 0)

    res_all, qk_all, inter_all = [], [], []
    for h in heads:
        c = rows_c[h:h + 1, :]
        c_hi = c.astype(BF16).astype(F32)
        c_mid = (c - c_hi).astype(BF16).astype(F32)
        c_lo = c - c_hi - c_mid
        tail = jnp.where(srow == 0, c_hi, jnp.where(srow == 1, c_mid, jnp.where(srow == 2, c_lo, 0.0)))
        sel_h = jnp.concatenate(
            [select_ref[h], jnp.concatenate([tail.astype(BF16), zeros_tail], axis=1)], axis=0)
        res_all.append(lax.dot_general(lhs_t, sel_h, (((0,), (0,)), ((), ())), preferred_element_type=F32))
        qb = qc_ref[0, :, slices[h]]
        kb = kc_ref[0, :, slices[h]]
        qk_all.append(lax.dot_general(qb, kb, (((1,), (1,)), ((), ())), preferred_element_type=F32))
        inter_all.append(_dot(qb, ct_ref[h].astype(BF16)))

    s_all, wv_all, carry_all = [], [], []
    for h in heads:
        sl = slices[h]
        res = res_all[h]
        qk = qk_all[h]
        qb = qc_ref[0, :, sl]
        kb = kc_ref[0, :, sl]
        v = mv_ref[0, :, sl]
        m_prev = m_ref[h]
        b_col = res[:, L:L + LANES]
        i_col = res[:, L + LANES:L + 2 * LANES]

        dslabs = []
        for j in range(L // LANES):
            keep = (s_iota + j * LANES) <= t_iota
            dslabs.append(jnp.where(keep, res[:, j * LANES:(j + 1) * LANES], -jnp.inf))
        dmax = jnp.max(dslabs[0], axis=-1, keepdims=True)
        for d in dslabs[1:]:
            dmax = jnp.maximum(dmax, jnp.max(d, axis=-1, keepdims=True))
        inter = b_col + m_prev
        m_t = jnp.maximum(inter, dmax)
        w_inter = jnp.exp2(inter - m_t)
        s = jnp.concatenate([qk[:, j * LANES:(j + 1) * LANES] * jnp.exp2(d - m_t)
                             for j, d in enumerate(dslabs)], axis=1)

        b_last = b_col[L - 1:L, :]
        g = b_last - b_col + i_col
        m_new = jnp.maximum(b_last + m_prev, jnp.max(g, axis=0, keepdims=True))
        decay = jnp.exp2(b_last + m_prev - m_new)
        w = jnp.exp2(g - m_new)
        m_ref[h] = m_new
        s_all.append(s.astype(BF16))
        wv_all.append(jnp.concatenate([w * v.astype(F32), w], axis=1).astype(BF16))
        carry_all.append((w_inter, jnp.exp2(-m_t), jnp.concatenate([decay, decay], axis=1)))

    ones_cols = jnp.ones((L, hd), BF16)
    intra_all = [_dot(s_all[h], jnp.concatenate([mv_ref[0, :, slices[h]], ones_cols], axis=1))
                 for h in heads]
    for h in heads:
        ct_ref[h] = carry_all[h][2] * ct_ref[h] + lax.dot_general(
            kc_ref[0, :, slices[h]], wv_all[h], (((0,), (0,)), ((), ())), preferred_element_type=F32)

    for h in heads:
        sl = slices[h]
        w_inter, floor, _ = carry_all[h]
        num = w_inter * inter_all[h][:, :hd] + intra_all[h][:, :hd]
        nq = w_inter * inter_all[h][:, hd:] + intra_all[h][:, hd:]
        hh = num / jnp.maximum(jnp.abs(nq), floor)
        mu = jnp.mean(hh, axis=-1, keepdims=True)
        var = jnp.mean(jnp.square(hh - mu), axis=-1, keepdims=True)
        hn = (hh - mu) * lax.rsqrt(var + NORM_EPS) * gain[:, sl]
        o_ref[0, :, sl] = (og_ref[0, :, sl].astype(F32) * hn).astype(o_ref.dtype)


def _mlstm(qc, kc, mv, og, gates_t, gate_bias, out_g):
    bsz, seq, width = qc.shape
    L = MLSTM_CHUNK
    step = L * MLSTM_CHUNKS_PER_STEP
    nc = seq // step
    nh = MLSTM_HEADS
    triu = jnp.triu(jnp.ones((L, L), F32)).astype(BF16)
    r_in = jnp.arange(3 * GATE_ROWS)[None, :, None] % GATE_ROWS
    col = jnp.arange(L + 2 * LANES)[None, None, :]
    head = jnp.arange(nh)[:, None, None]
    select = jnp.where(col < L + LANES, r_in == nh + head, r_in == head).astype(BF16)
    tok = lambda b, c: (b, c, 0)
    fixed = lambda b, c: (0, 0)
    wide = pl.BlockSpec((1, step, width), tok)
    return pl.pallas_call(
        _mlstm_kernel,
        out_shape=jax.ShapeDtypeStruct((bsz, seq, width), BF16),
        grid=(bsz, nc),
        in_specs=[
            wide, wide, wide, wide,
            pl.BlockSpec((GATE_ROWS, step), lambda b, c: (0, b * nc + c)),
            pl.BlockSpec((GATE_ROWS, 1), fixed),
            pl.BlockSpec(out_g.shape, fixed),
            pl.BlockSpec((L, L), fixed),
            pl.BlockSpec(select.shape, lambda b, c: (0, 0, 0)),
        ],
        out_specs=wide,
        scratch_shapes=[
            pltpu.VMEM((nh, MLSTM_HEAD_DIM, 2 * MLSTM_HEAD_DIM), F32),
            pltpu.VMEM((nh, 1, LANES), F32),
        ],
        compiler_params=pltpu.CompilerParams(
            dimension_semantics=("arbitrary", "arbitrary"), vmem_limit_bytes=VMEM_LIMIT),
        name="mlstm",
    )(qc, kc, mv, og, gates_t, gate_bias, out_g, triu, select)


def _out_kernel(x_ref, o_ref, sz_ref, ym_ref, ag_ref, wa_ref, wm_ref, pg_ref, out_ref):
    o = o_ref[...].astype(F32)
    ms = jnp.mean(o * o, axis=-1, keepdims=True)
    ya = (o * lax.rsqrt(ms + NORM_EPS)) * ag_ref[...] * sz_ref[...].astype(F32)
    y = _dot(ya.astype(BF16), wa_ref[...]) + _dot(ym_ref[...], wm_ref[...])
    ms2 = jnp.mean(y * y, axis=-1, keepdims=True)
    out_ref[...] = x_ref[...] + (y * lax.rsqrt(ms2 + NORM_EPS)) * pg_ref[...]


def _output_projection(x2, o2, az, ym, attn_g, w_a, w_m, post_g):
    n_tok, d_model = x2.shape
    tm = OUT_ROWS
    row = lambda i: (i, 0)
    fixed = lambda i: (0, 0)
    return pl.pallas_call(
        _out_kernel,
        out_shape=jax.ShapeDtypeStruct((n_tok, d_model), F32),
        grid=(n_tok // tm,),
        in_specs=[
            pl.BlockSpec((tm, d_model), row),
            pl.BlockSpec((tm, ATT_WIDTH), row),
            pl.BlockSpec((tm, ATT_WIDTH), row),
            pl.BlockSpec((tm, MLSTM_WIDTH), row),
            pl.BlockSpec((1, ATT_WIDTH), fixed),
            pl.BlockSpec(w_a.shape, fixed),
            pl.BlockSpec(w_m.shape, fixed),
            pl.BlockSpec((1, d_model), fixed),
        ],
        out_specs=pl.BlockSpec((tm, d_model), row),
        compiler_params=pltpu.CompilerParams(
            dimension_semantics=("arbitrary",), vmem_limit_bytes=VMEM_LIMIT),
        name="output_projection",
    )(x2, o2, az, ym, attn_g, w_a, w_m, post_g)


def _layer(x, pre_g, w_in, i_bias, f_bias, conv_w, conv_b, attn_g, mlstm_g, w_out, post_g):
    bsz, seq, d_model = x.shape
    assert seq % MOBA_BLOCK == 0 and seq % PROJ_ROWS == 0
    assert seq % (MLSTM_CHUNK * MLSTM_CHUNKS_PER_STEP) == 0
    nb = seq // MOBA_BLOCK
    n_tok = bsz * seq
    n_main = 4 * ATT_WIDTH + 5 * MLSTM_WIDTH
    n_gate = 2 * MLSTM_HEADS
    assert w_in.shape == (d_model, n_main + n_gate)

    w_main = w_in[:, :n_main].astype(BF16)
    w_gate_t = jnp.pad(w_in[:, n_main:].T, ((0, GATE_ROWS - n_gate), (0, 0))).astype(BF16)
    x2 = x.reshape(n_tok, d_model)
    (q, k, v, sz, qc, kc, mv, og, gates_t) = _input_projection(
        x2, pre_g[None, :], w_main, w_gate_t, _rope_tables(seq), conv_w, conv_b[None, :], seq)

    to_blocks_t = lambda t: t.reshape(bsz, nb, MOBA_BLOCK, ATT_WIDTH).transpose(0, 1, 3, 2)
    ot4 = _moba_attention(to_blocks_t(q), k.reshape(bsz, nb, MOBA_BLOCK, ATT_WIDTH), to_blocks_t(v))
    o2 = ot4.transpose(0, 1, 3, 2).reshape(n_tok, ATT_WIDTH)

    gate_bias = jnp.pad(jnp.concatenate([i_bias, f_bias]), (0, GATE_ROWS - n_gate))[:, None]
    shp = (bsz, seq, MLSTM_WIDTH)
    ym = _mlstm(qc.reshape(shp), kc.reshape(shp), mv.reshape(shp), og.reshape(shp),
                gates_t, gate_bias, mlstm_g[None, :])

    w_out_b = w_out.astype(BF16)
    out = _output_projection(x2, o2, sz, ym.reshape(n_tok, MLSTM_WIDTH), attn_g[None, :],
                             w_out_b[:ATT_WIDTH], w_out_b[ATT_WIDTH:], post_g[None, :])
    return out.reshape(bsz, seq, d_model)


def kernel(x, pre_norm_g, w_in, mlstm_i_bias, mlstm_f_bias, conv_w, conv_b, attn_out_g, mlstm_out_g,
           w_out, post_norm_g):
    for l in range(pre_norm_g.shape[0]):
        x = _layer(x, pre_norm_g[l], w_in[l], mlstm_i_bias[l], mlstm_f_bias[l], conv_w[l], conv_b[l],
                   attn_out_g[l], mlstm_out_g[l], w_out[l], post_norm_g[l])
    return x
```

```python
import functools

import jax
import jax.numpy as jnp
from jax import lax
from jax.experimental import pallas as pl
from jax.experimental.pallas import tpu as pltpu

F32 = jnp.float32
BF16 = jnp.bfloat16

ATT_HEADS = 8
ATT_HEAD_DIM = 64
ATT_WIDTH = ATT_HEADS * ATT_HEAD_DIM
ROT_DIM = ATT_HEAD_DIM // 4
ROPE_THETA = 500000.0
MOBA_BLOCK = 256
MOBA_TOPK = 3
MLSTM_HEADS = 4
MLSTM_HEAD_DIM = 128
MLSTM_WIDTH = MLSTM_HEADS * MLSTM_HEAD_DIM
CONV_WIDTH = 4
NORM_EPS = 1e-6

LANES = 128
SUBLANES = 8
GATE_ROWS = 16
PROJ_ROWS = 512
PROJ_EPILOGUE_ROWS = 32
OUT_ROWS = 1024
MLSTM_CHUNK = 256
MLSTM_CHUNKS_PER_STEP = 2
NEG_BIG = -1e30
LOG2_E = 1.4426950408889634
_SKEW = 2
VMEM_LIMIT = 56 * 1024 * 1024


def _sigmoid(x):
    return 1.0 / (1.0 + jnp.exp(-x))


def _silu(x):
    return x * _sigmoid(x)


def _log_sigmoid(x):
    return jnp.minimum(x, 0.0) - jnp.log(1.0 + jnp.exp(-jnp.abs(x)))


def _split3(x):
    hi = x.astype(BF16)
    r1 = x - hi.astype(F32)
    mid = r1.astype(BF16)
    lo = (r1 - mid.astype(F32)).astype(BF16)
    return hi, mid, lo


def _dot(a, b):
    return jnp.dot(a, b, preferred_element_type=F32)


def _proj_kernel(x_ref, g_ref, w_ref, wgt_ref, cos_ref, sa_ref, sb_ref, cw_ref, cb_ref,
                 q_ref, k_ref, v_ref, sz_ref, qc_ref, kc_ref, mv_ref, og_ref, gate_ref,
                 pq_ref, pk_ref, acc0_ref, acc1_ref, gt0_ref, gt1_ref, *, seq_tiles, n_tiles):
    s = pl.program_id(0)
    first_of_seq = ((s - 1) % seq_tiles) == 0
    tm = x_ref.shape[0]
    halo = SUBLANES
    n_groups = w_ref.shape[1] // ATT_WIDTH

    def matmul_parts(acc_ref, gt_ref):
        x = x_ref[...]
        ms = jnp.mean(x * x, axis=-1, keepdims=True)
        hb = ((x * lax.rsqrt(ms + NORM_EPS)) * g_ref[...]).astype(BF16)

        def group(gi):
            acc_ref[gi] = _dot(hb, w_ref[:, gi * ATT_WIDTH:(gi + 1) * ATT_WIDTH])

        def gates():
            gt_ref[...] = lax.dot_general(wgt_ref[...], hb, (((1,), (1,)), ((), ())),
                                          preferred_element_type=F32)

        return [functools.partial(group, gi) for gi in range(n_groups)] + [gates]

    def rope(acc, rs):
        cos = cos_ref[rs, :]
        sa = sa_ref[rs, :]
        sb = sb_ref[rs, :]
        parts = []
        for c in range(ATT_WIDTH // LANES):
            xs = acc[:, c * LANES:(c + 1) * LANES]
            parts.append(xs * cos
                         + pltpu.roll(xs, LANES - ROT_DIM // 2, 1) * sa
                         + pltpu.roll(xs, ROT_DIM // 2, 1) * sb)
        return jnp.concatenate(parts, axis=1)

    def conv_silu(acc_ref, gi, r, prev_ref, w, b):
        rows = PROJ_EPILOGUE_ROWS
        if r == 0:
            prev = jnp.where(first_of_seq, 0.0, prev_ref[...])
        else:
            prev = acc_ref[gi, r - halo:r, :]
        ext = jnp.concatenate([prev, acc_ref[gi, r:r + rows, :]], axis=0)
        out = b
        for j in range(CONV_WIDTH):
            off = halo - (CONV_WIDTH - 1) + j
            out = out + w[j:j + 1, :] * ext[off:off + rows, :]
        return _silu(out)

    def epilogue_parts(acc_ref, gt_ref):
        cw = cw_ref[...]
        cb = cb_ref[...]

        def chunk(r):
            rs = slice(r, r + PROJ_EPILOGUE_ROWS)
            q_ref[rs, :] = (rope(acc_ref[0, rs, :], rs) * (LOG2_E * ATT_HEAD_DIM ** -0.5)).astype(q_ref.dtype)
            k_ref[rs, :] = rope(acc_ref[1, rs, :], rs).astype(k_ref.dtype)
            v_ref[rs, :] = acc_ref[2, rs, :].astype(v_ref.dtype)
            sz_ref[rs, :] = _silu(acc_ref[3, rs, :]).astype(sz_ref.dtype)
            qc = conv_silu(acc_ref, 4, r, pq_ref, cw[:, :MLSTM_WIDTH], cb[:, :MLSTM_WIDTH])
            qc_ref[rs, :] = qc.astype(qc_ref.dtype)
            kc = conv_silu(acc_ref, 5, r, pk_ref, cw[:, MLSTM_WIDTH:], cb[:, MLSTM_WIDTH:])
            kc_ref[rs, :] = (kc * (MLSTM_HEAD_DIM ** -0.5)).astype(kc_ref.dtype)
            mv_ref[rs, :] = acc_ref[6, rs, :].astype(mv_ref.dtype)
            og_ref[rs, :] = (_sigmoid(acc_ref[7, rs, :]) * _silu(acc_ref[8, rs, :])).astype(og_ref.dtype)

        def tail():
            pq_ref[...] = acc_ref[4, tm - halo:, :]
            pk_ref[...] = acc_ref[5, tm - halo:, :]
            gate_ref[...] = gt_ref[...]

        return [functools.partial(chunk, r) for r in range(0, tm, PROJ_EPILOGUE_ROWS)] + [tail]

    def step(mat_bufs, epi_bufs):
        mats = matmul_parts(*mat_bufs) if mat_bufs is not None else []
        epis = epilogue_parts(*epi_bufs) if epi_bufs is not None else []
        per_mat = -(-len(epis) // max(len(mats), 1))
        for part in mats:
            part()
            for e in epis[:per_mat]:
                e()
            epis = epis[per_mat:]
        for e in epis:
            e()

    bufs = ((acc0_ref, gt0_ref), (acc1_ref, gt1_ref))

    @pl.when(s == 0)
    def _():
        step(bufs[0], None)

    for parity in range(2):
        @pl.when((s > 0) & (s < n_tiles) & (s % 2 == parity))
        def _():
            step(bufs[parity], bufs[1 - parity])

    @pl.when(s == n_tiles)
    def _():
        step(None, bufs[(n_tiles - 1) % 2])


def _rope_tables(seq):
    half = ROT_DIM // 2
    inv_freq = jnp.power(ROPE_THETA, -jnp.arange(half, dtype=F32) * 2.0 / ROT_DIM)
    ang = jnp.arange(seq, dtype=jnp.int32).astype(F32)[:, None] * inv_freq[None, :]
    cos = jnp.cos(ang)
    sin = jnp.sin(ang)
    ones = jnp.ones((seq, ATT_HEAD_DIM - ROT_DIM), F32)
    zeros = jnp.zeros((seq, ATT_HEAD_DIM - ROT_DIM), F32)
    zh = jnp.zeros((seq, half), F32)
    cos_h = jnp.concatenate([cos, cos, ones], axis=1)
    sa_h = jnp.concatenate([-sin, zh, zeros], axis=1)
    sb_h = jnp.concatenate([zh, sin, zeros], axis=1)
    rep = LANES // ATT_HEAD_DIM
    return (jnp.tile(cos_h, (1, rep)), jnp.tile(sa_h, (1, rep)), jnp.tile(sb_h, (1, rep)))


def _input_projection(x2, pre_g, w_main, w_gate_t, tables, conv_w, conv_b, seq):
    n_tok, d_model = x2.shape
    tm = PROJ_ROWS
    seq_tiles = seq // tm
    n_tiles = n_tok // tm
    n_groups = w_main.shape[1] // ATT_WIDTH
    row_in = lambda s: (jnp.minimum(s, n_tiles - 1), 0)
    row_out = lambda s: (jnp.maximum(s - 1, 0), 0)
    fixed = lambda s: (0, 0)
    tab = lambda s: (jnp.maximum(s - 1, 0) % seq_tiles, 0)
    wide = jax.ShapeDtypeStruct((n_tok, ATT_WIDTH), BF16)
    n_wide = 8
    return pl.pallas_call(
        functools.partial(_proj_kernel, seq_tiles=seq_tiles, n_tiles=n_tiles),
        out_shape=(wide,) * n_wide + (jax.ShapeDtypeStruct((GATE_ROWS, n_tok), F32),),
        grid=(n_tiles + 1,),
        in_specs=[
            pl.BlockSpec((tm, d_model), row_in),
            pl.BlockSpec((1, d_model), fixed),
            pl.BlockSpec(w_main.shape, fixed),
            pl.BlockSpec(w_gate_t.shape, fixed),
            pl.BlockSpec((tm, LANES), tab),
            pl.BlockSpec((tm, LANES), tab),
            pl.BlockSpec((tm, LANES), tab),
            pl.BlockSpec(conv_w.shape, fixed),
            pl.BlockSpec(conv_b.shape, fixed),
        ],
        out_specs=tuple(pl.BlockSpec((tm, ATT_WIDTH), row_out) for _ in range(n_wide))
        + (pl.BlockSpec((GATE_ROWS, tm), lambda s: (0, jnp.maximum(s - 1, 0))),),
        scratch_shapes=[
            pltpu.VMEM((SUBLANES, MLSTM_WIDTH), F32),
            pltpu.VMEM((SUBLANES, MLSTM_WIDTH), F32),
            pltpu.VMEM((n_groups, tm, ATT_WIDTH), F32),
            pltpu.VMEM((n_groups, tm, ATT_WIDTH), F32),
            pltpu.VMEM((GATE_ROWS, tm), F32),
            pltpu.VMEM((GATE_ROWS, tm), F32),
        ],
        compiler_params=pltpu.CompilerParams(
            dimension_semantics=("arbitrary",), vmem_limit_bytes=VMEM_LIMIT),
        name="input_projection",
    )(x2, pre_g, w_main, w_gate_t, *tables, conv_w, conv_b)


def _moba_kernel(own_tab, past_tab, qt_ref, k_ref, vt_ref, o_ref,
                 kst_hi, kst_mid, kst_lo, bias_ref, m_ref, acc_ref,
                 *stage_bufs, nb, own_ticks, past_ticks):
    blk = MOBA_BLOCK
    hd = ATT_HEAD_DIM

    lane_head = lax.broadcasted_iota(jnp.int32, (nb, ATT_WIDTH), 1) // hd
    means = []
    for n in range(nb):
        means.append(jnp.mean(k_ref[0, n].astype(F32), axis=0, keepdims=True))
    km = jnp.concatenate(means, axis=0)
    for h in range(ATT_HEADS):
        hi, mid, lo = _split3(jnp.where(lane_head == h, km, 0.0))
        kst_hi[h * nb:(h + 1) * nb, :] = hi
        kst_mid[h * nb:(h + 1) * nb, :] = mid
        kst_lo[h * nb:(h + 1) * nb, :] = lo

    n_iota = lax.broadcasted_iota(jnp.int32, (nb, blk), 0)
    n_iota_f = n_iota.astype(F32)
    tail_iota = lax.broadcasted_iota(jnp.int32, (SUBLANES, blk), 0)
    tail_rows = jnp.where(tail_iota == 0, 0.0, NEG_BIG)

    def prepare(i, carry):
        qt = qt_ref[0, i]
        gates = _dot(kst_hi[...], qt) + _dot(kst_mid[...], qt) + _dot(kst_lo[...], qt)
        past = n_iota < i
        for h in range(ATT_HEADS):
            g = jnp.where(past, gates[h * nb:(h + 1) * nb, :], -jnp.inf)
            sel = jnp.zeros((nb, blk), jnp.bool_)
            for _ in range(min(MOBA_TOPK, nb)):
                gmax = jnp.max(g, axis=0, keepdims=True)
                first = jnp.min(jnp.where(g == gmax, n_iota_f, float(nb)), axis=0, keepdims=True)
                hit = n_iota_f == first
                sel = sel | hit
                g = jnp.where(hit, -jnp.inf, g)
            bias_ref[i, h, 0:nb, :] = jnp.where(sel & past, 0.0, NEG_BIG)
            bias_ref[i, h, nb:nb + SUBLANES, :] = tail_rows
            m_ref[i, h] = jnp.full((1, blk), NEG_BIG, F32)
            acc_ref[i, h] = jnp.zeros(acc_ref.shape[2:], F32)
        return carry

    lax.fori_loop(0, nb, prepare, 0)

    l_iota = lax.broadcasted_iota(jnp.int32, (blk, blk), 0)
    q_iota = lax.broadcasted_iota(jnp.int32, (blk, blk), 1)
    causal = l_iota <= q_iota
    row_head = lax.broadcasted_iota(jnp.int32, (LANES, blk), 0) // hd
    ones_rows = jnp.ones((2 * SUBLANES, blk), BF16)
    s_bufs, c_bufs, p_bufs, a_bufs = (stage_bufs[j * _SKEW:(j + 1) * _SKEW] for j in range(4))

    def run(tab, n_ticks, own):
        def scores(tick, slot):
            qi = tab[0, tick]
            kb = k_ref[0, tab[1, tick]]
            for h in range(ATT_HEADS):
                pr = slice((h // 2) * LANES, (h // 2 + 1) * LANES)
                qpair = qt_ref[0, qi, pr, :]
                qh = jnp.where(row_head == (h % 2), qpair, jnp.zeros_like(qpair))
                st = _dot(kb[:, pr], qh)
                st = jnp.where(causal, st, NEG_BIG) if own else st
                s_bufs[slot][h] = st
                c_bufs[slot][h] = jnp.max(st.reshape(blk // SUBLANES, SUBLANES, blk), axis=0)

        def softmax_update(tick, slot):
            qi = tab[0, tick]
            mask_row = tab[2, tick]
            for h in range(ATT_HEADS):
                st = s_bufs[slot][h]
                b = bias_ref[qi, h, pl.ds(mask_row, 1), :]
                m_old = m_ref[qi, h]
                m_new = jnp.maximum(m_old, jnp.max(c_bufs[slot][h], axis=0, keepdims=True) + b)
                p_bufs[slot][h] = jnp.exp2(st - (m_new - b)).astype(BF16)
                a_bufs[slot][h] = jnp.exp2(m_old - m_new)
                m_ref[qi, h] = m_new

        def values(tick, slot):
            qi = tab[0, tick]
            vtb = vt_ref[0, tab[1, tick]]
            for h in range(ATT_HEADS):
                lhs = jnp.concatenate([vtb[h * hd:(h + 1) * hd, :], ones_rows], axis=0)
                acc_ref[qi, h] = a_bufs[slot][h] * acc_ref[qi, h] + _dot(lhs, p_bufs[slot][h])

        for t in range(_SKEW):
            scores(t, t % _SKEW)
        for t in range(_SKEW, 2 * _SKEW):
            softmax_update(t - _SKEW, t % _SKEW)
            scores(t, t % _SKEW)

        def body(u, carry):
            t = _SKEW * (u + 2)
            for d in range(_SKEW):
                values(t + d - 2 * _SKEW, d)
                softmax_update(t + d - _SKEW, d)
                scores(t + d, d)
            return carry

        lax.fori_loop(0, -(-n_ticks // _SKEW), body, 0)

    run(own_tab, own_ticks, True)
    run(past_tab, past_ticks, False)

    def finalize(i, carry):
        for h in range(ATT_HEADS):
            acc = acc_ref[i, h]
            o_ref[0, i, h * hd:(h + 1) * hd, :] = (acc[:hd, :] / acc[hd:hd + 1, :]).astype(o_ref.dtype)
        return carry

    lax.fori_loop(0, nb, finalize, 0)


def _tick_table(ticks, n_ticks, pad):
    length = _SKEW * (-(-n_ticks // _SKEW) + 2)
    rows = list(ticks) + [pad] * (length - len(ticks))
    return jnp.asarray(rows, jnp.int32).T


def _moba_attention(qt4, k4, vt4):
    bsz, nb, width, blk = qt4.shape
    hn = ATT_HEADS * nb
    own = [(i, i, nb) for i in range(nb)]
    past = [(i, n, n) for i in range(nb) for n in range(i)]
    own_tab = _tick_table(own, len(own), (nb - 1, nb - 1, nb + 1))
    past_tab = _tick_table(past, len(past), (nb - 1, 0, nb + 1))
    tile_f32 = pltpu.VMEM((ATT_HEADS, blk, blk), F32)
    tile_bf16 = pltpu.VMEM((ATT_HEADS, blk, blk), BF16)
    row_f32 = pltpu.VMEM((ATT_HEADS, 1, blk), F32)
    part_f32 = pltpu.VMEM((ATT_HEADS, SUBLANES, blk), F32)
    whole = lambda b, *_: (b, 0, 0, 0)
    return pl.pallas_call(
        functools.partial(_moba_kernel, nb=nb, own_ticks=len(own), past_ticks=len(past)),
        out_shape=jax.ShapeDtypeStruct((bsz, nb, width, blk), BF16),
        grid_spec=pltpu.PrefetchScalarGridSpec(
            num_scalar_prefetch=2,
            grid=(bsz,),
            in_specs=[
                pl.BlockSpec((1, nb, width, blk), whole),
                pl.BlockSpec((1, nb, blk, width), whole),
                pl.BlockSpec((1, nb, width, blk), whole),
            ],
            out_specs=pl.BlockSpec((1, nb, width, blk), whole),
            scratch_shapes=[
                pltpu.VMEM((hn, width), BF16),
                pltpu.VMEM((hn, width), BF16),
                pltpu.VMEM((hn, width), BF16),
                pltpu.VMEM((nb, ATT_HEADS, nb + SUBLANES, blk), F32),
                pltpu.VMEM((nb, ATT_HEADS, 1, blk), F32),
                pltpu.VMEM((nb, ATT_HEADS, ATT_HEAD_DIM + 2 * SUBLANES, blk), F32),
                *([tile_f32] * _SKEW + [part_f32] * _SKEW + [tile_bf16] * _SKEW + [row_f32] * _SKEW),
            ],
        ),
        compiler_params=pltpu.CompilerParams(
            dimension_semantics=("arbitrary",), vmem_limit_bytes=VMEM_LIMIT),
        name="moba_attention",
    )(own_tab, past_tab, qt4, k4, vt4)


def _mlstm_kernel(qc_ref, kc_ref, mv_ref, og_ref, gate_ref, bias_ref, gain_ref, triu_ref, select_ref,
                  o_ref, ct_ref, m_ref):
    @pl.when(pl.program_id(1) == 0)
    def _():
        ct_ref[...] = jnp.zeros_like(ct_ref)
        m_ref[...] = jnp.zeros_like(m_ref)

    L = MLSTM_CHUNK
    for j in range(MLSTM_CHUNKS_PER_STEP):
        ts = slice(j * L, (j + 1) * L)
        _mlstm_chunk(qc_ref.at[:, ts, :], kc_ref.at[:, ts, :], mv_ref.at[:, ts, :], og_ref.at[:, ts, :],
                     gate_ref.at[:, ts], bias_ref, gain_ref, triu_ref, select_ref,
                     o_ref.at[:, ts, :], ct_ref, m_ref)


def _mlstm_chunk(qc_ref, kc_ref, mv_ref, og_ref, gate_ref, bias_ref, gain_ref, triu_ref, select_ref,
                 o_ref, ct_ref, m_ref):
    L = MLSTM_CHUNK
    hd = MLSTM_HEAD_DIM
    nh = MLSTM_HEADS

    gr = gate_ref[...] + bias_ref[...]
    r_hi, r_mid, r_lo = _split3(_log_sigmoid(gr) * LOG2_E)
    triu = triu_ref[...]
    bcum = _dot(r_hi, triu) + _dot(r_mid, triu) + _dot(r_lo, triu)
    row = lax.broadcasted_iota(jnp.int32, gr.shape, 0)
    rows_ib = jnp.where(row < nh, gr * LOG2_E, bcum)
    rows_c = rows_ib[0:nh, :] - rows_ib[nh:2 * nh, :]
    lhs_t = jnp.concatenate(list(_split3(rows_ib)) + [jnp.ones((GATE_ROWS, L), BF16)], axis=0)

    t_iota = lax.broadcasted_iota(jnp.int32, (L, LANES), 0)
    s_iota = lax.broadcasted_iota(jnp.int32, (L, LANES), 1)
    zeros_tail = jnp.zeros((GATE_ROWS, 2 * LANES), BF16)

    gain = gain_ref[...]
    heads = range(MLSTM_HEADS)
    slices = [slice(h * hd, (h + 1) * hd) for h in heads]
    srow = lax.broadcasted_iota(jnp.int32, (GATE_ROWS, L), 0)

    res_all, qk_all, inter_all = [], [], []
    for h in heads:
        c = rows_c[h:h + 1, :]
        c_hi = c.astype(BF16).astype(F32)
        c_mid = (c - c_hi).astype(BF16).astype(F32)
        c_lo = c - c_hi - c_mid
        tail = jnp.where(srow == 0, c_hi, jnp.where(srow == 1, c_mid, jnp.where(srow == 2, c_lo, 0.0)))
        sel_h = jnp.concatenate(
            [select_ref[h], jnp.concatenate([tail.astype(BF16), zeros_tail], axis=1)], axis=0)
        res_all.append(lax.dot_general(lhs_t, sel_h, (((0,), (0,)), ((), ())), preferred_element_type=F32))
        qb = qc_ref[0, :, slices[h]]
        kb = kc_ref[0, :, slices[h]]
        qk_all.append(lax.dot_general(qb, kb, (((1,), (1,)), ((), ())), preferred_element_type=F32))
        inter_all.append(_dot(qb, ct_ref[h].astype(BF16)))

    s_all, wv_all, carry_all = [], [], []
    for h in heads:
        sl = slices[h]
        res = res_all[h]
        qk = qk_all[h]
        v = mv_ref[0, :, sl]
        m_prev = m_ref[h]
        b_col = res[:, L:L + LANES]
        i_col = res[:, L + LANES:L + 2 * LANES]

        dslabs = []
        for j in range(L // LANES):
            keep = (s_iota + j * LANES) <= t_iota
            dslabs.append(jnp.where(keep, res[:, j * LANES:(j + 1) * LANES], -jnp.inf))
        dmax = jnp.max(dslabs[0], axis=-1, keepdims=True)
        for d in dslabs[1:]:
            dmax = jnp.maximum(dmax, jnp.max(d, axis=-1, keepdims=True))
        inter = b_col + m_prev
        m_t = jnp.maximum(inter, dmax)
        w_inter = jnp.exp2(inter - m_t)
        s = jnp.concatenate([qk[:, j * LANES:(j + 1) * LANES] * jnp.exp2(d - m_t)
                             for j, d in enumerate(dslabs)], axis=1)

        b_last = b_col[L - 1:L, :]
        g = b_last - b_col + i_col
        m_new = jnp.maximum(b_last + m_prev, jnp.max(g, axis=0, keepdims=True))
        decay = jnp.exp2(b_last + m_prev - m_new)
        w = jnp.exp2(g - m_new)
        m_ref[h] = m_new
        s_all.append(s.astype(BF16))
        wv_all.append(jnp.concatenate([w * v.astype(F32), w], axis=1).astype(BF16))
        carry_all.append((w_inter, jnp.exp2(-m_t), jnp.concatenate([decay, decay], axis=1)))

    ones_cols = jnp.ones((L, hd), BF16)
    intra_all = [_dot(s_all[h], jnp.concatenate([mv_ref[0, :, slices[h]], ones_cols], axis=1))
                 for h in heads]
    for h in heads:
        ct_ref[h] = carry_all[h][2] * ct_ref[h] + lax.dot_general(
            kc_ref[0, :, slices[h]], wv_all[h], (((0,), (0,)), ((), ())), preferred_element_type=F32)

    for h in heads:
        sl = slices[h]
        w_inter, floor, _ = carry_all[h]
        num = w_inter * inter_all[h][:, :hd] + intra_all[h][:, :hd]
        nq = w_inter * inter_all[h][:, hd:] + intra_all[h][:, hd:]
        hh = num / jnp.maximum(jnp.abs(nq), floor)
        mu = jnp.mean(hh, axis=-1, keepdims=True)
        var = jnp.mean(jnp.square(hh - mu), axis=-1, keepdims=True)
        hn = (hh - mu) * lax.rsqrt(var + NORM_EPS) * gain[:, sl]
        o_ref[0, :, sl] = (og_ref[0, :, sl].astype(F32) * hn).astype(o_ref.dtype)


def _mlstm(qc, kc, mv, og, gates_t, gate_bias, out_g):
    bsz, seq, width = qc.shape
    L = MLSTM_CHUNK
    step = L * MLSTM_CHUNKS_PER_STEP
    nc = seq // step
    nh = MLSTM_HEADS
    triu = jnp.triu(jnp.ones((L, L), F32)).astype(BF16)
    r_in = jnp.arange(3 * GATE_ROWS)[None, :, None] % GATE_ROWS
    col = jnp.arange(L + 2 * LANES)[None, None, :]
    head = jnp.arange(nh)[:, None, None]
    select = jnp.where(col < L + LANES, r_in == nh + head, r_in == head).astype(BF16)
    tok = lambda b, c: (b, c, 0)
    fixed = lambda b, c: (0, 0)
    wide = pl.BlockSpec((1, step, width), tok)
    return pl.pallas_call(
        _mlstm_kernel,
        out_shape=jax.ShapeDtypeStruct((bsz, seq, width), BF16),
        grid=(bsz, nc),
        in_specs=[
            wide, wide, wide, wide,
            pl.BlockSpec((GATE_ROWS, step), lambda b, c: (0, b * nc + c)),
            pl.BlockSpec((GATE_ROWS, 1), fixed),
            pl.BlockSpec(out_g.shape, fixed),
            pl.BlockSpec((L, L), fixed),
            pl.BlockSpec(select.shape, lambda b, c: (0, 0, 0)),
        ],
        out_specs=wide,
        scratch_shapes=[
            pltpu.VMEM((nh, MLSTM_HEAD_DIM, 2 * MLSTM_HEAD_DIM), F32),
            pltpu.VMEM((nh, 1, LANES), F32),
        ],
        compiler_params=pltpu.CompilerParams(
            dimension_semantics=("arbitrary", "arbitrary"), vmem_limit_bytes=VMEM_LIMIT),
        name="mlstm",
    )(qc, kc, mv, og, gates_t, gate_bias, out_g, triu, select)


def _out_kernel(x_ref, o_ref, sz_ref, ym_ref, ag_ref, w_ref, pg_ref, out_ref):
    o = o_ref[...].astype(F32)
    ms = jnp.mean(o * o, axis=-1, keepdims=True)
    ya = (o * lax.rsqrt(ms + NORM_EPS)) * ag_ref[...] * sz_ref[...].astype(F32)
    y = _dot(ya.astype(BF16), w_ref[:ATT_WIDTH, :]) + _dot(ym_ref[...], w_ref[ATT_WIDTH:, :])
    ms2 = jnp.mean(y * y, axis=-1, keepdims=True)
    out_ref[...] = x_ref[...] + (y * lax.rsqrt(ms2 + NORM_EPS)) * pg_ref[...]


def _output_projection(x2, o2, sz, ym, attn_g, w_out, post_g):
    n_tok, d_model = x2.shape
    tm = OUT_ROWS
    row = lambda i: (i, 0)
    fixed = lambda i: (0, 0)
    return pl.pallas_call(
        _out_kernel,
        out_shape=jax.ShapeDtypeStruct((n_tok, d_model), F32),
        grid=(n_tok // tm,),
        in_specs=[
            pl.BlockSpec((tm, d_model), row),
            pl.BlockSpec((tm, ATT_WIDTH), row),
            pl.BlockSpec((tm, ATT_WIDTH), row),
            pl.BlockSpec((tm, MLSTM_WIDTH), row),
            pl.BlockSpec((1, ATT_WIDTH), fixed),
            pl.BlockSpec(w_out.shape, fixed),
            pl.BlockSpec((1, d_model), fixed),
        ],
        out_specs=pl.BlockSpec((tm, d_model), row),
        compiler_params=pltpu.CompilerParams(
            dimension_semantics=("arbitrary",), vmem_limit_bytes=VMEM_LIMIT),
        name="output_projection",
    )(x2, o2, sz, ym, attn_g, w_out, post_g)


def _layer(x, pre_g, w_in, i_bias, f_bias, conv_w, conv_b, attn_g, mlstm_g, w_out, post_g):
    bsz, seq, d_model = x.shape
    assert seq % MOBA_BLOCK == 0 and seq % PROJ_ROWS == 0
    assert seq % (MLSTM_CHUNK * MLSTM_CHUNKS_PER_STEP) == 0
    nb = seq // MOBA_BLOCK
    n_tok = bsz * seq
    assert n_tok % OUT_ROWS == 0
    n_main = 4 * ATT_WIDTH + 5 * MLSTM_WIDTH
    n_gate = 2 * MLSTM_HEADS
    assert w_in.shape == (d_model, n_main + n_gate)

    w_main = w_in[:, :n_main].astype(BF16)
    w_gate_t = jnp.pad(w_in[:, n_main:].T, ((0, GATE_ROWS - n_gate), (0, 0))).astype(BF16)
    x2 = x.reshape(n_tok, d_model)
    (q, k, v, sz, qc, kc, mv, og, gates_t) = _input_projection(
        x2, pre_g[None, :], w_main, w_gate_t, _rope_tables(seq), conv_w, conv_b[None, :], seq)

    to_blocks_t = lambda t: t.reshape(bsz, nb, MOBA_BLOCK, ATT_WIDTH).transpose(0, 1, 3, 2)
    ot4 = _moba_attention(to_blocks_t(q), k.reshape(bsz, nb, MOBA_BLOCK, ATT_WIDTH), to_blocks_t(v))
    o2 = ot4.transpose(0, 1, 3, 2).reshape(n_tok, ATT_WIDTH)

    gate_bias = jnp.pad(jnp.concatenate([i_bias, f_bias]), (0, GATE_ROWS - n_gate))[:, None]
    shp = (bsz, seq, MLSTM_WIDTH)
    ym = _mlstm(qc.reshape(shp), kc.reshape(shp), mv.reshape(shp), og.reshape(shp),
                gates_t, gate_bias, mlstm_g[None, :])

    out = _output_projection(x2, o2, sz, ym.reshape(n_tok, MLSTM_WIDTH), attn_g[None, :],
                             w_out.astype(BF16), post_g[None, :])
    return out.reshape(bsz, seq, d_model)


def kernel(x, pre_norm_g, w_in, mlstm_i_bias, mlstm_f_bias, conv_w, conv_b, attn_out_g, mlstm_out_g,
           w_out, post_norm_g):
    for l in range(pre_norm_g.shape[0]):
        x = _layer(x, pre_norm_g[l], w_in[l], mlstm_i_bias[l], mlstm_f_bias[l], conv_w[l], conv_b[l],
                   attn_out_g[l], mlstm_out_g[l], w_out[l], post_norm_g[l])
    return x
```

```python
import functools

import jax
import jax.numpy as jnp
from jax import lax
from jax.experimental import pallas as pl
from jax.experimental.pallas import tpu as pltpu

F32 = jnp.float32
BF16 = jnp.bfloat16

ATT_HEADS = 8
ATT_HEAD_DIM = 64
ATT_WIDTH = ATT_HEADS * ATT_HEAD_DIM
ROT_DIM = ATT_HEAD_DIM // 4
ROPE_THETA = 500000.0
MOBA_BLOCK = 256
MOBA_TOPK = 3
MLSTM_HEADS = 4
MLSTM_HEAD_DIM = 128
MLSTM_WIDTH = MLSTM_HEADS * MLSTM_HEAD_DIM
CONV_WIDTH = 4
NORM_EPS = 1e-6

LANES = 128
SUBLANES = 8
GATE_ROWS = 16
PROJ_ROWS = 512
PROJ_EPILOGUE_ROWS = 32
OUT_ROWS = 1024
MLSTM_CHUNK = 256
MLSTM_CHUNKS_PER_STEP = 2
NEG_BIG = -1e30
LOG2_E = 1.4426950408889634
_SKEW = 2
VMEM_LIMIT = 56 * 1024 * 1024


def _sigmoid(x):
    return 1.0 / (1.0 + jnp.exp(-x))


def _silu(x):
    return x * _sigmoid(x)


def _log_sigmoid(x):
    return jnp.minimum(x, 0.0) - jnp.log(1.0 + jnp.exp(-jnp.abs(x)))


def _split3(x):
    hi = x.astype(BF16)
    r1 = x - hi.astype(F32)
    mid = r1.astype(BF16)
    lo = (r1 - mid.astype(F32)).astype(BF16)
    return hi, mid, lo


def _dot(a, b):
    return jnp.dot(a, b, preferred_element_type=F32)


def _proj_kernel(x_ref, g_ref, w_ref, wgt_ref, cos_ref, sa_ref, sb_ref, cw_ref, cb_ref,
                 qt_ref, k_ref, vt_ref, sz_ref, qc_ref, kc_ref, mv_ref, og_ref, gate_ref,
                 pq_ref, pk_ref, acc0_ref, acc1_ref, gt0_ref, gt1_ref, *, seq_tiles, n_tiles):
    s = pl.program_id(0)
    first_of_seq = ((s - 1) % seq_tiles) == 0
    tm = x_ref.shape[0]
    halo = SUBLANES
    n_groups = w_ref.shape[1] // ATT_WIDTH

    def matmul_parts(acc_ref, gt_ref):
        x = x_ref[...]
        ms = jnp.mean(x * x, axis=-1, keepdims=True)
        hb = ((x * lax.rsqrt(ms + NORM_EPS)) * g_ref[...]).astype(BF16)

        def group(gi):
            acc_ref[gi] = _dot(hb, w_ref[:, gi * ATT_WIDTH:(gi + 1) * ATT_WIDTH])

        def gates():
            gt_ref[...] = lax.dot_general(wgt_ref[...], hb, (((1,), (1,)), ((), ())),
                                          preferred_element_type=F32)

        return [functools.partial(group, gi) for gi in range(n_groups)] + [gates]

    def rope(acc, rs):
        cos = cos_ref[rs, :]
        sa = sa_ref[rs, :]
        sb = sb_ref[rs, :]
        parts = []
        for c in range(ATT_WIDTH // LANES):
            xs = acc[:, c * LANES:(c + 1) * LANES]
            parts.append(xs * cos
                         + pltpu.roll(xs, LANES - ROT_DIM // 2, 1) * sa
                         + pltpu.roll(xs, ROT_DIM // 2, 1) * sb)
        return jnp.concatenate(parts, axis=1)

    def conv_silu(acc_ref, gi, r, prev_ref, w, b):
        rows = PROJ_EPILOGUE_ROWS
        if r == 0:
            prev = jnp.where(first_of_seq, 0.0, prev_ref[...])
        else:
            prev = acc_ref[gi, r - halo:r, :]
        ext = jnp.concatenate([prev, acc_ref[gi, r:r + rows, :]], axis=0)
        out = b
        for j in range(CONV_WIDTH):
            off = halo - (CONV_WIDTH - 1) + j
            out = out + w[j:j + 1, :] * ext[off:off + rows, :]
        return _silu(out)

    def epilogue_parts(acc_ref, gt_ref):
        cw = cw_ref[...]
        cb = cb_ref[...]

        def chunk(r):
            rs = slice(r, r + PROJ_EPILOGUE_ROWS)
            acc_ref[0, rs, :] = rope(acc_ref[0, rs, :], rs) * (LOG2_E * ATT_HEAD_DIM ** -0.5)
            k_ref[rs, :] = rope(acc_ref[1, rs, :], rs).astype(k_ref.dtype)
            sz_ref[rs, :] = _silu(acc_ref[3, rs, :]).astype(sz_ref.dtype)
            qc = conv_silu(acc_ref, 4, r, pq_ref, cw[:, :MLSTM_WIDTH], cb[:, :MLSTM_WIDTH])
            qc_ref[rs, :] = qc.astype(qc_ref.dtype)
            kc = conv_silu(acc_ref, 5, r, pk_ref, cw[:, MLSTM_WIDTH:], cb[:, MLSTM_WIDTH:])
            kc_ref[rs, :] = (kc * (MLSTM_HEAD_DIM ** -0.5)).astype(kc_ref.dtype)
            mv_ref[rs, :] = acc_ref[6, rs, :].astype(mv_ref.dtype)
            og_ref[rs, :] = (_sigmoid(acc_ref[7, rs, :]) * _silu(acc_ref[8, rs, :])).astype(og_ref.dtype)

        def tail():
            pq_ref[...] = acc_ref[4, tm - halo:, :]
            pk_ref[...] = acc_ref[5, tm - halo:, :]
            gate_ref[...] = gt_ref[...]

        def key_block(j):
            rows = slice(j * MOBA_BLOCK, (j + 1) * MOBA_BLOCK)
            qt_ref[0, j] = acc_ref[0, rows, :].T.astype(qt_ref.dtype)
            vt_ref[0, j] = acc_ref[2, rows, :].T.astype(vt_ref.dtype)

        parts = []
        for r in range(0, tm, PROJ_EPILOGUE_ROWS):
            parts.append(functools.partial(chunk, r))
            if (r + PROJ_EPILOGUE_ROWS) % MOBA_BLOCK == 0:
                parts.append(functools.partial(key_block, r // MOBA_BLOCK))
        return parts + [tail]

    def step(mat_bufs, epi_bufs):
        mats = matmul_parts(*mat_bufs) if mat_bufs is not None else []
        epis = epilogue_parts(*epi_bufs) if epi_bufs is not None else []
        per_mat = -(-len(epis) // max(len(mats), 1))
        for part in mats:
            part()
            for e in epis[:per_mat]:
                e()
            epis = epis[per_mat:]
        for e in epis:
            e()

    bufs = ((acc0_ref, gt0_ref), (acc1_ref, gt1_ref))

    @pl.when(s == 0)
    def _():
        step(bufs[0], None)

    for parity in range(2):
        @pl.when((s > 0) & (s < n_tiles) & (s % 2 == parity))
        def _():
            step(bufs[parity], bufs[1 - parity])

    @pl.when(s == n_tiles)
    def _():
        step(None, bufs[(n_tiles - 1) % 2])


def _rope_tables(seq):
    half = ROT_DIM // 2
    inv_freq = jnp.power(ROPE_THETA, -jnp.arange(half, dtype=F32) * 2.0 / ROT_DIM)
    ang = jnp.arange(seq, dtype=jnp.int32).astype(F32)[:, None] * inv_freq[None, :]
    cos = jnp.cos(ang)
    sin = jnp.sin(ang)
    ones = jnp.ones((seq, ATT_HEAD_DIM - ROT_DIM), F32)
    zeros = jnp.zeros((seq, ATT_HEAD_DIM - ROT_DIM), F32)
    zh = jnp.zeros((seq, half), F32)
    cos_h = jnp.concatenate([cos, cos, ones], axis=1)
    sa_h = jnp.concatenate([-sin, zh, zeros], axis=1)
    sb_h = jnp.concatenate([zh, sin, zeros], axis=1)
    rep = LANES // ATT_HEAD_DIM
    return (jnp.tile(cos_h, (1, rep)), jnp.tile(sa_h, (1, rep)), jnp.tile(sb_h, (1, rep)))


def _input_projection(x2, pre_g, w_main, w_gate_t, tables, conv_w, conv_b, seq):
    n_tok, d_model = x2.shape
    tm = PROJ_ROWS
    seq_tiles = seq // tm
    n_tiles = n_tok // tm
    n_groups = w_main.shape[1] // ATT_WIDTH
    row_in = lambda s: (jnp.minimum(s, n_tiles - 1), 0)
    row_out = lambda s: (jnp.maximum(s - 1, 0), 0)
    fixed = lambda s: (0, 0)
    tab = lambda s: (jnp.maximum(s - 1, 0) % seq_tiles, 0)
    done = lambda s: jnp.maximum(s - 1, 0)
    wide = jax.ShapeDtypeStruct((n_tok, ATT_WIDTH), BF16)
    wide_spec = pl.BlockSpec((tm, ATT_WIDTH), row_out)
    blocked = jax.ShapeDtypeStruct((n_tok // seq, seq // MOBA_BLOCK, ATT_WIDTH, MOBA_BLOCK), BF16)
    blocked_spec = pl.BlockSpec((1, tm // MOBA_BLOCK, ATT_WIDTH, MOBA_BLOCK),
                                lambda s: (done(s) // seq_tiles, done(s) % seq_tiles, 0, 0))
    return pl.pallas_call(
        functools.partial(_proj_kernel, seq_tiles=seq_tiles, n_tiles=n_tiles),
        out_shape=(blocked, wide, blocked) + (wide,) * 5 + (jax.ShapeDtypeStruct((GATE_ROWS, n_tok), F32),),
        grid=(n_tiles + 1,),
        in_specs=[
            pl.BlockSpec((tm, d_model), row_in),
            pl.BlockSpec((1, d_model), fixed),
            pl.BlockSpec(w_main.shape, fixed),
            pl.BlockSpec(w_gate_t.shape, fixed),
            pl.BlockSpec((tm, LANES), tab),
            pl.BlockSpec((tm, LANES), tab),
            pl.BlockSpec((tm, LANES), tab),
            pl.BlockSpec(conv_w.shape, fixed),
            pl.BlockSpec(conv_b.shape, fixed),
        ],
        out_specs=(blocked_spec, wide_spec, blocked_spec) + (wide_spec,) * 5
        + (pl.BlockSpec((GATE_ROWS, tm), lambda s: (0, done(s))),),
        scratch_shapes=[
            pltpu.VMEM((SUBLANES, MLSTM_WIDTH), F32),
            pltpu.VMEM((SUBLANES, MLSTM_WIDTH), F32),
            pltpu.VMEM((n_groups, tm, ATT_WIDTH), F32),
            pltpu.VMEM((n_groups, tm, ATT_WIDTH), F32),
            pltpu.VMEM((GATE_ROWS, tm), F32),
            pltpu.VMEM((GATE_ROWS, tm), F32),
        ],
        compiler_params=pltpu.CompilerParams(
            dimension_semantics=("arbitrary",), vmem_limit_bytes=VMEM_LIMIT),
        name="input_projection",
    )(x2, pre_g, w_main, w_gate_t, *tables, conv_w, conv_b)


def _moba_kernel(own_tab, past_tab, qt_ref, k_ref, vt_ref, o_ref,
                 kst_hi, kst_mid, kst_lo, bias_ref, m_ref, acc_ref,
                 *stage_bufs, nb, own_ticks, past_ticks):
    blk = MOBA_BLOCK
    hd = ATT_HEAD_DIM

    lane_head = lax.broadcasted_iota(jnp.int32, (nb, ATT_WIDTH), 1) // hd
    means = []
    for n in range(nb):
        means.append(jnp.mean(k_ref[0, n].astype(F32), axis=0, keepdims=True))
    km = jnp.concatenate(means, axis=0)
    for h in range(ATT_HEADS):
        hi, mid, lo = _split3(jnp.where(lane_head == h, km, 0.0))
        kst_hi[h * nb:(h + 1) * nb, :] = hi
        kst_mid[h * nb:(h + 1) * nb, :] = mid
        kst_lo[h * nb:(h + 1) * nb, :] = lo

    n_iota = lax.broadcasted_iota(jnp.int32, (nb, blk), 0)
    n_iota_f = n_iota.astype(F32)
    tail_iota = lax.broadcasted_iota(jnp.int32, (SUBLANES, blk), 0)
    tail_rows = jnp.where(tail_iota == 0, 0.0, NEG_BIG)

    def prepare(i, carry):
        qt = qt_ref[0, i]
        gates = _dot(kst_hi[...], qt) + _dot(kst_mid[...], qt) + _dot(kst_lo[...], qt)
        past = n_iota < i
        for h in range(ATT_HEADS):
            g = jnp.where(past, gates[h * nb:(h + 1) * nb, :], -jnp.inf)
            sel = jnp.zeros((nb, blk), jnp.bool_)
            for _ in range(min(MOBA_TOPK, nb)):
                gmax = jnp.max(g, axis=0, keepdims=True)
                first = jnp.min(jnp.where(g == gmax, n_iota_f, float(nb)), axis=0, keepdims=True)
                hit = n_iota_f == first
                sel = sel | hit
                g = jnp.where(hit, -jnp.inf, g)
            bias_ref[i, h, 0:nb, :] = jnp.where(sel & past, 0.0, NEG_BIG)
            bias_ref[i, h, nb:nb + SUBLANES, :] = tail_rows
            m_ref[i, h] = jnp.full((1, blk), NEG_BIG, F32)
            acc_ref[i, h] = jnp.zeros(acc_ref.shape[2:], F32)
        return carry

    lax.fori_loop(0, nb, prepare, 0)

    l_iota = lax.broadcasted_iota(jnp.int32, (blk, blk), 0)
    q_iota = lax.broadcasted_iota(jnp.int32, (blk, blk), 1)
    causal = l_iota <= q_iota
    row_head = lax.broadcasted_iota(jnp.int32, (LANES, blk), 0) // hd
    ones_rows = jnp.ones((2 * SUBLANES, blk), BF16)
    s_bufs, c_bufs, p_bufs, a_bufs = (stage_bufs[j * _SKEW:(j + 1) * _SKEW] for j in range(4))

    def run(tab, n_ticks, own):
        def scores(tick, slot):
            qi = tab[0, tick]
            kb = k_ref[0, tab[1, tick]]
            for h in range(ATT_HEADS):
                pr = slice((h // 2) * LANES, (h // 2 + 1) * LANES)
                qpair = qt_ref[0, qi, pr, :]
                qh = jnp.where(row_head == (h % 2), qpair, jnp.zeros_like(qpair))
                st = _dot(kb[:, pr], qh)
                st = jnp.where(causal, st, NEG_BIG) if own else st
                s_bufs[slot][h] = st
                c_bufs[slot][h] = jnp.max(st.reshape(blk // SUBLANES, SUBLANES, blk), axis=0)

        def softmax_update(tick, slot):
            qi = tab[0, tick]
            mask_row = tab[2, tick]
            for h in range(ATT_HEADS):
                st = s_bufs[slot][h]
                b = bias_ref[qi, h, pl.ds(mask_row, 1), :]
                m_old = m_ref[qi, h]
                m_new = jnp.maximum(m_old, jnp.max(c_bufs[slot][h], axis=0, keepdims=True) + b)
                p_bufs[slot][h] = jnp.exp2(st - (m_new - b)).astype(BF16)
                a_bufs[slot][h] = jnp.exp2(m_old - m_new)
                m_ref[qi, h] = m_new

        def values(tick, slot):
            qi = tab[0, tick]
            vtb = vt_ref[0, tab[1, tick]]
            for h in range(ATT_HEADS):
                lhs = jnp.concatenate([vtb[h * hd:(h + 1) * hd, :], ones_rows], axis=0)
                acc_ref[qi, h] = a_bufs[slot][h] * acc_ref[qi, h] + _dot(lhs, p_bufs[slot][h])

        for t in range(_SKEW):
            scores(t, t % _SKEW)
        for t in range(_SKEW, 2 * _SKEW):
            softmax_update(t - _SKEW, t % _SKEW)
            scores(t, t % _SKEW)

        def body(u, carry):
            t = _SKEW * (u + 2)
            for d in range(_SKEW):
                values(t + d - 2 * _SKEW, d)
                softmax_update(t + d - _SKEW, d)
                scores(t + d, d)
            return carry

        lax.fori_loop(0, -(-n_ticks // _SKEW), body, 0)

    run(own_tab, own_ticks, True)
    run(past_tab, past_ticks, False)

    def finalize(i, carry):
        for h in range(ATT_HEADS):
            acc = acc_ref[i, h]
            o_ref[0, i, h * hd:(h + 1) * hd, :] = (acc[:hd, :] / acc[hd:hd + 1, :]).astype(o_ref.dtype)
        return carry

    lax.fori_loop(0, nb, finalize, 0)


def _tick_table(ticks, n_ticks, pad):
    length = _SKEW * (-(-n_ticks // _SKEW) + 2)
    rows = list(ticks) + [pad] * (length - len(ticks))
    return jnp.asarray(rows, jnp.int32).T


def _moba_attention(qt4, k4, vt4):
    bsz, nb, width, blk = qt4.shape
    hn = ATT_HEADS * nb
    own = [(i, i, nb) for i in range(nb)]
    past = [(i, n, n) for i in range(nb) for n in range(i)]
    own_tab = _tick_table(own, len(own), (nb - 1, nb - 1, nb + 1))
    past_tab = _tick_table(past, len(past), (nb - 1, 0, nb + 1))
    tile_f32 = pltpu.VMEM((ATT_HEADS, blk, blk), F32)
    tile_bf16 = pltpu.VMEM((ATT_HEADS, blk, blk), BF16)
    row_f32 = pltpu.VMEM((ATT_HEADS, 1, blk), F32)
    part_f32 = pltpu.VMEM((ATT_HEADS, SUBLANES, blk), F32)
    whole = lambda b, *_: (b, 0, 0, 0)
    return pl.pallas_call(
        functools.partial(_moba_kernel, nb=nb, own_ticks=len(own), past_ticks=len(past)),
        out_shape=jax.ShapeDtypeStruct((bsz, nb, width, blk), BF16),
        grid_spec=pltpu.PrefetchScalarGridSpec(
            num_scalar_prefetch=2,
            grid=(bsz,),
            in_specs=[
                pl.BlockSpec((1, nb, width, blk), whole),
                pl.BlockSpec((1, nb, blk, width), whole),
                pl.BlockSpec((1, nb, width, blk), whole),
            ],
            out_specs=pl.BlockSpec((1, nb, width, blk), whole),
            scratch_shapes=[
                pltpu.VMEM((hn, width), BF16),
                pltpu.VMEM((hn, width), BF16),
                pltpu.VMEM((hn, width), BF16),
                pltpu.VMEM((nb, ATT_HEADS, nb + SUBLANES, blk), F32),
                pltpu.VMEM((nb, ATT_HEADS, 1, blk), F32),
                pltpu.VMEM((nb, ATT_HEADS, ATT_HEAD_DIM + 2 * SUBLANES, blk), F32),
                *([tile_f32] * _SKEW + [part_f32] * _SKEW + [tile_bf16] * _SKEW + [row_f32] * _SKEW),
            ],
        ),
        compiler_params=pltpu.CompilerParams(
            dimension_semantics=("arbitrary",), vmem_limit_bytes=VMEM_LIMIT),
        name="moba_attention",
    )(own_tab, past_tab, qt4, k4, vt4)


def _mlstm_kernel(qc_ref, kc_ref, mv_ref, og_ref, gate_ref, bias_ref, gain_ref, triu_ref, select_ref,
                  o_ref, ct_ref, m_ref):
    @pl.when(pl.program_id(1) == 0)
    def _():
        ct_ref[...] = jnp.zeros_like(ct_ref)
        m_ref[...] = jnp.zeros_like(m_ref)

    L = MLSTM_CHUNK
    for j in range(MLSTM_CHUNKS_PER_STEP):
        ts = slice(j * L, (j + 1) * L)
        _mlstm_chunk(qc_ref.at[:, ts, :], kc_ref.at[:, ts, :], mv_ref.at[:, ts, :], og_ref.at[:, ts, :],
                     gate_ref.at[:, ts], bias_ref, gain_ref, triu_ref, select_ref,
                     o_ref.at[:, ts, :], ct_ref, m_ref)


def _mlstm_chunk(qc_ref, kc_ref, mv_ref, og_ref, gate_ref, bias_ref, gain_ref, triu_ref, select_ref,
                 o_ref, ct_ref, m_ref):
    L = MLSTM_CHUNK
    hd = MLSTM_HEAD_DIM
    nh = MLSTM_HEADS

    gr = gate_ref[...] + bias_ref[...]
    r_hi, r_mid, r_lo = _split3(_log_sigmoid(gr) * LOG2_E)
    triu = triu_ref[...]
    bcum = _dot(r_hi, triu) + _dot(r_mid, triu) + _dot(r_lo, triu)
    row = lax.broadcasted_iota(jnp.int32, gr.shape, 0)
    rows_ib = jnp.where(row < nh, gr * LOG2_E, bcum)
    rows_c = rows_ib[0:nh, :] - rows_ib[nh:2 * nh, :]
    lhs_t = jnp.concatenate(list(_split3(rows_ib)) + [jnp.ones((GATE_ROWS, L), BF16)], axis=0)

    t_iota = lax.broadcasted_iota(jnp.int32, (L, LANES), 0)
    s_iota = lax.broadcasted_iota(jnp.int32, (L, LANES), 1)
    zeros_tail = jnp.zeros((GATE_ROWS, 2 * LANES), BF16)

    gain = gain_ref[...]
    heads = range(MLSTM_HEADS)
    slices = [slice(h * hd, (h + 1) * hd) for h in heads]
    srow = lax.broadcasted_iota(jnp.int32, (GATE_ROWS, L), 0)

    res_all, qk_all, inter_all = [], [], []
    for h in heads:
        c = rows_c[h:h + 1, :]
        c_hi = c.astype(BF16).astype(F32)
        c_mid = (c - c_hi).astype(BF16).astype(F32)
        c_lo = c - c_hi - c_mid
        tail = jnp.where(srow == 0, c_hi, jnp.where(srow == 1, c_mid, jnp.where(srow == 2, c_lo, 0.0)))
        sel_h = jnp.concatenate(
            [select_ref[h], jnp.concatenate([tail.astype(BF16), zeros_tail], axis=1)], axis=0)
        res_all.append(lax.dot_general(lhs_t, sel_h, (((0,), (0,)), ((), ())), preferred_element_type=F32))
        qb = qc_ref[0, :, slices[h]]
        kb = kc_ref[0, :, slices[h]]
        qk_all.append(lax.dot_general(qb, kb, (((1,), (1,)), ((), ())), preferred_element_type=F32))
        inter_all.append(_dot(qb, ct_ref[h].astype(BF16)))

    s_all, wv_all, carry_all = [], [], []
    for h in heads:
        sl = slices[h]
        res = res_all[h]
        qk = qk_all[h]
        v = mv_ref[0, :, sl]
        m_prev = m_ref[h]
        b_col = res[:, L:L + LANES]
        i_col = res[:, L + LANES:L + 2 * LANES]

        dslabs = []
        for j in range(L // LANES):
            keep = (s_iota + j * LANES) <= t_iota
            dslabs.append(jnp.where(keep, res[:, j * LANES:(j + 1) * LANES], -jnp.inf))
        dmax = jnp.max(dslabs[0], axis=-1, keepdims=True)
        for d in dslabs[1:]:
            dmax = jnp.maximum(dmax, jnp.max(d, axis=-1, keepdims=True))
        inter = b_col + m_prev
        m_t = jnp.maximum(inter, dmax)
        w_inter = jnp.exp2(inter - m_t)
        s = jnp.concatenate([qk[:, j * LANES:(j + 1) * LANES] * jnp.exp2(d - m_t)
                             for j, d in enumerate(dslabs)], axis=1)

        b_last = b_col[L - 1:L, :]
        g = b_last - b_col + i_col
        m_new = jnp.maximum(b_last + m_prev, jnp.max(g, axis=0, keepdims=True))
        decay = jnp.exp2(b_last + m_prev - m_new)
        w = jnp.exp2(g - m_new)
        m_ref[h] = m_new
        s_all.append(s.astype(BF16))
        wv_all.append(jnp.concatenate([w * v.astype(F32), w], axis=1).astype(BF16))
        carry_all.append((w_inter, jnp.exp2(-m_t), jnp.concatenate([decay, decay], axis=1)))

    ones_cols = jnp.ones((L, hd), BF16)
    intra_all = [_dot(s_all[h], jnp.concatenate([mv_ref[0, :, slices[h]], ones_cols], axis=1))
                 for h in heads]
    for h in heads:
        ct_ref[h] = carry_all[h][2] * ct_ref[h] + lax.dot_general(
            kc_ref[0, :, slices[h]], wv_all[h], (((0,), (0,)), ((), ())), preferred_element_type=F32)

    for h in heads:
        sl = slices[h]
        w_inter, floor, _ = carry_all[h]
        num = w_inter * inter_all[h][:, :hd] + intra_all[h][:, :hd]
        nq = w_inter * inter_all[h][:, hd:] + intra_all[h][:, hd:]
        hh = num / jnp.maximum(jnp.abs(nq), floor)
        mu = jnp.mean(hh, axis=-1, keepdims=True)
        var = jnp.mean(jnp.square(hh - mu), axis=-1, keepdims=True)
        hn = (hh - mu) * lax.rsqrt(var + NORM_EPS) * gain[:, sl]
        o_ref[0, :, sl] = (og_ref[0, :, sl].astype(F32) * hn).astype(o_ref.dtype)


def _mlstm(qc, kc, mv, og, gates_t, gate_bias, out_g):
    bsz, seq, width = qc.shape
    L = MLSTM_CHUNK
    step = L * MLSTM_CHUNKS_PER_STEP
    nc = seq // step
    nh = MLSTM_HEADS
    triu = jnp.triu(jnp.ones((L, L), F32)).astype(BF16)
    r_in = jnp.arange(3 * GATE_ROWS)[None, :, None] % GATE_ROWS
    col = jnp.arange(L + 2 * LANES)[None, None, :]
    head = jnp.arange(nh)[:, None, None]
    select = jnp.where(col < L + LANES, r_in == nh + head, r_in == head).astype(BF16)
    tok = lambda b, c: (b, c, 0)
    fixed = lambda b, c: (0, 0)
    wide = pl.BlockSpec((1, step, width), tok)
    return pl.pallas_call(
        _mlstm_kernel,
        out_shape=jax.ShapeDtypeStruct((bsz, seq, width), BF16),
        grid=(bsz, nc),
        in_specs=[
            wide, wide, wide, wide,
            pl.BlockSpec((GATE_ROWS, step), lambda b, c: (0, b * nc + c)),
            pl.BlockSpec((GATE_ROWS, 1), fixed),
            pl.BlockSpec(out_g.shape, fixed),
            pl.BlockSpec((L, L), fixed),
            pl.BlockSpec(select.shape, lambda b, c: (0, 0, 0)),
        ],
        out_specs=wide,
        scratch_shapes=[
            pltpu.VMEM((nh, MLSTM_HEAD_DIM, 2 * MLSTM_HEAD_DIM), F32),
            pltpu.VMEM((nh, 1, LANES), F32),
        ],
        compiler_params=pltpu.CompilerParams(
            dimension_semantics=("arbitrary", "arbitrary"), vmem_limit_bytes=VMEM_LIMIT),
        name="mlstm",
    )(qc, kc, mv, og, gates_t, gate_bias, out_g, triu, select)


def _out_kernel(x_ref, o_ref, sz_ref, ym_ref, ag_ref, w_ref, pg_ref, out_ref):
    o = jnp.concatenate([o_ref[0, j].astype(F32).T for j in range(o_ref.shape[1])], axis=0)
    ms = jnp.mean(o * o, axis=-1, keepdims=True)
    ya = (o * lax.rsqrt(ms + NORM_EPS)) * ag_ref[...] * sz_ref[...].astype(F32)
    y = _dot(ya.astype(BF16), w_ref[:ATT_WIDTH, :]) + _dot(ym_ref[...], w_ref[ATT_WIDTH:, :])
    ms2 = jnp.mean(y * y, axis=-1, keepdims=True)
    out_ref[...] = x_ref[...] + (y * lax.rsqrt(ms2 + NORM_EPS)) * pg_ref[...]


def _output_projection(x2, ot4, sz, ym, attn_g, w_out, post_g):
    n_tok, d_model = x2.shape
    tm = OUT_ROWS
    per_tile = tm // MOBA_BLOCK
    tiles_per_seq = ot4.shape[1] // per_tile
    row = lambda i: (i, 0)
    fixed = lambda i: (0, 0)
    return pl.pallas_call(
        _out_kernel,
        out_shape=jax.ShapeDtypeStruct((n_tok, d_model), F32),
        grid=(n_tok // tm,),
        in_specs=[
            pl.BlockSpec((tm, d_model), row),
            pl.BlockSpec((1, per_tile, ATT_WIDTH, MOBA_BLOCK),
                         lambda i: (i // tiles_per_seq, i % tiles_per_seq, 0, 0)),
            pl.BlockSpec((tm, ATT_WIDTH), row),
            pl.BlockSpec((tm, MLSTM_WIDTH), row),
            pl.BlockSpec((1, ATT_WIDTH), fixed),
            pl.BlockSpec(w_out.shape, fixed),
            pl.BlockSpec((1, d_model), fixed),
        ],
        out_specs=pl.BlockSpec((tm, d_model), row),
        compiler_params=pltpu.CompilerParams(
            dimension_semantics=("arbitrary",), vmem_limit_bytes=VMEM_LIMIT),
        name="output_projection",
    )(x2, ot4, sz, ym, attn_g, w_out, post_g)


def _layer(x, pre_g, w_in, i_bias, f_bias, conv_w, conv_b, attn_g, mlstm_g, w_out, post_g):
    bsz, seq, d_model = x.shape
    assert seq % MOBA_BLOCK == 0 and seq % PROJ_ROWS == 0
    assert seq % (MLSTM_CHUNK * MLSTM_CHUNKS_PER_STEP) == 0
    nb = seq // MOBA_BLOCK
    n_tok = bsz * seq
    assert seq % OUT_ROWS == 0
    n_main = 4 * ATT_WIDTH + 5 * MLSTM_WIDTH
    n_gate = 2 * MLSTM_HEADS
    assert w_in.shape == (d_model, n_main + n_gate)

    w_main = w_in[:, :n_main].astype(BF16)
    w_gate_t = jnp.pad(w_in[:, n_main:].T, ((0, GATE_ROWS - n_gate), (0, 0))).astype(BF16)
    x2 = x.reshape(n_tok, d_model)
    (qt4, k, vt4, sz, qc, kc, mv, og, gates_t) = _input_projection(
        x2, pre_g[None, :], w_main, w_gate_t, _rope_tables(seq), conv_w, conv_b[None, :], seq)

    ot4 = _moba_attention(qt4, k.reshape(bsz, nb, MOBA_BLOCK, ATT_WIDTH), vt4)

    gate_bias = jnp.pad(jnp.concatenate([i_bias, f_bias]), (0, GATE_ROWS - n_gate))[:, None]
    shp = (bsz, seq, MLSTM_WIDTH)
    ym = _mlstm(qc.reshape(shp), kc.reshape(shp), mv.reshape(shp), og.reshape(shp),
                gates_t, gate_bias, mlstm_g[None, :])

    out = _output_projection(x2, ot4, sz, ym.reshape(n_tok, MLSTM_WIDTH), attn_g[None, :],
                             w_out.astype(BF16), post_g[None, :])
    return out.reshape(bsz, seq, d_model)


def kernel(x, pre_norm_g, w_in, mlstm_i_bias, mlstm_f_bias, conv_w, conv_b, attn_out_g, mlstm_out_g,
           w_out, post_norm_g):
    for l in range(pre_norm_g.shape[0]):
        x = _layer(x, pre_norm_g[l], w_in[l], mlstm_i_bias[l], mlstm_f_bias[l], conv_w[l], conv_b[l],
                   attn_out_g[l], mlstm_out_g[l], w_out[l], post_norm_g[l])
    return x
```

```python
import functools

import jax
import jax.numpy as jnp
from jax import lax
from jax.experimental import pallas as pl
from jax.experimental.pallas import tpu as pltpu

F32 = jnp.float32
BF16 = jnp.bfloat16

ATT_HEADS = 8
ATT_HEAD_DIM = 64
ATT_WIDTH = ATT_HEADS * ATT_HEAD_DIM
ROT_DIM = ATT_HEAD_DIM // 4
ROPE_THETA = 500000.0
MOBA_BLOCK = 256
MOBA_TOPK = 3
MLSTM_HEADS = 4
MLSTM_HEAD_DIM = 128
MLSTM_WIDTH = MLSTM_HEADS * MLSTM_HEAD_DIM
CONV_WIDTH = 4
NORM_EPS = 1e-6

LANES = 128
SUBLANES = 8
GATE_ROWS = 16
PROJ_ROWS = 512
PROJ_EPILOGUE_ROWS = 32
OUT_ROWS = 1024
MLSTM_CHUNK = 256
MLSTM_CHUNKS_PER_STEP = 2
NEG_BIG = -1e30
LOG2_E = 1.4426950408889634
_SKEW = 2
VMEM_LIMIT = 56 * 1024 * 1024


def _sigmoid(x):
    return 1.0 / (1.0 + jnp.exp(-x))


def _silu(x):
    return x * _sigmoid(x)


def _log_sigmoid(x):
    return jnp.minimum(x, 0.0) - jnp.log(1.0 + jnp.exp(-jnp.abs(x)))


def _split3(x):
    hi = x.astype(BF16)
    r1 = x - hi.astype(F32)
    mid = r1.astype(BF16)
    lo = (r1 - mid.astype(F32)).astype(BF16)
    return hi, mid, lo


def _dot(a, b):
    return jnp.dot(a, b, preferred_element_type=F32)


def _proj_kernel(x_ref, g_ref, w_ref, wg_ref, cos_ref, sa_ref, sb_ref, cw_ref, cb_ref,
                 qt_ref, k_ref, vt_ref, sz_ref, qc_ref, kc_ref, mv_ref, og_ref, gate_ref,
                 pq_ref, pk_ref, acc0_ref, acc1_ref, gt0_ref, gt1_ref, *, seq_tiles, n_tiles):
    s = pl.program_id(0)
    first_of_seq = ((s - 1) % seq_tiles) == 0
    tm = x_ref.shape[0]
    halo = SUBLANES
    n_groups = w_ref.shape[1] // ATT_WIDTH

    def matmul_parts(acc_ref, gt_ref):
        x = x_ref[...]
        ms = jnp.mean(x * x, axis=-1, keepdims=True)
        hb = ((x * lax.rsqrt(ms + NORM_EPS)) * g_ref[...]).astype(BF16)

        def group(gi):
            acc_ref[gi] = _dot(hb, w_ref[:, gi * ATT_WIDTH:(gi + 1) * ATT_WIDTH])

        def gates():
            gt_ref[...] = lax.dot_general(wg_ref[...], hb, (((0,), (1,)), ((), ())),
                                          preferred_element_type=F32)

        return [functools.partial(group, gi) for gi in range(n_groups)] + [gates]

    def rope(acc, rs):
        cos = cos_ref[rs, :]
        sa = sa_ref[rs, :]
        sb = sb_ref[rs, :]
        parts = []
        for c in range(ATT_WIDTH // LANES):
            xs = acc[:, c * LANES:(c + 1) * LANES]
            parts.append(xs * cos
                         + pltpu.roll(xs, LANES - ROT_DIM // 2, 1) * sa
                         + pltpu.roll(xs, ROT_DIM // 2, 1) * sb)
        return jnp.concatenate(parts, axis=1)

    def conv_silu(acc_ref, gi, r, prev_ref, w, b):
        rows = PROJ_EPILOGUE_ROWS
        if r == 0:
            prev = jnp.where(first_of_seq, 0.0, prev_ref[...])
        else:
            prev = acc_ref[gi, r - halo:r, :]
        ext = jnp.concatenate([prev, acc_ref[gi, r:r + rows, :]], axis=0)
        out = b
        for j in range(CONV_WIDTH):
            off = halo - (CONV_WIDTH - 1) + j
            out = out + w[j:j + 1, :] * ext[off:off + rows, :]
        return _silu(out)

    def epilogue_parts(acc_ref, gt_ref):
        cw = cw_ref[...]
        cb = cb_ref[...]

        def chunk(r):
            rs = slice(r, r + PROJ_EPILOGUE_ROWS)
            acc_ref[0, rs, :] = rope(acc_ref[0, rs, :], rs) * (LOG2_E * ATT_HEAD_DIM ** -0.5)
            k_ref[rs, :] = rope(acc_ref[1, rs, :], rs).astype(k_ref.dtype)
            sz_ref[rs, :] = _silu(acc_ref[3, rs, :]).astype(sz_ref.dtype)
            qc = conv_silu(acc_ref, 4, r, pq_ref, cw[:, :MLSTM_WIDTH], cb[:, :MLSTM_WIDTH])
            qc_ref[rs, :] = qc.astype(qc_ref.dtype)
            kc = conv_silu(acc_ref, 5, r, pk_ref, cw[:, MLSTM_WIDTH:], cb[:, MLSTM_WIDTH:])
            kc_ref[rs, :] = (kc * (MLSTM_HEAD_DIM ** -0.5)).astype(kc_ref.dtype)
            mv_ref[rs, :] = acc_ref[6, rs, :].astype(mv_ref.dtype)
            og_ref[rs, :] = (_sigmoid(acc_ref[7, rs, :]) * _silu(acc_ref[8, rs, :])).astype(og_ref.dtype)

        def tail():
            pq_ref[...] = acc_ref[4, tm - halo:, :]
            pk_ref[...] = acc_ref[5, tm - halo:, :]
            gate_ref[...] = gt_ref[...]

        def key_block(j):
            rows = slice(j * MOBA_BLOCK, (j + 1) * MOBA_BLOCK)
            qt_ref[0, j] = acc_ref[0, rows, :].T.astype(qt_ref.dtype)
            vt_ref[0, j] = acc_ref[2, rows, :].T.astype(vt_ref.dtype)

        parts = []
        for r in range(0, tm, PROJ_EPILOGUE_ROWS):
            parts.append(functools.partial(chunk, r))
            if (r + PROJ_EPILOGUE_ROWS) % MOBA_BLOCK == 0:
                parts.append(functools.partial(key_block, r // MOBA_BLOCK))
        return parts + [tail]

    def step(mat_bufs, epi_bufs):
        mats = matmul_parts(*mat_bufs) if mat_bufs is not None else []
        epis = epilogue_parts(*epi_bufs) if epi_bufs is not None else []
        per_mat = -(-len(epis) // max(len(mats), 1))
        for part in mats:
            part()
            for e in epis[:per_mat]:
                e()
            epis = epis[per_mat:]
        for e in epis:
            e()

    bufs = ((acc0_ref, gt0_ref), (acc1_ref, gt1_ref))

    @pl.when(s == 0)
    def _():
        step(bufs[0], None)

    for parity in range(2):
        @pl.when((s > 0) & (s < n_tiles) & (s % 2 == parity))
        def _():
            step(bufs[parity], bufs[1 - parity])

    @pl.when(s == n_tiles)
    def _():
        step(None, bufs[(n_tiles - 1) % 2])


def _rope_tables(seq):
    half = ROT_DIM // 2
    inv_freq = jnp.power(ROPE_THETA, -jnp.arange(half, dtype=F32) * 2.0 / ROT_DIM)
    ang = jnp.arange(seq, dtype=jnp.int32).astype(F32)[:, None] * inv_freq[None, :]
    cos = jnp.cos(ang)
    sin = jnp.sin(ang)
    ones = jnp.ones((seq, ATT_HEAD_DIM - ROT_DIM), F32)
    zeros = jnp.zeros((seq, ATT_HEAD_DIM - ROT_DIM), F32)
    zh = jnp.zeros((seq, half), F32)
    cos_h = jnp.concatenate([cos, cos, ones], axis=1)
    sa_h = jnp.concatenate([-sin, zh, zeros], axis=1)
    sb_h = jnp.concatenate([zh, sin, zeros], axis=1)
    rep = LANES // ATT_HEAD_DIM
    return (jnp.tile(cos_h, (1, rep)), jnp.tile(sa_h, (1, rep)), jnp.tile(sb_h, (1, rep)))


def _input_projection(x2, pre_g, w_all, n_main, w_gate, tables, conv_w, conv_b, seq):
    n_tok, d_model = x2.shape
    tm = PROJ_ROWS
    seq_tiles = seq // tm
    n_tiles = n_tok // tm
    n_groups = n_main // ATT_WIDTH
    row_in = lambda s: (jnp.minimum(s, n_tiles - 1), 0)
    row_out = lambda s: (jnp.maximum(s - 1, 0), 0)
    fixed = lambda s: (0, 0)
    tab = lambda s: (jnp.maximum(s - 1, 0) % seq_tiles, 0)
    done = lambda s: jnp.maximum(s - 1, 0)
    wide = jax.ShapeDtypeStruct((n_tok, ATT_WIDTH), BF16)
    wide_spec = pl.BlockSpec((tm, ATT_WIDTH), row_out)
    blocked = jax.ShapeDtypeStruct((n_tok // seq, seq // MOBA_BLOCK, ATT_WIDTH, MOBA_BLOCK), BF16)
    blocked_spec = pl.BlockSpec((1, tm // MOBA_BLOCK, ATT_WIDTH, MOBA_BLOCK),
                                lambda s: (done(s) // seq_tiles, done(s) % seq_tiles, 0, 0))
    return pl.pallas_call(
        functools.partial(_proj_kernel, seq_tiles=seq_tiles, n_tiles=n_tiles),
        out_shape=(blocked, wide, blocked) + (wide,) * 5 + (jax.ShapeDtypeStruct((GATE_ROWS, n_tok), F32),),
        grid=(n_tiles + 1,),
        in_specs=[
            pl.BlockSpec((tm, d_model), row_in),
            pl.BlockSpec((1, d_model), fixed),
            pl.BlockSpec((d_model, n_main), fixed),
            pl.BlockSpec(w_gate.shape, fixed),
            pl.BlockSpec((tm, LANES), tab),
            pl.BlockSpec((tm, LANES), tab),
            pl.BlockSpec((tm, LANES), tab),
            pl.BlockSpec(conv_w.shape, fixed),
            pl.BlockSpec(conv_b.shape, fixed),
        ],
        out_specs=(blocked_spec, wide_spec, blocked_spec) + (wide_spec,) * 5
        + (pl.BlockSpec((GATE_ROWS, tm), lambda s: (0, done(s))),),
        scratch_shapes=[
            pltpu.VMEM((SUBLANES, MLSTM_WIDTH), F32),
            pltpu.VMEM((SUBLANES, MLSTM_WIDTH), F32),
            pltpu.VMEM((n_groups, tm, ATT_WIDTH), F32),
            pltpu.VMEM((n_groups, tm, ATT_WIDTH), F32),
            pltpu.VMEM((GATE_ROWS, tm), F32),
            pltpu.VMEM((GATE_ROWS, tm), F32),
        ],
        compiler_params=pltpu.CompilerParams(
            dimension_semantics=("arbitrary",), vmem_limit_bytes=VMEM_LIMIT),
        name="input_projection",
    )(x2, pre_g, w_all, w_gate, *tables, conv_w, conv_b)


def _moba_kernel(own_tab, past_tab, qt_ref, k_ref, vt_ref, o_ref,
                 kst_hi, kst_mid, kst_lo, bias_ref, m_ref, acc_ref,
                 *stage_bufs, nb, own_ticks, past_ticks):
    blk = MOBA_BLOCK
    hd = ATT_HEAD_DIM

    lane_head = lax.broadcasted_iota(jnp.int32, (nb, ATT_WIDTH), 1) // hd
    means = []
    for n in range(nb):
        means.append(jnp.mean(k_ref[0, n].astype(F32), axis=0, keepdims=True))
    km = jnp.concatenate(means, axis=0)
    for h in range(ATT_HEADS):
        hi, mid, lo = _split3(jnp.where(lane_head == h, km, 0.0))
        kst_hi[h * nb:(h + 1) * nb, :] = hi
        kst_mid[h * nb:(h + 1) * nb, :] = mid
        kst_lo[h * nb:(h + 1) * nb, :] = lo

    n_iota = lax.broadcasted_iota(jnp.int32, (nb, blk), 0)
    n_iota_f = n_iota.astype(F32)
    tail_iota = lax.broadcasted_iota(jnp.int32, (SUBLANES, blk), 0)
    tail_rows = jnp.where(tail_iota == 0, 0.0, NEG_BIG)

    def prepare(i, carry):
        qt = qt_ref[0, i]
        gates = _dot(kst_hi[...], qt) + _dot(kst_mid[...], qt) + _dot(kst_lo[...], qt)
        past = n_iota < i
        for h in range(ATT_HEADS):
            g = jnp.where(past, gates[h * nb:(h + 1) * nb, :], -jnp.inf)
            sel = jnp.zeros((nb, blk), jnp.bool_)
            for _ in range(min(MOBA_TOPK, nb)):
                gmax = jnp.max(g, axis=0, keepdims=True)
                first = jnp.min(jnp.where(g == gmax, n_iota_f, float(nb)), axis=0, keepdims=True)
                hit = n_iota_f == first
                sel = sel | hit
                g = jnp.where(hit, -jnp.inf, g)
            bias_ref[i, h, 0:nb, :] = jnp.where(sel & past, 0.0, NEG_BIG)
            bias_ref[i, h, nb:nb + SUBLANES, :] = tail_rows
            m_ref[i, h] = jnp.full((1, blk), NEG_BIG, F32)
            acc_ref[i, h] = jnp.zeros(acc_ref.shape[2:], F32)
        return carry

    lax.fori_loop(0, nb, prepare, 0)

    l_iota = lax.broadcasted_iota(jnp.int32, (blk, blk), 0)
    q_iota = lax.broadcasted_iota(jnp.int32, (blk, blk), 1)
    causal = l_iota <= q_iota
    row_head = lax.broadcasted_iota(jnp.int32, (LANES, blk), 0) // hd
    ones_rows = jnp.ones((2 * SUBLANES, blk), BF16)
    s_bufs, c_bufs, p_bufs, a_bufs = (stage_bufs[j * _SKEW:(j + 1) * _SKEW] for j in range(4))

    def run(tab, n_ticks, own):
        def scores(tick, slot):
            qi = tab[0, tick]
            kb = k_ref[0, tab[1, tick]]
            for h in range(ATT_HEADS):
                pr = slice((h // 2) * LANES, (h // 2 + 1) * LANES)
                qpair = qt_ref[0, qi, pr, :]
                qh = jnp.where(row_head == (h % 2), qpair, jnp.zeros_like(qpair))
                st = _dot(kb[:, pr], qh)
                st = jnp.where(causal, st, NEG_BIG) if own else st
                s_bufs[slot][h] = st
                c_bufs[slot][h] = jnp.max(st.reshape(blk // SUBLANES, SUBLANES, blk), axis=0)

        def softmax_update(tick, slot):
            qi = tab[0, tick]
            mask_row = tab[2, tick]
            for h in range(ATT_HEADS):
                st = s_bufs[slot][h]
                b = bias_ref[qi, h, pl.ds(mask_row, 1), :]
                m_old = m_ref[qi, h]
                m_new = jnp.maximum(m_old, jnp.max(c_bufs[slot][h], axis=0, keepdims=True) + b)
                p_bufs[slot][h] = jnp.exp2(st - (m_new - b)).astype(BF16)
                a_bufs[slot][h] = jnp.exp2(m_old - m_new)
                m_ref[qi, h] = m_new

        def values(tick, slot):
            qi = tab[0, tick]
            vtb = vt_ref[0, tab[1, tick]]
            for h in range(ATT_HEADS):
                lhs = jnp.concatenate([vtb[h * hd:(h + 1) * hd, :], ones_rows], axis=0)
                acc_ref[qi, h] = a_bufs[slot][h] * acc_ref[qi, h] + _dot(lhs, p_bufs[slot][h])

        for t in range(_SKEW):
            scores(t, t % _SKEW)
        for t in range(_SKEW, 2 * _SKEW):
            softmax_update(t - _SKEW, t % _SKEW)
            scores(t, t % _SKEW)

        def body(u, carry):
            t = _SKEW * (u + 2)
            for d in range(_SKEW):
                values(t + d - 2 * _SKEW, d)
                softmax_update(t + d - _SKEW, d)
                scores(t + d, d)
            return carry

        lax.fori_loop(0, -(-n_ticks // _SKEW), body, 0)

    run(own_tab, own_ticks, True)
    run(past_tab, past_ticks, False)

    def finalize(i, carry):
        for h in range(ATT_HEADS):
            acc = acc_ref[i, h]
            o_ref[0, i, h * hd:(h + 1) * hd, :] = (acc[:hd, :] / acc[hd:hd + 1, :]).astype(o_ref.dtype)
        return carry

    lax.fori_loop(0, nb, finalize, 0)


def _tick_table(ticks, n_ticks, pad):
    length = _SKEW * (-(-n_ticks // _SKEW) + 2)
    rows = list(ticks) + [pad] * (length - len(ticks))
    return jnp.asarray(rows, jnp.int32).T


def _moba_attention(qt4, k4, vt4):
    bsz, nb, width, blk = qt4.shape
    hn = ATT_HEADS * nb
    own = [(i, i, nb) for i in range(nb)]
    past = [(i, n, n) for i in range(nb) for n in range(i)]
    own_tab = _tick_table(own, len(own), (nb - 1, nb - 1, nb + 1))
    past_tab = _tick_table(past, len(past), (nb - 1, 0, nb + 1))
    tile_f32 = pltpu.VMEM((ATT_HEADS, blk, blk), F32)
    tile_bf16 = pltpu.VMEM((ATT_HEADS, blk, blk), BF16)
    row_f32 = pltpu.VMEM((ATT_HEADS, 1, blk), F32)
    part_f32 = pltpu.VMEM((ATT_HEADS, SUBLANES, blk), F32)
    whole = lambda b, *_: (b, 0, 0, 0)
    return pl.pallas_call(
        functools.partial(_moba_kernel, nb=nb, own_ticks=len(own), past_ticks=len(past)),
        out_shape=jax.ShapeDtypeStruct((bsz, nb, width, blk), BF16),
        grid_spec=pltpu.PrefetchScalarGridSpec(
            num_scalar_prefetch=2,
            grid=(bsz,),
            in_specs=[
                pl.BlockSpec((1, nb, width, blk), whole),
                pl.BlockSpec((1, nb, blk, width), whole),
                pl.BlockSpec((1, nb, width, blk), whole),
            ],
            out_specs=pl.BlockSpec((1, nb, width, blk), whole),
            scratch_shapes=[
                pltpu.VMEM((hn, width), BF16),
                pltpu.VMEM((hn, width), BF16),
                pltpu.VMEM((hn, width), BF16),
                pltpu.VMEM((nb, ATT_HEADS, nb + SUBLANES, blk), F32),
                pltpu.VMEM((nb, ATT_HEADS, 1, blk), F32),
                pltpu.VMEM((nb, ATT_HEADS, ATT_HEAD_DIM + 2 * SUBLANES, blk), F32),
                *([tile_f32] * _SKEW + [part_f32] * _SKEW + [tile_bf16] * _SKEW + [row_f32] * _SKEW),
            ],
        ),
        compiler_params=pltpu.CompilerParams(
            dimension_semantics=("arbitrary",), vmem_limit_bytes=VMEM_LIMIT),
        name="moba_attention",
    )(own_tab, past_tab, qt4, k4, vt4)


def _mlstm_kernel(qc_ref, kc_ref, mv_ref, og_ref, gate_ref, bias_ref, gain_ref, triu_ref, select_ref,
                  o_ref, ct_ref, m_ref):
    @pl.when(pl.program_id(1) == 0)
    def _():
        ct_ref[...] = jnp.zeros_like(ct_ref)
        m_ref[...] = jnp.zeros_like(m_ref)

    L = MLSTM_CHUNK
    for j in range(MLSTM_CHUNKS_PER_STEP):
        ts = slice(j * L, (j + 1) * L)
        _mlstm_chunk(qc_ref.at[:, ts, :], kc_ref.at[:, ts, :], mv_ref.at[:, ts, :], og_ref.at[:, ts, :],
                     gate_ref.at[:, ts], bias_ref, gain_ref, triu_ref, select_ref,
                     o_ref.at[:, ts, :], ct_ref, m_ref)


def _mlstm_chunk(qc_ref, kc_ref, mv_ref, og_ref, gate_ref, bias_ref, gain_ref, triu_ref, select_ref,
                 o_ref, ct_ref, m_ref):
    L = MLSTM_CHUNK
    hd = MLSTM_HEAD_DIM
    nh = MLSTM_HEADS

    gr = gate_ref[...] + bias_ref[...]
    r_hi, r_mid, r_lo = _split3(_log_sigmoid(gr) * LOG2_E)
    triu = triu_ref[...]
    bcum = _dot(r_hi, triu) + _dot(r_mid, triu) + _dot(r_lo, triu)
    row = lax.broadcasted_iota(jnp.int32, gr.shape, 0)
    rows_ib = jnp.where(row < nh, gr * LOG2_E, bcum)
    rows_c = rows_ib[0:nh, :] - rows_ib[nh:2 * nh, :]
    lhs_t = jnp.concatenate(list(_split3(rows_ib)) + [jnp.ones((GATE_ROWS, L), BF16)], axis=0)

    t_iota = lax.broadcasted_iota(jnp.int32, (L, LANES), 0)
    s_iota = lax.broadcasted_iota(jnp.int32, (L, LANES), 1)
    zeros_tail = jnp.zeros((GATE_ROWS, 2 * LANES), BF16)

    gain = gain_ref[...]
    heads = range(MLSTM_HEADS)
    slices = [slice(h * hd, (h + 1) * hd) for h in heads]
    srow = lax.broadcasted_iota(jnp.int32, (GATE_ROWS, L), 0)

    res_all, qk_all, inter_all = [], [], []
    for h in heads:
        c = rows_c[h:h + 1, :]
        c_hi = c.astype(BF16).astype(F32)
        c_mid = (c - c_hi).astype(BF16).astype(F32)
        c_lo = c - c_hi - c_mid
        tail = jnp.where(srow == 0, c_hi, jnp.where(srow == 1, c_mid, jnp.where(srow == 2, c_lo, 0.0)))
        sel_h = jnp.concatenate(
            [select_ref[h], jnp.concatenate([tail.astype(BF16), zeros_tail], axis=1)], axis=0)
        res_all.append(lax.dot_general(lhs_t, sel_h, (((0,), (0,)), ((), ())), preferred_element_type=F32))
        qb = qc_ref[0, :, slices[h]]
        kb = kc_ref[0, :, slices[h]]
        qk_all.append(lax.dot_general(qb, kb, (((1,), (1,)), ((), ())), preferred_element_type=F32))
        inter_all.append(_dot(qb, ct_ref[h].astype(BF16)))

    s_all, wv_all, carry_all = [], [], []
    for h in heads:
        sl = slices[h]
        res = res_all[h]
        qk = qk_all[h]
        v = mv_ref[0, :, sl]
        m_prev = m_ref[h]
        b_col = res[:, L:L + LANES]
        i_col = res[:, L + LANES:L + 2 * LANES]

        dslabs = []
        for j in range(L // LANES):
            keep = (s_iota + j * LANES) <= t_iota
            dslabs.append(jnp.where(keep, res[:, j * LANES:(j + 1) * LANES], -jnp.inf))
        dmax = jnp.max(dslabs[0], axis=-1, keepdims=True)
        for d in dslabs[1:]:
            dmax = jnp.maximum(dmax, jnp.max(d, axis=-1, keepdims=True))
        inter = b_col + m_prev
        m_t = jnp.maximum(inter, dmax)
        w_inter = jnp.exp2(inter - m_t)
        s = jnp.concatenate([qk[:, j * LANES:(j + 1) * LANES] * jnp.exp2(d - m_t)
                             for j, d in enumerate(dslabs)], axis=1)

        b_last = b_col[L - 1:L, :]
        g = b_last - b_col + i_col
        m_new = jnp.maximum(b_last + m_prev, jnp.max(g, axis=0, keepdims=True))
        decay = jnp.exp2(b_last + m_prev - m_new)
        w = jnp.exp2(g - m_new)
        m_ref[h] = m_new
        s_all.append(s.astype(BF16))
        wv_all.append(jnp.concatenate([w * v.astype(F32), w], axis=1).astype(BF16))
        carry_all.append((w_inter, jnp.exp2(-m_t), jnp.concatenate([decay, decay], axis=1)))

    ones_cols = jnp.ones((L, hd), BF16)
    intra_all = [_dot(s_all[h], jnp.concatenate([mv_ref[0, :, slices[h]], ones_cols], axis=1))
                 for h in heads]
    for h in heads:
        ct_ref[h] = carry_all[h][2] * ct_ref[h] + lax.dot_general(
            kc_ref[0, :, slices[h]], wv_all[h], (((0,), (0,)), ((), ())), preferred_element_type=F32)

    for h in heads:
        sl = slices[h]
        w_inter, floor, _ = carry_all[h]
        num = w_inter * inter_all[h][:, :hd] + intra_all[h][:, :hd]
        nq = w_inter * inter_all[h][:, hd:] + intra_all[h][:, hd:]
        hh = num / jnp.maximum(jnp.abs(nq), floor)
        mu = jnp.mean(hh, axis=-1, keepdims=True)
        var = jnp.mean(jnp.square(hh - mu), axis=-1, keepdims=True)
        hn = (hh - mu) * lax.rsqrt(var + NORM_EPS) * gain[:, sl]
        o_ref[0, :, sl] = (og_ref[0, :, sl].astype(F32) * hn).astype(o_ref.dtype)


def _mlstm(qc, kc, mv, og, gates_t, gate_bias, out_g):
    bsz, seq, width = qc.shape
    L = MLSTM_CHUNK
    step = L * MLSTM_CHUNKS_PER_STEP
    nc = seq // step
    nh = MLSTM_HEADS
    triu = jnp.triu(jnp.ones((L, L), F32)).astype(BF16)
    r_in = jnp.arange(3 * GATE_ROWS)[None, :, None] % GATE_ROWS
    col = jnp.arange(L + 2 * LANES)[None, None, :]
    head = jnp.arange(nh)[:, None, None]
    select = jnp.where(col < L + LANES, r_in == nh + head, r_in == head).astype(BF16)
    tok = lambda b, c: (b, c, 0)
    fixed = lambda b, c: (0, 0)
    wide = pl.BlockSpec((1, step, width), tok)
    return pl.pallas_call(
        _mlstm_kernel,
        out_shape=jax.ShapeDtypeStruct((bsz, seq, width), BF16),
        grid=(bsz, nc),
        in_specs=[
            wide, wide, wide, wide,
            pl.BlockSpec((GATE_ROWS, step), lambda b, c: (0, b * nc + c)),
            pl.BlockSpec((GATE_ROWS, 1), fixed),
            pl.BlockSpec(out_g.shape, fixed),
            pl.BlockSpec((L, L), fixed),
            pl.BlockSpec(select.shape, lambda b, c: (0, 0, 0)),
        ],
        out_specs=wide,
        scratch_shapes=[
            pltpu.VMEM((nh, MLSTM_HEAD_DIM, 2 * MLSTM_HEAD_DIM), F32),
            pltpu.VMEM((nh, 1, LANES), F32),
        ],
        compiler_params=pltpu.CompilerParams(
            dimension_semantics=("arbitrary", "arbitrary"), vmem_limit_bytes=VMEM_LIMIT),
        name="mlstm",
    )(qc, kc, mv, og, gates_t, gate_bias, out_g, triu, select)


def _out_kernel(x_ref, o_ref, sz_ref, ym_ref, ag_ref, w_ref, pg_ref, out_ref):
    o = jnp.concatenate([o_ref[0, j].astype(F32).T for j in range(o_ref.shape[1])], axis=0)
    ms = jnp.mean(o * o, axis=-1, keepdims=True)
    ya = (o * lax.rsqrt(ms + NORM_EPS)) * ag_ref[...] * sz_ref[...].astype(F32)
    y = _dot(ya.astype(BF16), w_ref[:ATT_WIDTH, :]) + _dot(ym_ref[...], w_ref[ATT_WIDTH:, :])
    ms2 = jnp.mean(y * y, axis=-1, keepdims=True)
    out_ref[...] = x_ref[...] + (y * lax.rsqrt(ms2 + NORM_EPS)) * pg_ref[...]


def _output_projection(x2, ot4, sz, ym, attn_g, w_out, post_g):
    n_tok, d_model = x2.shape
    tm = OUT_ROWS
    per_tile = tm // MOBA_BLOCK
    tiles_per_seq = ot4.shape[1] // per_tile
    row = lambda i: (i, 0)
    fixed = lambda i: (0, 0)
    return pl.pallas_call(
        _out_kernel,
        out_shape=jax.ShapeDtypeStruct((n_tok, d_model), F32),
        grid=(n_tok // tm,),
        in_specs=[
            pl.BlockSpec((tm, d_model), row),
            pl.BlockSpec((1, per_tile, ATT_WIDTH, MOBA_BLOCK),
                         lambda i: (i // tiles_per_seq, i % tiles_per_seq, 0, 0)),
            pl.BlockSpec((tm, ATT_WIDTH), row),
            pl.BlockSpec((tm, MLSTM_WIDTH), row),
            pl.BlockSpec((1, ATT_WIDTH), fixed),
            pl.BlockSpec(w_out.shape, fixed),
            pl.BlockSpec((1, d_model), fixed),
        ],
        out_specs=pl.BlockSpec((tm, d_model), row),
        compiler_params=pltpu.CompilerParams(
            dimension_semantics=("arbitrary",), vmem_limit_bytes=VMEM_LIMIT),
        name="output_projection",
    )(x2, ot4, sz, ym, attn_g, w_out, post_g)


def _layer(x, pre_g, w_in, i_bias, f_bias, conv_w, conv_b, attn_g, mlstm_g, w_out, post_g):
    bsz, seq, d_model = x.shape
    assert seq % MOBA_BLOCK == 0 and seq % PROJ_ROWS == 0
    assert seq % (MLSTM_CHUNK * MLSTM_CHUNKS_PER_STEP) == 0
    nb = seq // MOBA_BLOCK
    n_tok = bsz * seq
    assert seq % OUT_ROWS == 0
    n_main = 4 * ATT_WIDTH + 5 * MLSTM_WIDTH
    n_gate = 2 * MLSTM_HEADS
    assert w_in.shape == (d_model, n_main + n_gate)

    w_all = w_in.astype(BF16)
    w_gate = jnp.pad(w_all[:, n_main:], ((0, 0), (0, GATE_ROWS - n_gate)))
    x2 = x.reshape(n_tok, d_model)
    (qt4, k, vt4, sz, qc, kc, mv, og, gates_t) = _input_projection(
        x2, pre_g[None, :], w_all, n_main, w_gate, _rope_tables(seq), conv_w, conv_b[None, :], seq)

    ot4 = _moba_attention(qt4, k.reshape(bsz, nb, MOBA_BLOCK, ATT_WIDTH), vt4)

    gate_bias = jnp.pad(jnp.concatenate([i_bias, f_bias]), (0, GATE_ROWS - n_gate))[:, None]
    shp = (bsz, seq, MLSTM_WIDTH)
    ym = _mlstm(qc.reshape(shp), kc.reshape(shp), mv.reshape(shp), og.reshape(shp),
                gates_t, gate_bias, mlstm_g[None, :])

    out = _output_projection(x2, ot4, sz, ym.reshape(n_tok, MLSTM_WIDTH), attn_g[None, :],
                             w_out.astype(BF16), post_g[None, :])
    return out.reshape(bsz, seq, d_model)


def kernel(x, pre_norm_g, w_in, mlstm_i_bias, mlstm_f_bias, conv_w, conv_b, attn_out_g, mlstm_out_g,
           w_out, post_norm_g):
    for l in range(pre_norm_g.shape[0]):
        x = _layer(x, pre_norm_g[l], w_in[l], mlstm_i_bias[l], mlstm_f_bias[l], conv_w[l], conv_b[l],
                   attn_out_g[l], mlstm_out_g[l], w_out[l], post_norm_g[l])
    return x
```

```python
import functools

import jax
import jax.numpy as jnp
from jax import lax
from jax.experimental import pallas as pl
from jax.experimental.pallas import tpu as pltpu

F32 = jnp.float32
BF16 = jnp.bfloat16

ATT_HEADS = 8
ATT_HEAD_DIM = 64
ATT_WIDTH = ATT_HEADS * ATT_HEAD_DIM
ROT_DIM = ATT_HEAD_DIM // 4
ROPE_THETA = 500000.0
MOBA_BLOCK = 256
MOBA_TOPK = 3
MLSTM_HEADS = 4
MLSTM_HEAD_DIM = 128
MLSTM_WIDTH = MLSTM_HEADS * MLSTM_HEAD_DIM
CONV_WIDTH = 4
NORM_EPS = 1e-6

LANES = 128
SUBLANES = 8
GATE_ROWS = 16
PROJ_ROWS = 512
PROJ_EPILOGUE_ROWS = 32
OUT_ROWS = 1024
MLSTM_CHUNK = 256
MLSTM_CHUNKS_PER_STEP = 4
NEG_BIG = -1e30
LOG2_E = 1.4426950408889634
_SKEW = 2
VMEM_LIMIT = 56 * 1024 * 1024


def _sigmoid(x):
    return 1.0 / (1.0 + jnp.exp(-x))


def _silu(x):
    return x * _sigmoid(x)


def _log_sigmoid(x):
    return jnp.minimum(x, 0.0) - jnp.log(1.0 + jnp.exp(-jnp.abs(x)))


def _split3(x):
    hi = x.astype(BF16)
    r1 = x - hi.astype(F32)
    mid = r1.astype(BF16)
    lo = (r1 - mid.astype(F32)).astype(BF16)
    return hi, mid, lo


def _dot(a, b):
    return jnp.dot(a, b, preferred_element_type=F32)


def _proj_kernel(x_ref, g_ref, w_ref, wg_ref, cos_ref, sa_ref, sb_ref, cw_ref, cb_ref,
                 qt_ref, k_ref, vt_ref, sz_ref, qc_ref, kc_ref, mv_ref, og_ref, gate_ref,
                 pq_ref, pk_ref, acc0_ref, acc1_ref, gt0_ref, gt1_ref, *, seq_tiles, n_tiles):
    s = pl.program_id(0)
    first_of_seq = ((s - 1) % seq_tiles) == 0
    tm = x_ref.shape[0]
    halo = SUBLANES
    n_groups = w_ref.shape[1] // ATT_WIDTH

    def matmul_parts(acc_ref, gt_ref):
        x = x_ref[...]
        ms = jnp.mean(x * x, axis=-1, keepdims=True)
        hb = ((x * lax.rsqrt(ms + NORM_EPS)) * g_ref[...]).astype(BF16)

        def group(gi):
            acc_ref[gi] = _dot(hb, w_ref[:, gi * ATT_WIDTH:(gi + 1) * ATT_WIDTH])

        def gates():
            gt_ref[...] = lax.dot_general(wg_ref[...], hb, (((0,), (1,)), ((), ())),
                                          preferred_element_type=F32)

        return [functools.partial(group, gi) for gi in range(n_groups)] + [gates]

    def rope(acc, rs):
        cos = cos_ref[rs, :]
        sa = sa_ref[rs, :]
        sb = sb_ref[rs, :]
        parts = []
        for c in range(ATT_WIDTH // LANES):
            xs = acc[:, c * LANES:(c + 1) * LANES]
            parts.append(xs * cos
                         + pltpu.roll(xs, LANES - ROT_DIM // 2, 1) * sa
                         + pltpu.roll(xs, ROT_DIM // 2, 1) * sb)
        return jnp.concatenate(parts, axis=1)

    def conv_silu(acc_ref, gi, r, prev_ref, w, b):
        rows = PROJ_EPILOGUE_ROWS
        if r == 0:
            prev = jnp.where(first_of_seq, 0.0, prev_ref[...])
        else:
            prev = acc_ref[gi, r - halo:r, :]
        ext = jnp.concatenate([prev, acc_ref[gi, r:r + rows, :]], axis=0)
        out = b
        for j in range(CONV_WIDTH):
            off = halo - (CONV_WIDTH - 1) + j
            out = out + w[j:j + 1, :] * ext[off:off + rows, :]
        return _silu(out)

    def epilogue_parts(acc_ref, gt_ref):
        cw = cw_ref[...]
        cb = cb_ref[...]

        def chunk(r):
            rs = slice(r, r + PROJ_EPILOGUE_ROWS)
            acc_ref[0, rs, :] = rope(acc_ref[0, rs, :], rs) * (LOG2_E * ATT_HEAD_DIM ** -0.5)
            k_ref[rs, :] = rope(acc_ref[1, rs, :], rs).astype(k_ref.dtype)
            sz_ref[rs, :] = _silu(acc_ref[3, rs, :]).astype(sz_ref.dtype)
            qc = conv_silu(acc_ref, 4, r, pq_ref, cw[:, :MLSTM_WIDTH], cb[:, :MLSTM_WIDTH])
            qc_ref[rs, :] = qc.astype(qc_ref.dtype)
            kc = conv_silu(acc_ref, 5, r, pk_ref, cw[:, MLSTM_WIDTH:], cb[:, MLSTM_WIDTH:])
            kc_ref[rs, :] = (kc * (MLSTM_HEAD_DIM ** -0.5)).astype(kc_ref.dtype)
            mv_ref[rs, :] = acc_ref[6, rs, :].astype(mv_ref.dtype)
            og_ref[rs, :] = (_sigmoid(acc_ref[7, rs, :]) * _silu(acc_ref[8, rs, :])).astype(og_ref.dtype)

        def tail():
            pq_ref[...] = acc_ref[4, tm - halo:, :]
            pk_ref[...] = acc_ref[5, tm - halo:, :]
            gate_ref[...] = gt_ref[...]

        def key_block(j):
            rows = slice(j * MOBA_BLOCK, (j + 1) * MOBA_BLOCK)
            qt_ref[0, j] = acc_ref[0, rows, :].T.astype(qt_ref.dtype)
            vt_ref[0, j] = acc_ref[2, rows, :].T.astype(vt_ref.dtype)

        parts = []
        for r in range(0, tm, PROJ_EPILOGUE_ROWS):
            parts.append(functools.partial(chunk, r))
            if (r + PROJ_EPILOGUE_ROWS) % MOBA_BLOCK == 0:
                parts.append(functools.partial(key_block, r // MOBA_BLOCK))
        return parts + [tail]

    def step(mat_bufs, epi_bufs):
        mats = matmul_parts(*mat_bufs) if mat_bufs is not None else []
        epis = epilogue_parts(*epi_bufs) if epi_bufs is not None else []
        per_mat = -(-len(epis) // max(len(mats), 1))
        for part in mats:
            part()
            for e in epis[:per_mat]:
                e()
            epis = epis[per_mat:]
        for e in epis:
            e()

    bufs = ((acc0_ref, gt0_ref), (acc1_ref, gt1_ref))

    @pl.when(s == 0)
    def _():
        step(bufs[0], None)

    for parity in range(2):
        @pl.when((s > 0) & (s < n_tiles) & (s % 2 == parity))
        def _():
            step(bufs[parity], bufs[1 - parity])

    @pl.when(s == n_tiles)
    def _():
        step(None, bufs[(n_tiles - 1) % 2])


def _rope_tables(seq):
    half = ROT_DIM // 2
    inv_freq = jnp.power(ROPE_THETA, -jnp.arange(half, dtype=F32) * 2.0 / ROT_DIM)
    ang = jnp.arange(seq, dtype=jnp.int32).astype(F32)[:, None] * inv_freq[None, :]
    cos = jnp.cos(ang)
    sin = jnp.sin(ang)
    ones = jnp.ones((seq, ATT_HEAD_DIM - ROT_DIM), F32)
    zeros = jnp.zeros((seq, ATT_HEAD_DIM - ROT_DIM), F32)
    zh = jnp.zeros((seq, half), F32)
    cos_h = jnp.concatenate([cos, cos, ones], axis=1)
    sa_h = jnp.concatenate([-sin, zh, zeros], axis=1)
    sb_h = jnp.concatenate([zh, sin, zeros], axis=1)
    rep = LANES // ATT_HEAD_DIM
    return (jnp.tile(cos_h, (1, rep)), jnp.tile(sa_h, (1, rep)), jnp.tile(sb_h, (1, rep)))


def _input_projection(x2, pre_g, w_all, n_main, w_gate, tables, conv_w, conv_b, seq):
    n_tok, d_model = x2.shape
    tm = PROJ_ROWS
    seq_tiles = seq // tm
    n_tiles = n_tok // tm
    n_groups = n_main // ATT_WIDTH
    row_in = lambda s: (jnp.minimum(s, n_tiles - 1), 0)
    row_out = lambda s: (jnp.maximum(s - 1, 0), 0)
    fixed = lambda s: (0, 0)
    tab = lambda s: (jnp.maximum(s - 1, 0) % seq_tiles, 0)
    done = lambda s: jnp.maximum(s - 1, 0)
    wide = jax.ShapeDtypeStruct((n_tok, ATT_WIDTH), BF16)
    wide_spec = pl.BlockSpec((tm, ATT_WIDTH), row_out)
    blocked = jax.ShapeDtypeStruct((n_tok // seq, seq // MOBA_BLOCK, ATT_WIDTH, MOBA_BLOCK), BF16)
    blocked_spec = pl.BlockSpec((1, tm // MOBA_BLOCK, ATT_WIDTH, MOBA_BLOCK),
                                lambda s: (done(s) // seq_tiles, done(s) % seq_tiles, 0, 0))
    return pl.pallas_call(
        functools.partial(_proj_kernel, seq_tiles=seq_tiles, n_tiles=n_tiles),
        out_shape=(blocked, wide, blocked) + (wide,) * 5 + (jax.ShapeDtypeStruct((GATE_ROWS, n_tok), F32),),
        grid=(n_tiles + 1,),
        in_specs=[
            pl.BlockSpec((tm, d_model), row_in),
            pl.BlockSpec((1, d_model), fixed),
            pl.BlockSpec((d_model, n_main), fixed),
            pl.BlockSpec(w_gate.shape, fixed),
            pl.BlockSpec((tm, LANES), tab),
            pl.BlockSpec((tm, LANES), tab),
            pl.BlockSpec((tm, LANES), tab),
            pl.BlockSpec(conv_w.shape, fixed),
            pl.BlockSpec(conv_b.shape, fixed),
        ],
        out_specs=(blocked_spec, wide_spec, blocked_spec) + (wide_spec,) * 5
        + (pl.BlockSpec((GATE_ROWS, tm), lambda s: (0, done(s))),),
        scratch_shapes=[
            pltpu.VMEM((SUBLANES, MLSTM_WIDTH), F32),
            pltpu.VMEM((SUBLANES, MLSTM_WIDTH), F32),
            pltpu.VMEM((n_groups, tm, ATT_WIDTH), F32),
            pltpu.VMEM((n_groups, tm, ATT_WIDTH), F32),
            pltpu.VMEM((GATE_ROWS, tm), F32),
            pltpu.VMEM((GATE_ROWS, tm), F32),
        ],
        compiler_params=pltpu.CompilerParams(
            dimension_semantics=("arbitrary",), vmem_limit_bytes=VMEM_LIMIT),
        name="input_projection",
    )(x2, pre_g, w_all, w_gate, *tables, conv_w, conv_b)


def _moba_kernel(own_tab, past_tab, qt_ref, k_ref, vt_ref, o_ref,
                 kst_hi, kst_mid, kst_lo, bias_ref, m_ref, acc_ref,
                 *stage_bufs, nb, own_ticks, past_ticks):
    blk = MOBA_BLOCK
    hd = ATT_HEAD_DIM

    lane_head = lax.broadcasted_iota(jnp.int32, (nb, ATT_WIDTH), 1) // hd
    means = []
    for n in range(nb):
        means.append(jnp.mean(k_ref[0, n].astype(F32), axis=0, keepdims=True))
    km = jnp.concatenate(means, axis=0)
    for h in range(ATT_HEADS):
        hi, mid, lo = _split3(jnp.where(lane_head == h, km, 0.0))
        kst_hi[h * nb:(h + 1) * nb, :] = hi
        kst_mid[h * nb:(h + 1) * nb, :] = mid
        kst_lo[h * nb:(h + 1) * nb, :] = lo

    n_iota = lax.broadcasted_iota(jnp.int32, (nb, blk), 0)
    n_iota_f = n_iota.astype(F32)
    tail_iota = lax.broadcasted_iota(jnp.int32, (SUBLANES, blk), 0)
    tail_rows = jnp.where(tail_iota == 0, 0.0, NEG_BIG)

    def prepare(i, carry):
        qt = qt_ref[0, i]
        gates = _dot(kst_hi[...], qt) + _dot(kst_mid[...], qt) + _dot(kst_lo[...], qt)
        past = n_iota < i
        for h in range(ATT_HEADS):
            g = jnp.where(past, gates[h * nb:(h + 1) * nb, :], -jnp.inf)
            sel = jnp.zeros((nb, blk), jnp.bool_)
            for _ in range(min(MOBA_TOPK, nb)):
                gmax = jnp.max(g, axis=0, keepdims=True)
                first = jnp.min(jnp.where(g == gmax, n_iota_f, float(nb)), axis=0, keepdims=True)
                hit = n_iota_f == first
                sel = sel | hit
                g = jnp.where(hit, -jnp.inf, g)
            bias_ref[i, h, 0:nb, :] = jnp.where(sel & past, 0.0, NEG_BIG)
            bias_ref[i, h, nb:nb + SUBLANES, :] = tail_rows
            m_ref[i, h] = jnp.full((1, blk), NEG_BIG, F32)
            acc_ref[i, h] = jnp.zeros(acc_ref.shape[2:], F32)
        return carry

    lax.fori_loop(0, nb, prepare, 0)

    l_iota = lax.broadcasted_iota(jnp.int32, (blk, blk), 0)
    q_iota = lax.broadcasted_iota(jnp.int32, (blk, blk), 1)
    causal = l_iota <= q_iota
    row_head = lax.broadcasted_iota(jnp.int32, (LANES, blk), 0) // hd
    ones_rows = jnp.ones((2 * SUBLANES, blk), BF16)
    s_bufs, c_bufs, p_bufs, a_bufs = (stage_bufs[j * _SKEW:(j + 1) * _SKEW] for j in range(4))

    def run(tab, n_ticks, own):
        def scores(tick, slot):
            qi = tab[0, tick]
            kb = k_ref[0, tab[1, tick]]
            for h in range(ATT_HEADS):
                pr = slice((h // 2) * LANES, (h // 2 + 1) * LANES)
                qpair = qt_ref[0, qi, pr, :]
                qh = jnp.where(row_head == (h % 2), qpair, jnp.zeros_like(qpair))
                st = _dot(kb[:, pr], qh)
                st = jnp.where(causal, st, NEG_BIG) if own else st
                s_bufs[slot][h] = st
                c_bufs[slot][h] = jnp.max(st.reshape(blk // SUBLANES, SUBLANES, blk), axis=0)

        def softmax_update(tick, slot):
            qi = tab[0, tick]
            mask_row = tab[2, tick]
            for h in range(ATT_HEADS):
                st = s_bufs[slot][h]
                b = bias_ref[qi, h, pl.ds(mask_row, 1), :]
                m_old = m_ref[qi, h]
                m_new = jnp.maximum(m_old, jnp.max(c_bufs[slot][h], axis=0, keepdims=True) + b)
                p_bufs[slot][h] = jnp.exp2(st - (m_new - b)).astype(BF16)
                a_bufs[slot][h] = jnp.exp2(m_old - m_new)
                m_ref[qi, h] = m_new

        def values(tick, slot):
            qi = tab[0, tick]
            vtb = vt_ref[0, tab[1, tick]]
            for h in range(ATT_HEADS):
                lhs = jnp.concatenate([vtb[h * hd:(h + 1) * hd, :], ones_rows], axis=0)
                acc_ref[qi, h] = a_bufs[slot][h] * acc_ref[qi, h] + _dot(lhs, p_bufs[slot][h])

        for t in range(_SKEW):
            scores(t, t % _SKEW)
        for t in range(_SKEW, 2 * _SKEW):
            softmax_update(t - _SKEW, t % _SKEW)
            scores(t, t % _SKEW)

        def body(u, carry):
            t = _SKEW * (u + 2)
            for d in range(_SKEW):
                values(t + d - 2 * _SKEW, d)
                softmax_update(t + d - _SKEW, d)
                scores(t + d, d)
            return carry

        lax.fori_loop(0, -(-n_ticks // _SKEW), body, 0)

    run(own_tab, own_ticks, True)
    run(past_tab, past_ticks, False)

    def finalize(i, carry):
        for h in range(ATT_HEADS):
            acc = acc_ref[i, h]
            o_ref[0, i, h * hd:(h + 1) * hd, :] = (acc[:hd, :] / acc[hd:hd + 1, :]).astype(o_ref.dtype)
        return carry

    lax.fori_loop(0, nb, finalize, 0)


def _tick_table(ticks, n_ticks, pad):
    length = _SKEW * (-(-n_ticks // _SKEW) + 2)
    rows = list(ticks) + [pad] * (length - len(ticks))
    return jnp.asarray(rows, jnp.int32).T


def _moba_attention(qt4, k4, vt4):
    bsz, nb, width, blk = qt4.shape
    hn = ATT_HEADS * nb
    own = [(i, i, nb) for i in range(nb)]
    past = [(i, n, n) for i in range(nb) for n in range(i)]
    own_tab = _tick_table(own, len(own), (nb - 1, nb - 1, nb + 1))
    past_tab = _tick_table(past, len(past), (nb - 1, 0, nb + 1))
    tile_f32 = pltpu.VMEM((ATT_HEADS, blk, blk), F32)
    tile_bf16 = pltpu.VMEM((ATT_HEADS, blk, blk), BF16)
    row_f32 = pltpu.VMEM((ATT_HEADS, 1, blk), F32)
    part_f32 = pltpu.VMEM((ATT_HEADS, SUBLANES, blk), F32)
    whole = lambda b, *_: (b, 0, 0, 0)
    return pl.pallas_call(
        functools.partial(_moba_kernel, nb=nb, own_ticks=len(own), past_ticks=len(past)),
        out_shape=jax.ShapeDtypeStruct((bsz, nb, width, blk), BF16),
        grid_spec=pltpu.PrefetchScalarGridSpec(
            num_scalar_prefetch=2,
            grid=(bsz,),
            in_specs=[
                pl.BlockSpec((1, nb, width, blk), whole),
                pl.BlockSpec((1, nb, blk, width), whole),
                pl.BlockSpec((1, nb, width, blk), whole),
            ],
            out_specs=pl.BlockSpec((1, nb, width, blk), whole),
            scratch_shapes=[
                pltpu.VMEM((hn, width), BF16),
                pltpu.VMEM((hn, width), BF16),
                pltpu.VMEM((hn, width), BF16),
                pltpu.VMEM((nb, ATT_HEADS, nb + SUBLANES, blk), F32),
                pltpu.VMEM((nb, ATT_HEADS, 1, blk), F32),
                pltpu.VMEM((nb, ATT_HEADS, ATT_HEAD_DIM + 2 * SUBLANES, blk), F32),
                *([tile_f32] * _SKEW + [part_f32] * _SKEW + [tile_bf16] * _SKEW + [row_f32] * _SKEW),
            ],
        ),
        compiler_params=pltpu.CompilerParams(
            dimension_semantics=("arbitrary",), vmem_limit_bytes=VMEM_LIMIT),
        name="moba_attention",
    )(own_tab, past_tab, qt4, k4, vt4)


def _mlstm_kernel(qc_ref, kc_ref, mv_ref, og_ref, gate_ref, bias_ref, gain_ref, triu_ref, select_ref,
                  o_ref, ct_ref, m_ref):
    @pl.when(pl.program_id(1) == 0)
    def _():
        ct_ref[...] = jnp.zeros_like(ct_ref)
        m_ref[...] = jnp.zeros_like(m_ref)

    L = MLSTM_CHUNK
    chunks = []
    for j in range(MLSTM_CHUNKS_PER_STEP):
        ts = slice(j * L, (j + 1) * L)
        chunks.append(_mlstm_chunk(
            qc_ref.at[:, ts, :], kc_ref.at[:, ts, :], mv_ref.at[:, ts, :], og_ref.at[:, ts, :],
            gate_ref.at[:, ts], bias_ref, gain_ref, triu_ref, select_ref,
            o_ref.at[:, ts, :], ct_ref, m_ref))
    for c in chunks:
        next(c)
    for c in chunks:
        next(c)
        next(c)
    for c in chunks:
        next(c, None)


def _mlstm_chunk(qc_ref, kc_ref, mv_ref, og_ref, gate_ref, bias_ref, gain_ref, triu_ref, select_ref,
                 o_ref, ct_ref, m_ref):
    L = MLSTM_CHUNK
    hd = MLSTM_HEAD_DIM
    nh = MLSTM_HEADS

    gr = gate_ref[...] + bias_ref[...]
    r_hi, r_mid, r_lo = _split3(_log_sigmoid(gr) * LOG2_E)
    triu = triu_ref[...]
    bcum = _dot(r_hi, triu) + _dot(r_mid, triu) + _dot(r_lo, triu)
    row = lax.broadcasted_iota(jnp.int32, gr.shape, 0)
    rows_ib = jnp.where(row < nh, gr * LOG2_E, bcum)
    rows_c = rows_ib[0:nh, :] - rows_ib[nh:2 * nh, :]
    lhs_t = jnp.concatenate(list(_split3(rows_ib)) + [jnp.ones((GATE_ROWS, L), BF16)], axis=0)

    t_iota = lax.broadcasted_iota(jnp.int32, (L, LANES), 0)
    s_iota = lax.broadcasted_iota(jnp.int32, (L, LANES), 1)
    zeros_tail = jnp.zeros((GATE_ROWS, 2 * LANES), BF16)

    gain = gain_ref[...]
    heads = range(MLSTM_HEADS)
    slices = [slice(h * hd, (h + 1) * hd) for h in heads]
    srow = lax.broadcasted_iota(jnp.int32, (GATE_ROWS, L), 0)

    res_all, qk_all = [], []
    for h in heads:
        c = rows_c[h:h + 1, :]
        c_hi = c.astype(BF16).astype(F32)
        c_mid = (c - c_hi).astype(BF16).astype(F32)
        c_lo = c - c_hi - c_mid
        tail = jnp.where(srow == 0, c_hi, jnp.where(srow == 1, c_mid, jnp.where(srow == 2, c_lo, 0.0)))
        sel_h = jnp.concatenate(
            [select_ref[h], jnp.concatenate([tail.astype(BF16), zeros_tail], axis=1)], axis=0)
        res_all.append(lax.dot_general(lhs_t, sel_h, (((0,), (0,)), ((), ())), preferred_element_type=F32))
        qb = qc_ref[0, :, slices[h]]
        kb = kc_ref[0, :, slices[h]]
        qk_all.append(lax.dot_general(qb, kb, (((1,), (1,)), ((), ())), preferred_element_type=F32))
    yield

    s_all, wv_all, carry_all = [], [], []
    for h in heads:
        sl = slices[h]
        res = res_all[h]
        qk = qk_all[h]
        v = mv_ref[0, :, sl]
        m_prev = m_ref[h]
        b_col = res[:, L:L + LANES]
        i_col = res[:, L + LANES:L + 2 * LANES]

        dslabs = []
        for j in range(L // LANES):
            keep = (s_iota + j * LANES) <= t_iota
            dslabs.append(jnp.where(keep, res[:, j * LANES:(j + 1) * LANES], -jnp.inf))
        dmax = jnp.max(dslabs[0], axis=-1, keepdims=True)
        for d in dslabs[1:]:
            dmax = jnp.maximum(dmax, jnp.max(d, axis=-1, keepdims=True))
        inter = b_col + m_prev
        m_t = jnp.maximum(inter, dmax)
        w_inter = jnp.exp2(inter - m_t)
        s = jnp.concatenate([qk[:, j * LANES:(j + 1) * LANES] * jnp.exp2(d - m_t)
                             for j, d in enumerate(dslabs)], axis=1)

        b_last = b_col[L - 1:L, :]
        g = b_last - b_col + i_col
        m_new = jnp.maximum(b_last + m_prev, jnp.max(g, axis=0, keepdims=True))
        decay = jnp.exp2(b_last + m_prev - m_new)
        w = jnp.exp2(g - m_new)
        m_ref[h] = m_new
        s_all.append(s.astype(BF16))
        wv_all.append(jnp.concatenate([w * v.astype(F32), w], axis=1).astype(BF16))
        carry_all.append((w_inter, jnp.exp2(-m_t), jnp.concatenate([decay, decay], axis=1)))

    yield

    inter_all = [_dot(qc_ref[0, :, slices[h]], ct_ref[h].astype(BF16)) for h in heads]
    ones_cols = jnp.ones((L, hd), BF16)
    intra_all = [_dot(s_all[h], jnp.concatenate([mv_ref[0, :, slices[h]], ones_cols], axis=1))
                 for h in heads]
    for h in heads:
        ct_ref[h] = carry_all[h][2] * ct_ref[h] + lax.dot_general(
            kc_ref[0, :, slices[h]], wv_all[h], (((0,), (0,)), ((), ())), preferred_element_type=F32)

    yield

    for h in heads:
        sl = slices[h]
        w_inter, floor, _ = carry_all[h]
        num = w_inter * inter_all[h][:, :hd] + intra_all[h][:, :hd]
        nq = w_inter * inter_all[h][:, hd:] + intra_all[h][:, hd:]
        hh = num / jnp.maximum(jnp.abs(nq), floor)
        mu = jnp.mean(hh, axis=-1, keepdims=True)
        var = jnp.mean(jnp.square(hh - mu), axis=-1, keepdims=True)
        hn = (hh - mu) * lax.rsqrt(var + NORM_EPS) * gain[:, sl]
        o_ref[0, :, sl] = (og_ref[0, :, sl].astype(F32) * hn).astype(o_ref.dtype)


def _mlstm(qc, kc, mv, og, gates_t, gate_bias, out_g):
    bsz, seq, width = qc.shape
    L = MLSTM_CHUNK
    step = L * MLSTM_CHUNKS_PER_STEP
    nc = seq // step
    nh = MLSTM_HEADS
    triu = jnp.triu(jnp.ones((L, L), F32)).astype(BF16)
    r_in = jnp.arange(3 * GATE_ROWS)[None, :, None] % GATE_ROWS
    col = jnp.arange(L + 2 * LANES)[None, None, :]
    head = jnp.arange(nh)[:, None, None]
    select = jnp.where(col < L + LANES, r_in == nh + head, r_in == head).astype(BF16)
    tok = lambda b, c: (b, c, 0)
    fixed = lambda b, c: (0, 0)
    wide = pl.BlockSpec((1, step, width), tok)
    return pl.pallas_call(
        _mlstm_kernel,
        out_shape=jax.ShapeDtypeStruct((bsz, seq, width), BF16),
        grid=(bsz, nc),
        in_specs=[
            wide, wide, wide, wide,
            pl.BlockSpec((GATE_ROWS, step), lambda b, c: (0, b * nc + c)),
            pl.BlockSpec((GATE_ROWS, 1), fixed),
            pl.BlockSpec(out_g.shape, fixed),
            pl.BlockSpec((L, L), fixed),
            pl.BlockSpec(select.shape, lambda b, c: (0, 0, 0)),
        ],
        out_specs=wide,
        scratch_shapes=[
            pltpu.VMEM((nh, MLSTM_HEAD_DIM, 2 * MLSTM_HEAD_DIM), F32),
            pltpu.VMEM((nh, 1, LANES), F32),
        ],
        compiler_params=pltpu.CompilerParams(
            dimension_semantics=("arbitrary", "arbitrary"), vmem_limit_bytes=VMEM_LIMIT),
        name="mlstm",
    )(qc, kc, mv, og, gates_t, gate_bias, out_g, triu, select)


def _out_kernel(x_ref, o_ref, sz_ref, ym_ref, ag_ref, w_ref, pg_ref, out_ref):
    o = jnp.concatenate([o_ref[0, j].astype(F32).T for j in range(o_ref.shape[1])], axis=0)
    ms = jnp.mean(o * o, axis=-1, keepdims=True)
    ya = (o * lax.rsqrt(ms + NORM_EPS)) * ag_ref[...] * sz_ref[...].astype(F32)
    y = _dot(ya.astype(BF16), w_ref[:ATT_WIDTH, :]) + _dot(ym_ref[...], w_ref[ATT_WIDTH:, :])
    ms2 = jnp.mean(y * y, axis=-1, keepdims=True)
    out_ref[...] = x_ref[...] + (y * lax.rsqrt(ms2 + NORM_EPS)) * pg_ref[...]


def _output_projection(x2, ot4, sz, ym, attn_g, w_out, post_g):
    n_tok, d_model = x2.shape
    tm = OUT_ROWS
    per_tile = tm // MOBA_BLOCK
    tiles_per_seq = ot4.shape[1] // per_tile
    row = lambda i: (i, 0)
    fixed = lambda i: (0, 0)
    return pl.pallas_call(
        _out_kernel,
        out_shape=jax.ShapeDtypeStruct((n_tok, d_model), F32),
        grid=(n_tok // tm,),
        in_specs=[
            pl.BlockSpec((tm, d_model), row),
            pl.BlockSpec((1, per_tile, ATT_WIDTH, MOBA_BLOCK),
                         lambda i: (i // tiles_per_seq, i % tiles_per_seq, 0, 0)),
            pl.BlockSpec((tm, ATT_WIDTH), row),
            pl.BlockSpec((tm, MLSTM_WIDTH), row),
            pl.BlockSpec((1, ATT_WIDTH), fixed),
            pl.BlockSpec(w_out.shape, fixed),
            pl.BlockSpec((1, d_model), fixed),
        ],
        out_specs=pl.BlockSpec((tm, d_model), row),
        compiler_params=pltpu.CompilerParams(
            dimension_semantics=("arbitrary",), vmem_limit_bytes=VMEM_LIMIT),
        name="output_projection",
    )(x2, ot4, sz, ym, attn_g, w_out, post_g)


def _layer(x, pre_g, w_in, i_bias, f_bias, conv_w, conv_b, attn_g, mlstm_g, w_out, post_g):
    bsz, seq, d_model = x.shape
    assert seq % MOBA_BLOCK == 0 and seq % PROJ_ROWS == 0
    assert seq % (MLSTM_CHUNK * MLSTM_CHUNKS_PER_STEP) == 0
    nb = seq // MOBA_BLOCK
    n_tok = bsz * seq
    assert seq % OUT_ROWS == 0
    n_main = 4 * ATT_WIDTH + 5 * MLSTM_WIDTH
    n_gate = 2 * MLSTM_HEADS
    assert w_in.shape == (d_model, n_main + n_gate)

    w_all = w_in.astype(BF16)
    w_gate = jnp.pad(w_all[:, n_main:], ((0, 0), (0, GATE_ROWS - n_gate)))
    x2 = x.reshape(n_tok, d_model)
    (qt4, k, vt4, sz, qc, kc, mv, og, gates_t) = _input_projection(
        x2, pre_g[None, :], w_all, n_main, w_gate, _rope_tables(seq), conv_w, conv_b[None, :], seq)

    ot4 = _moba_attention(qt4, k.reshape(bsz, nb, MOBA_BLOCK, ATT_WIDTH), vt4)

    gate_bias = jnp.pad(jnp.concatenate([i_bias, f_bias]), (0, GATE_ROWS - n_gate))[:, None]
    shp = (bsz, seq, MLSTM_WIDTH)
    ym = _mlstm(qc.reshape(shp), kc.reshape(shp), mv.reshape(shp), og.reshape(shp),
                gates_t, gate_bias, mlstm_g[None, :])

    out = _output_projection(x2, ot4, sz, ym.reshape(n_tok, MLSTM_WIDTH), attn_g[None, :],
                             w_out.astype(BF16), post_g[None, :])
    return out.reshape(bsz, seq, d_model)


def kernel(x, pre_norm_g, w_in, mlstm_i_bias, mlstm_f_bias, conv_w, conv_b, attn_out_g, mlstm_out_g,
           w_out, post_norm_g):
    for l in range(pre_norm_g.shape[0]):
        x = _layer(x, pre_norm_g[l], w_in[l], mlstm_i_bias[l], mlstm_f_bias[l], conv_w[l], conv_b[l],
                   attn_out_g[l], mlstm_out_g[l], w_out[l], post_norm_g[l])
    return x
```

```python
import functools
import math

import jax
import jax.numpy as jnp
from jax import lax
from jax.experimental import pallas as pl
from jax.experimental.pallas import tpu as pltpu

F32 = jnp.float32
BF16 = jnp.bfloat16

ATT_HEADS = 8
ATT_HEAD_DIM = 64
ATT_WIDTH = ATT_HEADS * ATT_HEAD_DIM
ROT_DIM = ATT_HEAD_DIM // 4
ROPE_THETA = 500000.0
MOBA_BLOCK = 256
MOBA_TOPK = 3
MLSTM_HEADS = 4
MLSTM_HEAD_DIM = 128
MLSTM_WIDTH = MLSTM_HEADS * MLSTM_HEAD_DIM
CONV_WIDTH = 4
NORM_EPS = 1e-6

LANES = 128
SUBLANES = 8
GATE_ROWS = 16
PROJ_ROWS = 512
PROJ_EPILOGUE_ROWS = 32
OUT_ROWS = 1024
MLSTM_CHUNK = 256
MLSTM_CHUNKS_PER_STEP = 2
NEG_BIG = -1e30
LOG2_E = 1.4426950408889634
_SKEW = 2
VMEM_LIMIT = 56 * 1024 * 1024


def _sigmoid(x):
    return 1.0 / (1.0 + jnp.exp(-x))


def _silu(x):
    return x * _sigmoid(x)


def _log_sigmoid(x):
    return jnp.minimum(x, 0.0) - jnp.log(1.0 + jnp.exp(-jnp.abs(x)))


def _split3(x):
    hi = x.astype(BF16)
    r1 = x - hi.astype(F32)
    mid = r1.astype(BF16)
    lo = (r1 - mid.astype(F32)).astype(BF16)
    return hi, mid, lo


def _dot(a, b):
    return jnp.dot(a, b, preferred_element_type=F32)


def _proj_kernel(x_ref, g_ref, w_ref, wg_ref, cos_ref, sa_ref, sb_ref, cw_ref, cb_ref,
                 qt_ref, k_ref, vt_ref, sz_ref, qc_ref, kc_ref, mv_ref, og_ref, gate_ref,
                 pq_ref, pk_ref, acc0_ref, acc1_ref, gt0_ref, gt1_ref, *, seq_tiles, n_tiles):
    s = pl.program_id(0)
    first_of_seq = ((s - 1) % seq_tiles) == 0
    tm = x_ref.shape[0]
    halo = SUBLANES
    n_groups = w_ref.shape[1] // ATT_WIDTH

    def matmul_parts(acc_ref, gt_ref):
        x = x_ref[...]
        ms = jnp.mean(x * x, axis=-1, keepdims=True)
        hb = ((x * lax.rsqrt(ms + NORM_EPS)) * g_ref[...]).astype(BF16)

        def group(gi):
            acc_ref[gi] = _dot(hb, w_ref[:, gi * ATT_WIDTH:(gi + 1) * ATT_WIDTH])

        def gates():
            gt_ref[...] = lax.dot_general(wg_ref[...], hb, (((0,), (1,)), ((), ())),
                                          preferred_element_type=F32)

        return [functools.partial(group, gi) for gi in range(n_groups)] + [gates]

    def rope(acc, rs):
        cos = cos_ref[rs, :]
        sa = sa_ref[rs, :]
        sb = sb_ref[rs, :]
        parts = []
        for c in range(ATT_WIDTH // LANES):
            xs = acc[:, c * LANES:(c + 1) * LANES]
            parts.append(xs * cos
                         + pltpu.roll(xs, LANES - ROT_DIM // 2, 1) * sa
                         + pltpu.roll(xs, ROT_DIM // 2, 1) * sb)
        return jnp.concatenate(parts, axis=1)

    def conv_silu(acc_ref, gi, r, prev_ref, w, b):
        rows = PROJ_EPILOGUE_ROWS
        if r == 0:
            prev = jnp.where(first_of_seq, 0.0, prev_ref[...])
        else:
            prev = acc_ref[gi, r - halo:r, :]
        ext = jnp.concatenate([prev, acc_ref[gi, r:r + rows, :]], axis=0)
        out = b
        for j in range(CONV_WIDTH):
            off = halo - (CONV_WIDTH - 1) + j
            out = out + w[j:j + 1, :] * ext[off:off + rows, :]
        return _silu(out)

    def epilogue_parts(acc_ref, gt_ref):
        cw = cw_ref[...]
        cb = cb_ref[...]

        def chunk(r):
            rs = slice(r, r + PROJ_EPILOGUE_ROWS)
            acc_ref[0, rs, :] = rope(acc_ref[0, rs, :], rs) * (LOG2_E * ATT_HEAD_DIM ** -0.5)
            k_ref[rs, :] = rope(acc_ref[1, rs, :], rs).astype(k_ref.dtype)
            sz_ref[rs, :] = _silu(acc_ref[3, rs, :]).astype(sz_ref.dtype)
            qc = conv_silu(acc_ref, 4, r, pq_ref, cw[:, :MLSTM_WIDTH], cb[:, :MLSTM_WIDTH])
            qc_ref[rs, :] = qc.astype(qc_ref.dtype)
            kc = conv_silu(acc_ref, 5, r, pk_ref, cw[:, MLSTM_WIDTH:], cb[:, MLSTM_WIDTH:])
            kc_ref[rs, :] = (kc * (MLSTM_HEAD_DIM ** -0.5)).astype(kc_ref.dtype)
            mv_ref[rs, :] = acc_ref[6, rs, :].astype(mv_ref.dtype)
            og_ref[rs, :] = (_sigmoid(acc_ref[7, rs, :]) * _silu(acc_ref[8, rs, :])).astype(og_ref.dtype)

        def tail():
            pq_ref[...] = acc_ref[4, tm - halo:, :]
            pk_ref[...] = acc_ref[5, tm - halo:, :]
            gate_ref[...] = gt_ref[...]

        def key_block(j):
            rows = slice(j * MOBA_BLOCK, (j + 1) * MOBA_BLOCK)
            qt_ref[0, j] = acc_ref[0, rows, :].T.astype(qt_ref.dtype)
            vt_ref[0, j] = acc_ref[2, rows, :].T.astype(vt_ref.dtype)

        parts = []
        for r in range(0, tm, PROJ_EPILOGUE_ROWS):
            parts.append(functools.partial(chunk, r))
            if (r + PROJ_EPILOGUE_ROWS) % MOBA_BLOCK == 0:
                parts.append(functools.partial(key_block, r // MOBA_BLOCK))
        return parts + [tail]

    def step(mat_bufs, epi_bufs):
        mats = matmul_parts(*mat_bufs) if mat_bufs is not None else []
        epis = epilogue_parts(*epi_bufs) if epi_bufs is not None else []
        per_mat = -(-len(epis) // max(len(mats), 1))
        for part in mats:
            part()
            for e in epis[:per_mat]:
                e()
            epis = epis[per_mat:]
        for e in epis:
            e()

    bufs = ((acc0_ref, gt0_ref), (acc1_ref, gt1_ref))

    @pl.when(s == 0)
    def _():
        step(bufs[0], None)

    for parity in range(2):
        @pl.when((s > 0) & (s < n_tiles) & (s % 2 == parity))
        def _():
            step(bufs[parity], bufs[1 - parity])

    @pl.when(s == n_tiles)
    def _():
        step(None, bufs[(n_tiles - 1) % 2])


def _rope_tables(seq):
    half = ROT_DIM // 2
    inv_freq = jnp.power(ROPE_THETA, -jnp.arange(half, dtype=F32) * 2.0 / ROT_DIM)
    ang = jnp.arange(seq, dtype=jnp.int32).astype(F32)[:, None] * inv_freq[None, :]
    cos = jnp.cos(ang)
    sin = jnp.sin(ang)
    ones = jnp.ones((seq, ATT_HEAD_DIM - ROT_DIM), F32)
    zeros = jnp.zeros((seq, ATT_HEAD_DIM - ROT_DIM), F32)
    zh = jnp.zeros((seq, half), F32)
    cos_h = jnp.concatenate([cos, cos, ones], axis=1)
    sa_h = jnp.concatenate([-sin, zh, zeros], axis=1)
    sb_h = jnp.concatenate([zh, sin, zeros], axis=1)
    rep = LANES // ATT_HEAD_DIM
    return (jnp.tile(cos_h, (1, rep)), jnp.tile(sa_h, (1, rep)), jnp.tile(sb_h, (1, rep)))


def _input_projection(x2, pre_g, w_all, n_main, w_gate, tables, conv_w, conv_b, seq):
    n_tok, d_model = x2.shape
    tm = PROJ_ROWS
    seq_tiles = seq // tm
    n_tiles = n_tok // tm
    n_groups = n_main // ATT_WIDTH
    row_in = lambda s: (jnp.minimum(s, n_tiles - 1), 0)
    row_out = lambda s: (jnp.maximum(s - 1, 0), 0)
    fixed = lambda s: (0, 0)
    tab = lambda s: (jnp.maximum(s - 1, 0) % seq_tiles, 0)
    done = lambda s: jnp.maximum(s - 1, 0)
    wide = jax.ShapeDtypeStruct((n_tok, ATT_WIDTH), BF16)
    wide_spec = pl.BlockSpec((tm, ATT_WIDTH), row_out)
    blocked = jax.ShapeDtypeStruct((n_tok // seq, seq // MOBA_BLOCK, ATT_WIDTH, MOBA_BLOCK), BF16)
    blocked_spec = pl.BlockSpec((1, tm // MOBA_BLOCK, ATT_WIDTH, MOBA_BLOCK),
                                lambda s: (done(s) // seq_tiles, done(s) % seq_tiles, 0, 0))
    return pl.pallas_call(
        functools.partial(_proj_kernel, seq_tiles=seq_tiles, n_tiles=n_tiles),
        out_shape=(blocked, wide, blocked) + (wide,) * 5 + (jax.ShapeDtypeStruct((GATE_ROWS, n_tok), F32),),
        grid=(n_tiles + 1,),
        in_specs=[
            pl.BlockSpec((tm, d_model), row_in),
            pl.BlockSpec((1, d_model), fixed),
            pl.BlockSpec((d_model, n_main), fixed),
            pl.BlockSpec(w_gate.shape, fixed),
            pl.BlockSpec((tm, LANES), tab),
            pl.BlockSpec((tm, LANES), tab),
            pl.BlockSpec((tm, LANES), tab),
            pl.BlockSpec(conv_w.shape, fixed),
            pl.BlockSpec(conv_b.shape, fixed),
        ],
        out_specs=(blocked_spec, wide_spec, blocked_spec) + (wide_spec,) * 5
        + (pl.BlockSpec((GATE_ROWS, tm), lambda s: (0, done(s))),),
        scratch_shapes=[
            pltpu.VMEM((SUBLANES, MLSTM_WIDTH), F32),
            pltpu.VMEM((SUBLANES, MLSTM_WIDTH), F32),
            pltpu.VMEM((n_groups, tm, ATT_WIDTH), F32),
            pltpu.VMEM((n_groups, tm, ATT_WIDTH), F32),
            pltpu.VMEM((GATE_ROWS, tm), F32),
            pltpu.VMEM((GATE_ROWS, tm), F32),
        ],
        compiler_params=pltpu.CompilerParams(
            dimension_semantics=("arbitrary",), vmem_limit_bytes=VMEM_LIMIT),
        name="input_projection",
    )(x2, pre_g, w_all, w_gate, *tables, conv_w, conv_b)


def _moba_kernel(own_tab, past_tab, qt_ref, k_ref, vt_ref, o_ref,
                 kst_hi, kst_mid, kst_lo, bias_ref, m_ref, acc_ref,
                 *stage_bufs, nb, own_ticks, past_ticks):
    blk = MOBA_BLOCK
    hd = ATT_HEAD_DIM

    lane_head = lax.broadcasted_iota(jnp.int32, (nb, ATT_WIDTH), 1) // hd
    means = []
    for n in range(nb):
        means.append(jnp.mean(k_ref[0, n].astype(F32), axis=0, keepdims=True))
    km = jnp.concatenate(means, axis=0)
    for h in range(ATT_HEADS):
        hi, mid, lo = _split3(jnp.where(lane_head == h, km, 0.0))
        kst_hi[h * nb:(h + 1) * nb, :] = hi
        kst_mid[h * nb:(h + 1) * nb, :] = mid
        kst_lo[h * nb:(h + 1) * nb, :] = lo

    n_iota = lax.broadcasted_iota(jnp.int32, (nb, blk), 0)
    n_iota_f = n_iota.astype(F32)
    tail_iota = lax.broadcasted_iota(jnp.int32, (SUBLANES, blk), 0)
    tail_rows = jnp.where(tail_iota == 0, 0.0, NEG_BIG)

    def prepare(i, carry):
        qt = qt_ref[0, i]
        gates = _dot(kst_hi[...], qt) + _dot(kst_mid[...], qt) + _dot(kst_lo[...], qt)
        past = n_iota < i
        for h in range(ATT_HEADS):
            g = jnp.where(past, gates[h * nb:(h + 1) * nb, :], -jnp.inf)
            sel = jnp.zeros((nb, blk), jnp.bool_)
            for _ in range(min(MOBA_TOPK, nb)):
                gmax = jnp.max(g, axis=0, keepdims=True)
                first = jnp.min(jnp.where(g == gmax, n_iota_f, float(nb)), axis=0, keepdims=True)
                hit = n_iota_f == first
                sel = sel | hit
                g = jnp.where(hit, -jnp.inf, g)
            bias_ref[i, h, 0:nb, :] = jnp.where(sel & past, 0.0, NEG_BIG)
            bias_ref[i, h, nb:nb + SUBLANES, :] = tail_rows
            m_ref[i, h] = jnp.full((1, blk), NEG_BIG, F32)
            acc_ref[i, h] = jnp.zeros(acc_ref.shape[2:], F32)
        return carry

    lax.fori_loop(0, nb, prepare, 0, unroll=math.gcd(nb, 4))

    l_iota = lax.broadcasted_iota(jnp.int32, (blk, blk), 0)
    q_iota = lax.broadcasted_iota(jnp.int32, (blk, blk), 1)
    causal = l_iota <= q_iota
    row_head = lax.broadcasted_iota(jnp.int32, (LANES, blk), 0) // hd
    ones_rows = jnp.ones((2 * SUBLANES, blk), BF16)
    s_bufs, c_bufs, p_bufs, a_bufs = (stage_bufs[j * _SKEW:(j + 1) * _SKEW] for j in range(4))

    def run(tab, n_ticks, own):
        def scores(tick, slot):
            qi = tab[0, tick]
            kb = k_ref[0, tab[1, tick]]
            for h in range(ATT_HEADS):
                pr = slice((h // 2) * LANES, (h // 2 + 1) * LANES)
                qpair = qt_ref[0, qi, pr, :]
                qh = jnp.where(row_head == (h % 2), qpair, jnp.zeros_like(qpair))
                st = _dot(kb[:, pr], qh)
                st = jnp.where(causal, st, NEG_BIG) if own else st
                s_bufs[slot][h] = st
                c_bufs[slot][h] = jnp.max(st.reshape(blk // SUBLANES, SUBLANES, blk), axis=0)

        def softmax_update(tick, slot):
            qi = tab[0, tick]
            mask_row = tab[2, tick]
            for h in range(ATT_HEADS):
                st = s_bufs[slot][h]
                b = bias_ref[qi, h, pl.ds(mask_row, 1), :]
                m_old = m_ref[qi, h]
                m_new = jnp.maximum(m_old, jnp.max(c_bufs[slot][h], axis=0, keepdims=True) + b)
                p_bufs[slot][h] = jnp.exp2(st - (m_new - b)).astype(BF16)
                a_bufs[slot][h] = jnp.exp2(m_old - m_new)
                m_ref[qi, h] = m_new

        def values(tick, slot):
            qi = tab[0, tick]
            vtb = vt_ref[0, tab[1, tick]]
            for h in range(ATT_HEADS):
                lhs = jnp.concatenate([vtb[h * hd:(h + 1) * hd, :], ones_rows], axis=0)
                acc_ref[qi, h] = a_bufs[slot][h] * acc_ref[qi, h] + _dot(lhs, p_bufs[slot][h])

        for t in range(_SKEW):
            scores(t, t % _SKEW)
        for t in range(_SKEW, 2 * _SKEW):
            softmax_update(t - _SKEW, t % _SKEW)
            scores(t, t % _SKEW)

        def body(u, carry):
            t = _SKEW * (u + 2)
            for d in range(_SKEW):
                values(t + d - 2 * _SKEW, d)
                softmax_update(t + d - _SKEW, d)
                scores(t + d, d)
            return carry

        lax.fori_loop(0, -(-n_ticks // _SKEW), body, 0)

    run(own_tab, own_ticks, True)
    run(past_tab, past_ticks, False)

    def finalize(i, carry):
        for h in range(ATT_HEADS):
            acc = acc_ref[i, h]
            o_ref[0, i, h * hd:(h + 1) * hd, :] = (acc[:hd, :] / acc[hd:hd + 1, :]).astype(o_ref.dtype)
        return carry

    lax.fori_loop(0, nb, finalize, 0)


def _tick_table(ticks, n_ticks, pad):
    length = _SKEW * (-(-n_ticks // _SKEW) + 2)
    rows = list(ticks) + [pad] * (length - len(ticks))
    return jnp.asarray(rows, jnp.int32).T


def _moba_attention(qt4, k4, vt4):
    bsz, nb, width, blk = qt4.shape
    hn = ATT_HEADS * nb
    own = [(i, i, nb) for i in range(nb)]
    past = [(i, n, n) for i in range(nb) for n in range(i)]
    own_tab = _tick_table(own, len(own), (nb - 1, nb - 1, nb + 1))
    past_tab = _tick_table(past, len(past), (nb - 1, 0, nb + 1))
    tile_f32 = pltpu.VMEM((ATT_HEADS, blk, blk), F32)
    tile_bf16 = pltpu.VMEM((ATT_HEADS, blk, blk), BF16)
    row_f32 = pltpu.VMEM((ATT_HEADS, 1, blk), F32)
    part_f32 = pltpu.VMEM((ATT_HEADS, SUBLANES, blk), F32)
    whole = lambda b, *_: (b, 0, 0, 0)
    return pl.pallas_call(
        functools.partial(_moba_kernel, nb=nb, own_ticks=len(own), past_ticks=len(past)),
        out_shape=jax.ShapeDtypeStruct((bsz, nb, width, blk), BF16),
        grid_spec=pltpu.PrefetchScalarGridSpec(
            num_scalar_prefetch=2,
            grid=(bsz,),
            in_specs=[
                pl.BlockSpec((1, nb, width, blk), whole),
                pl.BlockSpec((1, nb, blk, width), whole),
                pl.BlockSpec((1, nb, width, blk), whole),
            ],
            out_specs=pl.BlockSpec((1, nb, width, blk), whole),
            scratch_shapes=[
                pltpu.VMEM((hn, width), BF16),
                pltpu.VMEM((hn, width), BF16),
                pltpu.VMEM((hn, width), BF16),
                pltpu.VMEM((nb, ATT_HEADS, nb + SUBLANES, blk), F32),
                pltpu.VMEM((nb, ATT_HEADS, 1, blk), F32),
                pltpu.VMEM((nb, ATT_HEADS, ATT_HEAD_DIM + 2 * SUBLANES, blk), F32),
                *([tile_f32] * _SKEW + [part_f32] * _SKEW + [tile_bf16] * _SKEW + [row_f32] * _SKEW),
            ],
        ),
        compiler_params=pltpu.CompilerParams(
            dimension_semantics=("arbitrary",), vmem_limit_bytes=VMEM_LIMIT),
        name="moba_attention",
    )(own_tab, past_tab, qt4, k4, vt4)


def _mlstm_kernel(qc_ref, kc_ref, mv_ref, og_ref, gate_ref, bias_ref, gain_ref, triu_ref, select_ref,
                  o_ref, ct_ref, m_ref):
    @pl.when(pl.program_id(1) == 0)
    def _():
        ct_ref[...] = jnp.zeros_like(ct_ref)
        m_ref[...] = jnp.zeros_like(m_ref)

    L = MLSTM_CHUNK
    for j in range(MLSTM_CHUNKS_PER_STEP):
        ts = slice(j * L, (j + 1) * L)
        _mlstm_chunk(qc_ref.at[:, ts, :], kc_ref.at[:, ts, :], mv_ref.at[:, ts, :], og_ref.at[:, ts, :],
                     gate_ref.at[:, ts], bias_ref, gain_ref, triu_ref, select_ref,
                     o_ref.at[:, ts, :], ct_ref, m_ref)


def _mlstm_chunk(qc_ref, kc_ref, mv_ref, og_ref, gate_ref, bias_ref, gain_ref, triu_ref, select_ref,
                 o_ref, ct_ref, m_ref):
    L = MLSTM_CHUNK
    hd = MLSTM_HEAD_DIM
    nh = MLSTM_HEADS

    gr = gate_ref[...] + bias_ref[...]
    r_hi, r_mid, r_lo = _split3(_log_sigmoid(gr) * LOG2_E)
    triu = triu_ref[...]
    bcum = _dot(r_hi, triu) + _dot(r_mid, triu) + _dot(r_lo, triu)
    row = lax.broadcasted_iota(jnp.int32, gr.shape, 0)
    rows_ib = jnp.where(row < nh, gr * LOG2_E, bcum)
    rows_c = rows_ib[0:nh, :] - rows_ib[nh:2 * nh, :]
    lhs_t = jnp.concatenate(list(_split3(rows_ib)) + [jnp.ones((GATE_ROWS, L), BF16)], axis=0)

    t_iota = lax.broadcasted_iota(jnp.int32, (L, LANES), 0)
    s_iota = lax.broadcasted_iota(jnp.int32, (L, LANES), 1)
    zeros_tail = jnp.zeros((GATE_ROWS, 2 * LANES), BF16)

    gain = gain_ref[...]
    heads = range(MLSTM_HEADS)
    slices = [slice(h * hd, (h + 1) * hd) for h in heads]
    srow = lax.broadcasted_iota(jnp.int32, (GATE_ROWS, L), 0)

    res_all, qk_all, inter_all = [], [], []
    for h in heads:
        c = rows_c[h:h + 1, :]
        c_hi = c.astype(BF16).astype(F32)
        c_mid = (c - c_hi).astype(BF16).astype(F32)
        c_lo = c - c_hi - c_mid
        tail = jnp.where(srow == 0, c_hi, jnp.where(srow == 1, c_mid, jnp.where(srow == 2, c_lo, 0.0)))
        sel_h = jnp.concatenate(
            [select_ref[h], jnp.concatenate([tail.astype(BF16), zeros_tail], axis=1)], axis=0)
        res_all.append(lax.dot_general(lhs_t, sel_h, (((0,), (0,)), ((), ())), preferred_element_type=F32))
        qb = qc_ref[0, :, slices[h]]
        kb = kc_ref[0, :, slices[h]]
        qk_all.append(lax.dot_general(qb, kb, (((1,), (1,)), ((), ())), preferred_element_type=F32))
        inter_all.append(_dot(qb, ct_ref[h].astype(BF16)))

    s_all, wv_all, carry_all = [], [], []
    for h in heads:
        sl = slices[h]
        res = res_all[h]
        qk = qk_all[h]
        v = mv_ref[0, :, sl]
        m_prev = m_ref[h]
        b_col = res[:, L:L + LANES]
        i_col = res[:, L + LANES:L + 2 * LANES]

        dslabs = []
        for j in range(L // LANES):
            keep = (s_iota + j * LANES) <= t_iota
            dslabs.append(jnp.where(keep, res[:, j * LANES:(j + 1) * LANES], -jnp.inf))
        dmax = jnp.max(dslabs[0], axis=-1, keepdims=True)
        for d in dslabs[1:]:
            dmax = jnp.maximum(dmax, jnp.max(d, axis=-1, keepdims=True))
        inter = b_col + m_prev
        m_t = jnp.maximum(inter, dmax)
        w_inter = jnp.exp2(inter - m_t)
        s = jnp.concatenate([qk[:, j * LANES:(j + 1) * LANES] * jnp.exp2(d - m_t)
                             for j, d in enumerate(dslabs)], axis=1)

        b_last = b_col[L - 1:L, :]
        g = b_last - b_col + i_col
        m_new = jnp.maximum(b_last + m_prev, jnp.max(g, axis=0, keepdims=True))
        decay = jnp.exp2(b_last + m_prev - m_new)
        w = jnp.exp2(g - m_new)
        m_ref[h] = m_new
        s_all.append(s.astype(BF16))
        wv_all.append(jnp.concatenate([w * v.astype(F32), w], axis=1).astype(BF16))
        carry_all.append((w_inter, jnp.exp2(-m_t), jnp.concatenate([decay, decay], axis=1)))

    ones_cols = jnp.ones((L, hd), BF16)
    intra_all = [_dot(s_all[h], jnp.concatenate([mv_ref[0, :, slices[h]], ones_cols], axis=1))
                 for h in heads]
    for h in heads:
        ct_ref[h] = carry_all[h][2] * ct_ref[h] + lax.dot_general(
            kc_ref[0, :, slices[h]], wv_all[h], (((0,), (0,)), ((), ())), preferred_element_type=F32)

    for h in heads:
        sl = slices[h]
        w_inter, floor, _ = carry_all[h]
        num = w_inter * inter_all[h][:, :hd] + intra_all[h][:, :hd]
        nq = w_inter * inter_all[h][:, hd:] + intra_all[h][:, hd:]
        hh = num / jnp.maximum(jnp.abs(nq), floor)
        mu = jnp.mean(hh, axis=-1, keepdims=True)
        var = jnp.mean(jnp.square(hh - mu), axis=-1, keepdims=True)
        hn = (hh - mu) * lax.rsqrt(var + NORM_EPS) * gain[:, sl]
        o_ref[0, :, sl] = (og_ref[0, :, sl].astype(F32) * hn).astype(o_ref.dtype)


def _mlstm(qc, kc, mv, og, gates_t, gate_bias, out_g):
    bsz, seq, width = qc.shape
    L = MLSTM_CHUNK
    step = L * MLSTM_CHUNKS_PER_STEP
    nc = seq // step
    nh = MLSTM_HEADS
    triu = jnp.triu(jnp.ones((L, L), F32)).astype(BF16)
    r_in = jnp.arange(3 * GATE_ROWS)[None, :, None] % GATE_ROWS
    col = jnp.arange(L + 2 * LANES)[None, None, :]
    head = jnp.arange(nh)[:, None, None]
    select = jnp.where(col < L + LANES, r_in == nh + head, r_in == head).astype(BF16)
    tok = lambda b, c: (b, c, 0)
    fixed = lambda b, c: (0, 0)
    wide = pl.BlockSpec((1, step, width), tok)
    return pl.pallas_call(
        _mlstm_kernel,
        out_shape=jax.ShapeDtypeStruct((bsz, seq, width), BF16),
        grid=(bsz, nc),
        in_specs=[
            wide, wide, wide, wide,
            pl.BlockSpec((GATE_ROWS, step), lambda b, c: (0, b * nc + c)),
            pl.BlockSpec((GATE_ROWS, 1), fixed),
            pl.BlockSpec(out_g.shape, fixed),
            pl.BlockSpec((L, L), fixed),
            pl.BlockSpec(select.shape, lambda b, c: (0, 0, 0)),
        ],
        out_specs=wide,
        scratch_shapes=[
            pltpu.VMEM((nh, MLSTM_HEAD_DIM, 2 * MLSTM_HEAD_DIM), F32),
            pltpu.VMEM((nh, 1, LANES), F32),
        ],
        compiler_params=pltpu.CompilerParams(
            dimension_semantics=("arbitrary", "arbitrary"), vmem_limit_bytes=VMEM_LIMIT),
        name="mlstm",
    )(qc, kc, mv, og, gates_t, gate_bias, out_g, triu, select)


def _out_kernel(x_ref, o_ref, sz_ref, ym_ref, ag_ref, w_ref, pg_ref, out_ref):
    o = jnp.concatenate([o_ref[0, j].astype(F32).T for j in range(o_ref.shape[1])], axis=0)
    ms = jnp.mean(o * o, axis=-1, keepdims=True)
    ya = (o * lax.rsqrt(ms + NORM_EPS)) * ag_ref[...] * sz_ref[...].astype(F32)
    y = _dot(ya.astype(BF16), w_ref[:ATT_WIDTH, :]) + _dot(ym_ref[...], w_ref[ATT_WIDTH:, :])
    ms2 = jnp.mean(y * y, axis=-1, keepdims=True)
    out_ref[...] = x_ref[...] + (y * lax.rsqrt(ms2 + NORM_EPS)) * pg_ref[...]


def _output_projection(x2, ot4, sz, ym, attn_g, w_out, post_g):
    n_tok, d_model = x2.shape
    tm = OUT_ROWS
    per_tile = tm // MOBA_BLOCK
    tiles_per_seq = ot4.shape[1] // per_tile
    row = lambda i: (i, 0)
    fixed = lambda i: (0, 0)
    return pl.pallas_call(
        _out_kernel,
        out_shape=jax.ShapeDtypeStruct((n_tok, d_model), F32),
        grid=(n_tok // tm,),
        in_specs=[
            pl.BlockSpec((tm, d_model), row),
            pl.BlockSpec((1, per_tile, ATT_WIDTH, MOBA_BLOCK),
                         lambda i: (i // tiles_per_seq, i % tiles_per_seq, 0, 0)),
            pl.BlockSpec((tm, ATT_WIDTH), row),
            pl.BlockSpec((tm, MLSTM_WIDTH), row),
            pl.BlockSpec((1, ATT_WIDTH), fixed),
            pl.BlockSpec(w_out.shape, fixed),
            pl.BlockSpec((1, d_model), fixed),
        ],
        out_specs=pl.BlockSpec((tm, d_model), row),
        compiler_params=pltpu.CompilerParams(
            dimension_semantics=("arbitrary",), vmem_limit_bytes=VMEM_LIMIT),
        name="output_projection",
    )(x2, ot4, sz, ym, attn_g, w_out, post_g)


def _layer(x, pre_g, w_in, i_bias, f_bias, conv_w, conv_b, attn_g, mlstm_g, w_out, post_g):
    bsz, seq, d_model = x.shape
    assert seq % MOBA_BLOCK == 0 and seq % PROJ_ROWS == 0
    assert seq % (MLSTM_CHUNK * MLSTM_CHUNKS_PER_STEP) == 0
    nb = seq // MOBA_BLOCK
    n_tok = bsz * seq
    assert seq % OUT_ROWS == 0
    n_main = 4 * ATT_WIDTH + 5 * MLSTM_WIDTH
    n_gate = 2 * MLSTM_HEADS
    assert w_in.shape == (d_model, n_main + n_gate)

    w_all = w_in.astype(BF16)
    w_gate = jnp.pad(w_all[:, n_main:], ((0, 0), (0, GATE_ROWS - n_gate)))
    x2 = x.reshape(n_tok, d_model)
    (qt4, k, vt4, sz, qc, kc, mv, og, gates_t) = _input_projection(
        x2, pre_g[None, :], w_all, n_main, w_gate, _rope_tables(seq), conv_w, conv_b[None, :], seq)

    ot4 = _moba_attention(qt4, k.reshape(bsz, nb, MOBA_BLOCK, ATT_WIDTH), vt4)

    gate_bias = jnp.pad(jnp.concatenate([i_bias, f_bias]), (0, GATE_ROWS - n_gate))[:, None]
    shp = (bsz, seq, MLSTM_WIDTH)
    ym = _mlstm(qc.reshape(shp), kc.reshape(shp), mv.reshape(shp), og.reshape(shp),
                gates_t, gate_bias, mlstm_g[None, :])

    out = _output_projection(x2, ot4, sz, ym.reshape(n_tok, MLSTM_WIDTH), attn_g[None, :],
                             w_out.astype(BF16), post_g[None, :])
    return out.reshape(bsz, seq, d_model)


def kernel(x, pre_norm_g, w_in, mlstm_i_bias, mlstm_f_bias, conv_w, conv_b, attn_out_g, mlstm_out_g,
           w_out, post_norm_g):
    for l in range(pre_norm_g.shape[0]):
        x = _layer(x, pre_norm_g[l], w_in[l], mlstm_i_bias[l], mlstm_f_bias[l], conv_w[l], conv_b[l],
                   attn_out_g[l], mlstm_out_g[l], w_out[l], post_norm_g[l])
    return x
```

```python
import functools
import math

import jax
import jax.numpy as jnp
from jax import lax
from jax.experimental import pallas as pl
from jax.experimental.pallas import tpu as pltpu

F32 = jnp.float32
BF16 = jnp.bfloat16

ATT_HEADS = 8
ATT_HEAD_DIM = 64
ATT_WIDTH = ATT_HEADS * ATT_HEAD_DIM
ROT_DIM = ATT_HEAD_DIM // 4
ROPE_THETA = 500000.0
MOBA_BLOCK = 256
MOBA_TOPK = 3
MLSTM_HEADS = 4
MLSTM_HEAD_DIM = 128
MLSTM_WIDTH = MLSTM_HEADS * MLSTM_HEAD_DIM
CONV_WIDTH = 4
NORM_EPS = 1e-6

LANES = 128
SUBLANES = 8
GATE_ROWS = 16
PROJ_ROWS = 512
PROJ_EPILOGUE_ROWS = 32
OUT_ROWS = 1024
MLSTM_CHUNK = 256
MLSTM_CHUNKS_PER_STEP = 4
NEG_BIG = -1e30
LOG2_E = 1.4426950408889634
_SKEW = 2
VMEM_LIMIT = 56 * 1024 * 1024


def _sigmoid(x):
    return 1.0 / (1.0 + jnp.exp(-x))


def _silu(x):
    return x * _sigmoid(x)


def _log_sigmoid(x):
    return jnp.minimum(x, 0.0) - jnp.log(1.0 + jnp.exp(-jnp.abs(x)))


def _split3(x):
    hi = x.astype(BF16)
    r1 = x - hi.astype(F32)
    mid = r1.astype(BF16)
    lo = (r1 - mid.astype(F32)).astype(BF16)
    return hi, mid, lo


def _dot(a, b):
    return jnp.dot(a, b, preferred_element_type=F32)


def _proj_kernel(x_ref, g_ref, w_ref, wg_ref, cos_ref, sa_ref, sb_ref, cw_ref, cb_ref,
                 qt_ref, k_ref, vt_ref, sz_ref, qc_ref, kc_ref, mv_ref, og_ref, gate_ref,
                 pq_ref, pk_ref, acc0_ref, acc1_ref, gt0_ref, gt1_ref, *, seq_tiles, n_tiles):
    s = pl.program_id(0)
    first_of_seq = ((s - 1) % seq_tiles) == 0
    tm = x_ref.shape[0]
    halo = SUBLANES
    n_groups = w_ref.shape[1] // ATT_WIDTH

    def matmul_parts(acc_ref, gt_ref):
        x = x_ref[...]
        ms = jnp.mean(x * x, axis=-1, keepdims=True)
        hb = ((x * lax.rsqrt(ms + NORM_EPS)) * g_ref[...]).astype(BF16)

        def group(gi):
            acc_ref[gi] = _dot(hb, w_ref[:, gi * ATT_WIDTH:(gi + 1) * ATT_WIDTH])

        def gates():
            gt_ref[...] = lax.dot_general(wg_ref[...], hb, (((0,), (1,)), ((), ())),
                                          preferred_element_type=F32)

        return [functools.partial(group, gi) for gi in range(n_groups)] + [gates]

    def rope(acc, rs):
        cos = cos_ref[rs, :]
        sa = sa_ref[rs, :]
        sb = sb_ref[rs, :]
        parts = []
        for c in range(ATT_WIDTH // LANES):
            xs = acc[:, c * LANES:(c + 1) * LANES]
            parts.append(xs * cos
                         + pltpu.roll(xs, LANES - ROT_DIM // 2, 1) * sa
                         + pltpu.roll(xs, ROT_DIM // 2, 1) * sb)
        return jnp.concatenate(parts, axis=1)

    def conv_silu(acc_ref, gi, r, prev_ref, w, b):
        rows = PROJ_EPILOGUE_ROWS
        if r == 0:
            prev = jnp.where(first_of_seq, 0.0, prev_ref[...])
        else:
            prev = acc_ref[gi, r - halo:r, :]
        ext = jnp.concatenate([prev, acc_ref[gi, r:r + rows, :]], axis=0)
        out = b
        for j in range(CONV_WIDTH):
            off = halo - (CONV_WIDTH - 1) + j
            out = out + w[j:j + 1, :] * ext[off:off + rows, :]
        return _silu(out)

    def epilogue_parts(acc_ref, gt_ref):
        cw = cw_ref[...]
        cb = cb_ref[...]

        def chunk(r):
            rs = slice(r, r + PROJ_EPILOGUE_ROWS)
            acc_ref[0, rs, :] = rope(acc_ref[0, rs, :], rs) * (LOG2_E * ATT_HEAD_DIM ** -0.5)
            k_ref[rs, :] = rope(acc_ref[1, rs, :], rs).astype(k_ref.dtype)
            sz_ref[rs, :] = _silu(acc_ref[3, rs, :]).astype(sz_ref.dtype)
            qc = conv_silu(acc_ref, 4, r, pq_ref, cw[:, :MLSTM_WIDTH], cb[:, :MLSTM_WIDTH])
            qc_ref[rs, :] = qc.astype(qc_ref.dtype)
            kc = conv_silu(acc_ref, 5, r, pk_ref, cw[:, MLSTM_WIDTH:], cb[:, MLSTM_WIDTH:])
            kc_ref[rs, :] = (kc * (MLSTM_HEAD_DIM ** -0.5)).astype(kc_ref.dtype)
            mv_ref[rs, :] = acc_ref[6, rs, :].astype(mv_ref.dtype)
            og_ref[rs, :] = (_sigmoid(acc_ref[7, rs, :]) * _silu(acc_ref[8, rs, :])).astype(og_ref.dtype)

        def tail():
            pq_ref[...] = acc_ref[4, tm - halo:, :]
            pk_ref[...] = acc_ref[5, tm - halo:, :]
            gate_ref[...] = gt_ref[...]

        def key_block(j):
            rows = slice(j * MOBA_BLOCK, (j + 1) * MOBA_BLOCK)
            qt_ref[0, j] = acc_ref[0, rows, :].T.astype(qt_ref.dtype)
            vt_ref[0, j] = acc_ref[2, rows, :].T.astype(vt_ref.dtype)

        parts = []
        for r in range(0, tm, PROJ_EPILOGUE_ROWS):
            parts.append(functools.partial(chunk, r))
            if (r + PROJ_EPILOGUE_ROWS) % MOBA_BLOCK == 0:
                parts.append(functools.partial(key_block, r // MOBA_BLOCK))
        return parts + [tail]

    def step(mat_bufs, epi_bufs):
        mats = matmul_parts(*mat_bufs) if mat_bufs is not None else []
        epis = epilogue_parts(*epi_bufs) if epi_bufs is not None else []
        per_mat = -(-len(epis) // max(len(mats), 1))
        for part in mats:
            part()
            for e in epis[:per_mat]:
                e()
            epis = epis[per_mat:]
        for e in epis:
            e()

    bufs = ((acc0_ref, gt0_ref), (acc1_ref, gt1_ref))

    @pl.when(s == 0)
    def _():
        step(bufs[0], None)

    for parity in range(2):
        @pl.when((s > 0) & (s < n_tiles) & (s % 2 == parity))
        def _():
            step(bufs[parity], bufs[1 - parity])

    @pl.when(s == n_tiles)
    def _():
        step(None, bufs[(n_tiles - 1) % 2])


def _rope_tables(seq):
    half = ROT_DIM // 2
    inv_freq = jnp.power(ROPE_THETA, -jnp.arange(half, dtype=F32) * 2.0 / ROT_DIM)
    ang = jnp.arange(seq, dtype=jnp.int32).astype(F32)[:, None] * inv_freq[None, :]
    cos = jnp.cos(ang)
    sin = jnp.sin(ang)
    ones = jnp.ones((seq, ATT_HEAD_DIM - ROT_DIM), F32)
    zeros = jnp.zeros((seq, ATT_HEAD_DIM - ROT_DIM), F32)
    zh = jnp.zeros((seq, half), F32)
    cos_h = jnp.concatenate([cos, cos, ones], axis=1)
    sa_h = jnp.concatenate([-sin, zh, zeros], axis=1)
    sb_h = jnp.concatenate([zh, sin, zeros], axis=1)
    rep = LANES // ATT_HEAD_DIM
    return (jnp.tile(cos_h, (1, rep)), jnp.tile(sa_h, (1, rep)), jnp.tile(sb_h, (1, rep)))


def _input_projection(x2, pre_g, w_all, n_main, w_gate, tables, conv_w, conv_b, seq):
    n_tok, d_model = x2.shape
    tm = PROJ_ROWS
    seq_tiles = seq // tm
    n_tiles = n_tok // tm
    n_groups = n_main // ATT_WIDTH
    row_in = lambda s: (jnp.minimum(s, n_tiles - 1), 0)
    row_out = lambda s: (jnp.maximum(s - 1, 0), 0)
    fixed = lambda s: (0, 0)
    tab = lambda s: (jnp.maximum(s - 1, 0) % seq_tiles, 0)
    done = lambda s: jnp.maximum(s - 1, 0)
    wide = jax.ShapeDtypeStruct((n_tok, ATT_WIDTH), BF16)
    wide_spec = pl.BlockSpec((tm, ATT_WIDTH), row_out)
    blocked = jax.ShapeDtypeStruct((n_tok // seq, seq // MOBA_BLOCK, ATT_WIDTH, MOBA_BLOCK), BF16)
    blocked_spec = pl.BlockSpec((1, tm // MOBA_BLOCK, ATT_WIDTH, MOBA_BLOCK),
                                lambda s: (done(s) // seq_tiles, done(s) % seq_tiles, 0, 0))
    return pl.pallas_call(
        functools.partial(_proj_kernel, seq_tiles=seq_tiles, n_tiles=n_tiles),
        out_shape=(blocked, wide, blocked) + (wide,) * 5 + (jax.ShapeDtypeStruct((GATE_ROWS, n_tok), F32),),
        grid=(n_tiles + 1,),
        in_specs=[
            pl.BlockSpec((tm, d_model), row_in),
            pl.BlockSpec((1, d_model), fixed),
            pl.BlockSpec((d_model, n_main), fixed),
            pl.BlockSpec(w_gate.shape, fixed),
            pl.BlockSpec((tm, LANES), tab),
            pl.BlockSpec((tm, LANES), tab),
            pl.BlockSpec((tm, LANES), tab),
            pl.BlockSpec(conv_w.shape, fixed),
            pl.BlockSpec(conv_b.shape, fixed),
        ],
        out_specs=(blocked_spec, wide_spec, blocked_spec) + (wide_spec,) * 5
        + (pl.BlockSpec((GATE_ROWS, tm), lambda s: (0, done(s))),),
        scratch_shapes=[
            pltpu.VMEM((SUBLANES, MLSTM_WIDTH), F32),
            pltpu.VMEM((SUBLANES, MLSTM_WIDTH), F32),
            pltpu.VMEM((n_groups, tm, ATT_WIDTH), F32),
            pltpu.VMEM((n_groups, tm, ATT_WIDTH), F32),
            pltpu.VMEM((GATE_ROWS, tm), F32),
            pltpu.VMEM((GATE_ROWS, tm), F32),
        ],
        compiler_params=pltpu.CompilerParams(
            dimension_semantics=("arbitrary",), vmem_limit_bytes=VMEM_LIMIT),
        name="input_projection",
    )(x2, pre_g, w_all, w_gate, *tables, conv_w, conv_b)


def _moba_kernel(own_tab, past_tab, qt_ref, k_ref, vt_ref, o_ref,
                 kst_hi, kst_mid, kst_lo, bias_ref, m_ref, acc_ref,
                 *stage_bufs, nb, own_ticks, past_ticks):
    blk = MOBA_BLOCK
    hd = ATT_HEAD_DIM

    lane_head = lax.broadcasted_iota(jnp.int32, (nb, ATT_WIDTH), 1) // hd
    means = []
    for n in range(nb):
        means.append(jnp.mean(k_ref[0, n].astype(F32), axis=0, keepdims=True))
    km = jnp.concatenate(means, axis=0)
    for h in range(ATT_HEADS):
        hi, mid, lo = _split3(jnp.where(lane_head == h, km, 0.0))
        kst_hi[h * nb:(h + 1) * nb, :] = hi
        kst_mid[h * nb:(h + 1) * nb, :] = mid
        kst_lo[h * nb:(h + 1) * nb, :] = lo

    n_iota = lax.broadcasted_iota(jnp.int32, (nb, blk), 0)
    n_iota_f = n_iota.astype(F32)
    tail_iota = lax.broadcasted_iota(jnp.int32, (SUBLANES, blk), 0)
    tail_rows = jnp.where(tail_iota == 0, 0.0, NEG_BIG)

    def prepare(i, carry):
        qt = qt_ref[0, i]
        gates = _dot(kst_hi[...], qt) + _dot(kst_mid[...], qt) + _dot(kst_lo[...], qt)
        past = n_iota < i
        for h in range(ATT_HEADS):
            g = jnp.where(past, gates[h * nb:(h + 1) * nb, :], -jnp.inf)
            sel = jnp.zeros((nb, blk), jnp.bool_)
            for _ in range(min(MOBA_TOPK, nb)):
                gmax = jnp.max(g, axis=0, keepdims=True)
                first = jnp.min(jnp.where(g == gmax, n_iota_f, float(nb)), axis=0, keepdims=True)
                hit = n_iota_f == first
                sel = sel | hit
                g = jnp.where(hit, -jnp.inf, g)
            bias_ref[i, h, 0:nb, :] = jnp.where(sel & past, 0.0, NEG_BIG)
            bias_ref[i, h, nb:nb + SUBLANES, :] = tail_rows
            m_ref[i, h] = jnp.full((1, blk), NEG_BIG, F32)
            acc_ref[i, h] = jnp.zeros(acc_ref.shape[2:], F32)
        return carry

    lax.fori_loop(0, nb, prepare, 0, unroll=math.gcd(nb, 4))

    l_iota = lax.broadcasted_iota(jnp.int32, (blk, blk), 0)
    q_iota = lax.broadcasted_iota(jnp.int32, (blk, blk), 1)
    causal = l_iota <= q_iota
    row_head = lax.broadcasted_iota(jnp.int32, (LANES, blk), 0) // hd
    ones_rows = jnp.ones((2 * SUBLANES, blk), BF16)
    s_bufs, c_bufs, p_bufs, a_bufs = (stage_bufs[j * _SKEW:(j + 1) * _SKEW] for j in range(4))

    def run(tab, n_ticks, own):
        def scores(tick, slot):
            qi = tab[0, tick]
            kb = k_ref[0, tab[1, tick]]
            for h in range(ATT_HEADS):
                pr = slice((h // 2) * LANES, (h // 2 + 1) * LANES)
                qpair = qt_ref[0, qi, pr, :]
                qh = jnp.where(row_head == (h % 2), qpair, jnp.zeros_like(qpair))
                st = _dot(kb[:, pr], qh)
                st = jnp.where(causal, st, NEG_BIG) if own else st
                s_bufs[slot][h] = st
                c_bufs[slot][h] = jnp.max(st.reshape(blk // SUBLANES, SUBLANES, blk), axis=0)

        def softmax_update(tick, slot):
            qi = tab[0, tick]
            mask_row = tab[2, tick]
            for h in range(ATT_HEADS):
                st = s_bufs[slot][h]
                b = bias_ref[qi, h, pl.ds(mask_row, 1), :]
                m_old = m_ref[qi, h]
                m_new = jnp.maximum(m_old, jnp.max(c_bufs[slot][h], axis=0, keepdims=True) + b)
                p_bufs[slot][h] = jnp.exp2(st - (m_new - b)).astype(BF16)
                a_bufs[slot][h] = jnp.exp2(m_old - m_new)
                m_ref[qi, h] = m_new

        def values(tick, slot):
            qi = tab[0, tick]
            vtb = vt_ref[0, tab[1, tick]]
            for h in range(ATT_HEADS):
                lhs = jnp.concatenate([vtb[h * hd:(h + 1) * hd, :], ones_rows], axis=0)
                acc_ref[qi, h] = a_bufs[slot][h] * acc_ref[qi, h] + _dot(lhs, p_bufs[slot][h])

        for t in range(_SKEW):
            scores(t, t % _SKEW)
        for t in range(_SKEW, 2 * _SKEW):
            softmax_update(t - _SKEW, t % _SKEW)
            scores(t, t % _SKEW)

        def body(u, carry):
            t = _SKEW * (u + 2)
            for d in range(_SKEW):
                values(t + d - 2 * _SKEW, d)
                softmax_update(t + d - _SKEW, d)
                scores(t + d, d)
            return carry

        lax.fori_loop(0, -(-n_ticks // _SKEW), body, 0)

    run(own_tab, own_ticks, True)
    run(past_tab, past_ticks, False)

    def finalize(i, carry):
        for h in range(ATT_HEADS):
            acc = acc_ref[i, h]
            o_ref[0, i, h * hd:(h + 1) * hd, :] = (acc[:hd, :] / acc[hd:hd + 1, :]).astype(o_ref.dtype)
        return carry

    lax.fori_loop(0, nb, finalize, 0)


def _tick_table(ticks, n_ticks, pad):
    length = _SKEW * (-(-n_ticks // _SKEW) + 2)
    rows = list(ticks) + [pad] * (length - len(ticks))
    return jnp.asarray(rows, jnp.int32).T


def _moba_attention(qt4, k4, vt4):
    bsz, nb, width, blk = qt4.shape
    hn = ATT_HEADS * nb
    own = [(i, i, nb) for i in range(nb)]
    past = [(i, n, n) for i in range(nb) for n in range(i)]
    own_tab = _tick_table(own, len(own), (nb - 1, nb - 1, nb + 1))
    past_tab = _tick_table(past, len(past), (nb - 1, 0, nb + 1))
    tile_f32 = pltpu.VMEM((ATT_HEADS, blk, blk), F32)
    tile_bf16 = pltpu.VMEM((ATT_HEADS, blk, blk), BF16)
    row_f32 = pltpu.VMEM((ATT_HEADS, 1, blk), F32)
    part_f32 = pltpu.VMEM((ATT_HEADS, SUBLANES, blk), F32)
    whole = lambda b, *_: (b, 0, 0, 0)
    return pl.pallas_call(
        functools.partial(_moba_kernel, nb=nb, own_ticks=len(own), past_ticks=len(past)),
        out_shape=jax.ShapeDtypeStruct((bsz, nb, width, blk), BF16),
        grid_spec=pltpu.PrefetchScalarGridSpec(
            num_scalar_prefetch=2,
            grid=(bsz,),
            in_specs=[
                pl.BlockSpec((1, nb, width, blk), whole),
                pl.BlockSpec((1, nb, blk, width), whole),
                pl.BlockSpec((1, nb, width, blk), whole),
            ],
            out_specs=pl.BlockSpec((1, nb, width, blk), whole),
            scratch_shapes=[
                pltpu.VMEM((hn, width), BF16),
                pltpu.VMEM((hn, width), BF16),
                pltpu.VMEM((hn, width), BF16),
                pltpu.VMEM((nb, ATT_HEADS, nb + SUBLANES, blk), F32),
                pltpu.VMEM((nb, ATT_HEADS, 1, blk), F32),
                pltpu.VMEM((nb, ATT_HEADS, ATT_HEAD_DIM + 2 * SUBLANES, blk), F32),
                *([tile_f32] * _SKEW + [part_f32] * _SKEW + [tile_bf16] * _SKEW + [row_f32] * _SKEW),
            ],
        ),
        compiler_params=pltpu.CompilerParams(
            dimension_semantics=("arbitrary",), vmem_limit_bytes=VMEM_LIMIT),
        name="moba_attention",
    )(own_tab, past_tab, qt4, k4, vt4)


def _mlstm_kernel(qc_ref, kc_ref, mv_ref, og_ref, gate_ref, bias_ref, gain_ref, triu_ref, select_ref,
                  o_ref, ct_ref, m_ref):
    @pl.when(pl.program_id(1) == 0)
    def _():
        ct_ref[...] = jnp.zeros_like(ct_ref)
        m_ref[...] = jnp.zeros_like(m_ref)

    L = MLSTM_CHUNK
    for j in range(MLSTM_CHUNKS_PER_STEP):
        ts = slice(j * L, (j + 1) * L)
        _mlstm_chunk(qc_ref.at[:, ts, :], kc_ref.at[:, ts, :], mv_ref.at[:, ts, :], og_ref.at[:, ts, :],
                     gate_ref.at[:, ts], bias_ref, gain_ref, triu_ref, select_ref,
                     o_ref.at[:, ts, :], ct_ref, m_ref)


def _mlstm_chunk(qc_ref, kc_ref, mv_ref, og_ref, gate_ref, bias_ref, gain_ref, triu_ref, select_ref,
                 o_ref, ct_ref, m_ref):
    L = MLSTM_CHUNK
    hd = MLSTM_HEAD_DIM
    nh = MLSTM_HEADS

    gr = gate_ref[...] + bias_ref[...]
    r_hi, r_mid, r_lo = _split3(_log_sigmoid(gr) * LOG2_E)
    triu = triu_ref[...]
    bcum = _dot(r_hi, triu) + _dot(r_mid, triu) + _dot(r_lo, triu)
    row = lax.broadcasted_iota(jnp.int32, gr.shape, 0)
    rows_ib = jnp.where(row < nh, gr * LOG2_E, bcum)
    rows_c = rows_ib[0:nh, :] - rows_ib[nh:2 * nh, :]
    lhs_t = jnp.concatenate(list(_split3(rows_ib)) + [jnp.ones((GATE_ROWS, L), BF16)], axis=0)

    t_iota = lax.broadcasted_iota(jnp.int32, (L, LANES), 0)
    s_iota = lax.broadcasted_iota(jnp.int32, (L, LANES), 1)
    zeros_tail = jnp.zeros((GATE_ROWS, 2 * LANES), BF16)

    gain = gain_ref[...]
    heads = range(MLSTM_HEADS)
    slices = [slice(h * hd, (h + 1) * hd) for h in heads]
    srow = lax.broadcasted_iota(jnp.int32, (GATE_ROWS, L), 0)

    res_all, qk_all, inter_all = [], [], []
    for h in heads:
        c = rows_c[h:h + 1, :]
        c_hi = c.astype(BF16).astype(F32)
        c_mid = (c - c_hi).astype(BF16).astype(F32)
        c_lo = c - c_hi - c_mid
        tail = jnp.where(srow == 0, c_hi, jnp.where(srow == 1, c_mid, jnp.where(srow == 2, c_lo, 0.0)))
        sel_h = jnp.concatenate(
            [select_ref[h], jnp.concatenate([tail.astype(BF16), zeros_tail], axis=1)], axis=0)
        res_all.append(lax.dot_general(lhs_t, sel_h, (((0,), (0,)), ((), ())), preferred_element_type=F32))
        qb = qc_ref[0, :, slices[h]]
        kb = kc_ref[0, :, slices[h]]
        qk_all.append(lax.dot_general(qb, kb, (((1,), (1,)), ((), ())), preferred_element_type=F32))
        inter_all.append(_dot(qb, ct_ref[h].astype(BF16)))

    s_all, wv_all, carry_all = [], [], []
    for h in heads:
        sl = slices[h]
        res = res_all[h]
        qk = qk_all[h]
        v = mv_ref[0, :, sl]
        m_prev = m_ref[h]
        b_col = res[:, L:L + LANES]
        i_col = res[:, L + LANES:L + 2 * LANES]

        dslabs = []
        for j in range(L // LANES):
            keep = (s_iota + j * LANES) <= t_iota
            dslabs.append(jnp.where(keep, res[:, j * LANES:(j + 1) * LANES], -jnp.inf))
        dmax = jnp.max(dslabs[0], axis=-1, keepdims=True)
        for d in dslabs[1:]:
            dmax = jnp.maximum(dmax, jnp.max(d, axis=-1, keepdims=True))
        inter = b_col + m_prev
        m_t = jnp.maximum(inter, dmax)
        w_inter = jnp.exp2(inter - m_t)
        s = jnp.concatenate([qk[:, j * LANES:(j + 1) * LANES] * jnp.exp2(d - m_t)
                             for j, d in enumerate(dslabs)], axis=1)

        b_last = b_col[L - 1:L, :]
        g = b_last - b_col + i_col
        m_new = jnp.maximum(b_last + m_prev, jnp.max(g, axis=0, keepdims=True))
        decay = jnp.exp2(b_last + m_prev - m_new)
        w = jnp.exp2(g - m_new)
        m_ref[h] = m_new
        s_all.append(s.astype(BF16))
        wv_all.append(jnp.concatenate([w * v.astype(F32), w], axis=1).astype(BF16))
        carry_all.append((w_inter, jnp.exp2(-m_t), jnp.concatenate([decay, decay], axis=1)))

    ones_cols = jnp.ones((L, hd), BF16)
    intra_all = [_dot(s_all[h], jnp.concatenate([mv_ref[0, :, slices[h]], ones_cols], axis=1))
                 for h in heads]
    for h in heads:
        ct_ref[h] = carry_all[h][2] * ct_ref[h] + lax.dot_general(
            kc_ref[0, :, slices[h]], wv_all[h], (((0,), (0,)), ((), ())), preferred_element_type=F32)

    for h in heads:
        sl = slices[h]
        w_inter, floor, _ = carry_all[h]
        num = w_inter * inter_all[h][:, :hd] + intra_all[h][:, :hd]
        nq = w_inter * inter_all[h][:, hd:] + intra_all[h][:, hd:]
        hh = num / jnp.maximum(jnp.abs(nq), floor)
        mu = jnp.mean(hh, axis=-1, keepdims=True)
        var = jnp.mean(jnp.square(hh - mu), axis=-1, keepdims=True)
        hn = (hh - mu) * lax.rsqrt(var + NORM_EPS) * gain[:, sl]
        o_ref[0, :, sl] = (og_ref[0, :, sl].astype(F32) * hn).astype(o_ref.dtype)


def _mlstm(qc, kc, mv, og, gates_t, gate_bias, out_g):
    bsz, seq, width = qc.shape
    L = MLSTM_CHUNK
    step = L * MLSTM_CHUNKS_PER_STEP
    nc = seq // step
    nh = MLSTM_HEADS
    triu = jnp.triu(jnp.ones((L, L), F32)).astype(BF16)
    r_in = jnp.arange(3 * GATE_ROWS)[None, :, None] % GATE_ROWS
    col = jnp.arange(L + 2 * LANES)[None, None, :]
    head = jnp.arange(nh)[:, None, None]
    select = jnp.where(col < L + LANES, r_in == nh + head, r_in == head).astype(BF16)
    tok = lambda b, c: (b, c, 0)
    fixed = lambda b, c: (0, 0)
    wide = pl.BlockSpec((1, step, width), tok)
    return pl.pallas_call(
        _mlstm_kernel,
        out_shape=jax.ShapeDtypeStruct((bsz, seq, width), BF16),
        grid=(bsz, nc),
        in_specs=[
            wide, wide, wide, wide,
            pl.BlockSpec((GATE_ROWS, step), lambda b, c: (0, b * nc + c)),
            pl.BlockSpec((GATE_ROWS, 1), fixed),
            pl.BlockSpec(out_g.shape, fixed),
            pl.BlockSpec((L, L), fixed),
            pl.BlockSpec(select.shape, lambda b, c: (0, 0, 0)),
        ],
        out_specs=wide,
        scratch_shapes=[
            pltpu.VMEM((nh, MLSTM_HEAD_DIM, 2 * MLSTM_HEAD_DIM), F32),
            pltpu.VMEM((nh, 1, LANES), F32),
        ],
        compiler_params=pltpu.CompilerParams(
            dimension_semantics=("arbitrary", "arbitrary"), vmem_limit_bytes=VMEM_LIMIT),
        name="mlstm",
    )(qc, kc, mv, og, gates_t, gate_bias, out_g, triu, select)


def _out_kernel(x_ref, o_ref, sz_ref, ym_ref, ag_ref, w_ref, pg_ref, out_ref):
    o = jnp.concatenate([o_ref[0, j].astype(F32).T for j in range(o_ref.shape[1])], axis=0)
    ms = jnp.mean(o * o, axis=-1, keepdims=True)
    ya = (o * lax.rsqrt(ms + NORM_EPS)) * ag_ref[...] * sz_ref[...].astype(F32)
    y = _dot(ya.astype(BF16), w_ref[:ATT_WIDTH, :]) + _dot(ym_ref[...], w_ref[ATT_WIDTH:, :])
    ms2 = jnp.mean(y * y, axis=-1, keepdims=True)
    out_ref[...] = x_ref[...] + (y * lax.rsqrt(ms2 + NORM_EPS)) * pg_ref[...]


def _output_projection(x2, ot4, sz, ym, attn_g, w_out, post_g):
    n_tok, d_model = x2.shape
    tm = OUT_ROWS
    per_tile = tm // MOBA_BLOCK
    tiles_per_seq = ot4.shape[1] // per_tile
    row = lambda i: (i, 0)
    fixed = lambda i: (0, 0)
    return pl.pallas_call(
        _out_kernel,
        out_shape=jax.ShapeDtypeStruct((n_tok, d_model), F32),
        grid=(n_tok // tm,),
        in_specs=[
            pl.BlockSpec((tm, d_model), row),
            pl.BlockSpec((1, per_tile, ATT_WIDTH, MOBA_BLOCK),
                         lambda i: (i // tiles_per_seq, i % tiles_per_seq, 0, 0)),
            pl.BlockSpec((tm, ATT_WIDTH), row),
            pl.BlockSpec((tm, MLSTM_WIDTH), row),
            pl.BlockSpec((1, ATT_WIDTH), fixed),
            pl.BlockSpec(w_out.shape, fixed),
            pl.BlockSpec((1, d_model), fixed),
        ],
        out_specs=pl.BlockSpec((tm, d_model), row),
        compiler_params=pltpu.CompilerParams(
            dimension_semantics=("arbitrary",), vmem_limit_bytes=VMEM_LIMIT),
        name="output_projection",
    )(x2, ot4, sz, ym, attn_g, w_out, post_g)


def _layer(x, pre_g, w_in, i_bias, f_bias, conv_w, conv_b, attn_g, mlstm_g, w_out, post_g):
    bsz, seq, d_model = x.shape
    assert seq % MOBA_BLOCK == 0 and seq % PROJ_ROWS == 0
    assert seq % (MLSTM_CHUNK * MLSTM_CHUNKS_PER_STEP) == 0
    nb = seq // MOBA_BLOCK
    n_tok = bsz * seq
    assert seq % OUT_ROWS == 0
    n_main = 4 * ATT_WIDTH + 5 * MLSTM_WIDTH
    n_gate = 2 * MLSTM_HEADS
    assert w_in.shape == (d_model, n_main + n_gate)

    w_all = w_in.astype(BF16)
    w_gate = jnp.pad(w_all[:, n_main:], ((0, 0), (0, GATE_ROWS - n_gate)))
    x2 = x.reshape(n_tok, d_model)
    (qt4, k, vt4, sz, qc, kc, mv, og, gates_t) = _input_projection(
        x2, pre_g[None, :], w_all, n_main, w_gate, _rope_tables(seq), conv_w, conv_b[None, :], seq)

    ot4 = _moba_attention(qt4, k.reshape(bsz, nb, MOBA_BLOCK, ATT_WIDTH), vt4)

    gate_bias = jnp.pad(jnp.concatenate([i_bias, f_bias]), (0, GATE_ROWS - n_gate))[:, None]
    shp = (bsz, seq, MLSTM_WIDTH)
    ym = _mlstm(qc.reshape(shp), kc.reshape(shp), mv.reshape(shp), og.reshape(shp),
                gates_t, gate_bias, mlstm_g[None, :])

    out = _output_projection(x2, ot4, sz, ym.reshape(n_tok, MLSTM_WIDTH), attn_g[None, :],
                             w_out.astype(BF16), post_g[None, :])
    return out.reshape(bsz, seq, d_model)


def kernel(x, pre_norm_g, w_in, mlstm_i_bias, mlstm_f_bias, conv_w, conv_b, attn_out_g, mlstm_out_g,
           w_out, post_norm_g):
    for l in range(pre_norm_g.shape[0]):
        x = _layer(x, pre_norm_g[l], w_in[l], mlstm_i_bias[l], mlstm_f_bias[l], conv_w[l], conv_b[l],
                   attn_out_g[l], mlstm_out_g[l], w_out[l], post_norm_g[l])
    return x
```
